```python
import math
import functools
import jax
import jax.numpy as jnp
from jax import lax
import numpy as np

D_MODEL = 2048
BATCH = 4
SEQ = 2048
DEPTH = 1
DEC_BATCH = 8
DEC_SEQ = 1
PAST_LEN = 16384
PAGE_SIZE = 128

HEAD_DIM = 128
N_HEADS = D_MODEL // HEAD_DIM
N_KV_HEADS = 4
GROUP = N_HEADS // N_KV_HEADS
ROPE_DIM = HEAD_DIM // 4
IDX_HEADS = 16
IDX_DIM = 64
IDX_ROPE_DIM = IDX_DIM // 4
TOPK_MAX = 256
ROPE_THETA = 500000.0
Q_BLOCK = 128
GDN_DK = 128
GDN_DV = 128
GDN_HEADS = D_MODEL // GDN_DV
CONV_W = 4
CONV_DIM = 2 * GDN_HEADS * GDN_DK + GDN_HEADS * GDN_DV
CHUNK = 64
D_FF = 5632
LN_EPS = 1e-5
NORM_EPS = 1e-6
DN_ALPHA = (2 * DEPTH) ** 0.25
DN_BETA = (8 * DEPTH) ** -0.25

IN_SPLITS = (
    ('q_a', N_HEADS * HEAD_DIM), ('k_a', N_KV_HEADS * HEAD_DIM), ('v_a', N_KV_HEADS * HEAD_DIM),
    ('q_idx', IDX_HEADS * IDX_DIM), ('k_idx', IDX_DIM), ('w_idx', IDX_HEADS),
    ('qkv_b', CONV_DIM), ('a_b', GDN_HEADS), ('beta_b', GDN_HEADS), ('z_b', GDN_HEADS * GDN_DV),
    ('gate_a', D_MODEL), ('gate_b', D_MODEL),
)
IN_COLS = sum(n for _, n in IN_SPLITS)

kernel_name = 'hybrid_dsa_gdn_macaron_step'


def _layer_norm(x, g, b):
    xf = x.astype(jnp.float32)
    mu = jnp.mean(xf, axis=-1, keepdims=True)
    var = jnp.mean(jnp.square(xf - mu), axis=-1, keepdims=True)
    return ((xf - mu) * lax.rsqrt(var + LN_EPS) * g.astype(jnp.float32) + b.astype(jnp.float32)).astype(x.dtype)


def _rms_norm(x, g):
    xf = x.astype(jnp.float32)
    return xf * lax.rsqrt(jnp.mean(jnp.square(xf), axis=-1, keepdims=True) + NORM_EPS) * g.astype(jnp.float32)


def _l2norm(x):
    xf = x.astype(jnp.float32)
    return xf * lax.rsqrt(jnp.sum(jnp.square(xf), axis=-1, keepdims=True) + NORM_EPS)


def _swiglu(x, wg, wu, wd):
    return (jax.nn.silu(x @ wg) * (x @ wu)) @ wd


def _split_columns(u):
    offs = np.cumsum([n for _, n in IN_SPLITS])[:-1].tolist()
    return dict(zip([nm for nm, _ in IN_SPLITS], jnp.split(u, offs, axis=-1)))


def _rope(x, pos, rot_dim):
    half = rot_dim // 2
    inv = ROPE_THETA ** (-jnp.arange(half, dtype=jnp.float32) / half)
    ang = pos.astype(jnp.float32)[:, None] * inv[None, :]
    cos = jnp.cos(ang)[:, None, :]
    sin = jnp.sin(ang)[:, None, :]
    xf = x.astype(jnp.float32)
    x1, x2, rest = xf[..., :half], xf[..., half:rot_dim], xf[..., rot_dim:]
    out = jnp.concatenate([x1 * cos - x2 * sin, x2 * cos + x1 * sin, rest], axis=-1)
    return out.astype(x.dtype)


def _causal_conv(x, buf, w):
    T = x.shape[1]
    xp = jnp.concatenate([buf.astype(x.dtype), x], axis=1)
    y = w[0] * xp[:, 0:T]
    for j in range(1, CONV_W):
        y = y + w[j] * xp[:, j:j + T]
    return jax.nn.silu(y), xp[:, xp.shape[1] - (CONV_W - 1):]


def _gated_delta(q, k, v, g, beta, s0):
    B, T, H, dk = q.shape
    dv = v.shape[-1]
    pad = (-T) % CHUNK
    def padt(a):
        return jnp.pad(a, [(0, 0), (0, pad)] + [(0, 0)] * (a.ndim - 2))
    nc = (T + pad) // CHUNK
    def chunks(a):
        a = padt(a.astype(jnp.float32))
        return jnp.moveaxis(a.reshape((B, nc, CHUNK, H) + a.shape[3:]), 3, 1)
    q, k, v, g, beta = chunks(q), chunks(k), chunks(v), chunks(g), chunks(beta)
    G = jnp.cumsum(g, axis=-1)
    tri_incl = jnp.tril(jnp.ones((CHUNK, CHUNK), dtype=bool))
    tri_strict = jnp.tril(jnp.ones((CHUNK, CHUNK), dtype=bool), -1)
    diff = G[..., :, None] - G[..., None, :]
    decay_incl = jnp.where(tri_incl, jnp.exp(jnp.where(tri_incl, diff, 0.0)), 0.0)
    decay_strict = jnp.where(tri_strict, decay_incl, 0.0)
    kb = k * beta[..., None]
    a_mat = jnp.einsum('bhnid,bhnjd->bhnij', kb, k) * decay_strict + jnp.eye(CHUNK, dtype=jnp.float32)
    u = lax.linalg.triangular_solve(a_mat, v * beta[..., None], left_side=True, lower=True, unit_diagonal=True)
    w = lax.linalg.triangular_solve(a_mat, kb * jnp.exp(G)[..., None], left_side=True, lower=True, unit_diagonal=True)
    qk = jnp.einsum('bhnid,bhnjd->bhnij', q, k) * decay_incl
    xs = tuple(jnp.moveaxis(a, 2, 0) for a in (q, k, u, w, G, qk))

    def step(S, c):
        qc, kc, uc, wc, Gc, qkc = c
        v_new = uc - jnp.einsum('bhcd,bhde->bhce', wc, S)
        o = jnp.einsum('bhcd,bhde->bhce', qc * jnp.exp(Gc)[..., None], S) + jnp.einsum('bhij,bhje->bhie', qkc, v_new)
        g_last = Gc[..., -1:]
        S = S * jnp.exp(g_last)[..., None] + jnp.einsum('bhcd,bhce->bhde', kc * jnp.exp(g_last - Gc)[..., None], v_new)
        return S, o

    S, o = lax.scan(step, s0.astype(jnp.float32), xs)
    o = jnp.moveaxis(jnp.moveaxis(o, 0, 2), 1, 3).reshape(B, nc * CHUNK, H, dv)[:, :T]
    return o, S


def _indexer_select(qi, wi, ki, qpos, ktop):
    L = ki.shape[1]
    s = jax.nn.relu(jnp.einsum('bthd,bld->bthl', qi, ki))
    score = jnp.einsum('bth,bthl->btl', wi, s).astype(jnp.float32)
    allowed = jnp.arange(L, dtype=jnp.int32)[None, :] <= qpos[:, None]
    score = jnp.where(allowed[None], score, -jnp.inf)
    _, sel = lax.top_k(score, ktop)
    valid = sel <= qpos[None, :, None]
    return sel, valid


def _gather_rows(a, sel):
    return a[jnp.arange(a.shape[0])[:, None, None], sel]


def _sparse_attend(q, ks, vs, valid):
    s = jnp.einsum('btngd,btknd->btngk', q, ks).astype(jnp.float32) * (HEAD_DIM ** -0.5)
    s = jnp.where(valid[:, :, None, None, :], s, -jnp.inf)
    p = jax.nn.softmax(s, axis=-1)
    return jnp.einsum('btngk,btknd->btngd', p.astype(vs.dtype), vs)


def _dsa_prompt(q, k, v, qi, ki, wi):
    B, S = q.shape[:2]
    ktop = min(TOPK_MAX, S // 4)
    nb = S // Q_BLOCK
    def blocks(a):
        return jnp.moveaxis(a.reshape((B, nb, Q_BLOCK) + a.shape[2:]), 1, 0)
    qpos = jnp.arange(S, dtype=jnp.int32).reshape(nb, Q_BLOCK)

    def one_block(args):
        qb, qib, wib, pb = args
        sel, valid = _indexer_select(qib, wib, ki, pb, ktop)
        return _sparse_attend(qb, _gather_rows(k, sel), _gather_rows(v, sel), valid)

    o = lax.map(one_block, (blocks(q), blocks(qi), blocks(wi), qpos))
    return jnp.moveaxis(o, 0, 1).reshape(q.shape)


def _gather_paged(pool, page_table, new_rows, sel):
    DB, T = new_rows.shape[:2]
    bidx = jnp.arange(DB)[:, None, None]
    ps = jnp.minimum(sel, PAST_LEN - 1)
    phys = page_table[bidx, ps // PAGE_SIZE]
    past = pool[phys, ps % PAGE_SIZE].astype(new_rows.dtype)
    new = new_rows[bidx, jnp.clip(sel - PAST_LEN, 0, T - 1)]
    return jnp.where((sel < PAST_LEN)[..., None, None], past, new)


def _dsa_sample(q, k, v, qi, ki, wi, cache_k, cache_v, cache_ki, page_table):
    DB, T = q.shape[:2]
    ktop = min(TOPK_MAX, (PAST_LEN + T) // 4)
    qpos = PAST_LEN + jnp.arange(T, dtype=jnp.int32)
    ki_past = cache_ki[page_table].reshape(DB, PAST_LEN, IDX_DIM)
    ki_all = jnp.concatenate([ki_past.astype(ki.dtype), ki], axis=1)
    sel, valid = _indexer_select(qi, wi, ki_all, qpos, ktop)
    ks = _gather_paged(cache_k, page_table, k, sel)
    vs = _gather_paged(cache_v, page_table, v, sel)
    return _sparse_attend(q, ks, vs, valid)


def _layer(x, pos, w, dsa_fn, conv_buf, ssm0):
    (ffn1_g, ffn1_u, ffn1_d, ln1_g, ln1_b, w_in, conv_w, a_log, dt_bias, gdn_norm_g,
     w_o, ln2_g, ln2_b, ffn2_g, ffn2_u, ffn2_d, ln3_g, ln3_b) = w
    B, T, _ = x.shape
    x = _layer_norm(DN_ALPHA * x + 0.5 * _swiglu(x, ffn1_g, ffn1_u, ffn1_d), ln1_g, ln1_b)
    parts = _split_columns(x @ w_in)
    q = _rope(parts['q_a'].reshape(B, T, N_HEADS, HEAD_DIM), pos, ROPE_DIM).reshape(B, T, N_KV_HEADS, GROUP, HEAD_DIM)
    k = _rope(parts['k_a'].reshape(B, T, N_KV_HEADS, HEAD_DIM), pos, ROPE_DIM)
    v = parts['v_a'].reshape(B, T, N_KV_HEADS, HEAD_DIM)
    qi = _rope(parts['q_idx'].reshape(B, T, IDX_HEADS, IDX_DIM), pos, IDX_ROPE_DIM)
    ki = _rope(parts['k_idx'].reshape(B, T, 1, IDX_DIM), pos, IDX_ROPE_DIM)[:, :, 0]
    wi = parts['w_idx'] * (IDX_HEADS ** -0.5 * IDX_DIM ** -0.5)
    o_a = dsa_fn(q, k, v, qi, ki, wi).reshape(B, T, D_MODEL)
    qkv, conv_new = _causal_conv(parts['qkv_b'], conv_buf, conv_w)
    qb, kb, vb = jnp.split(qkv, [GDN_HEADS * GDN_DK, 2 * GDN_HEADS * GDN_DK], axis=-1)
    qb = _l2norm(qb.reshape(B, T, GDN_HEADS, GDN_DK)) * (GDN_DK ** -0.5)
    kb = _l2norm(kb.reshape(B, T, GDN_HEADS, GDN_DK))
    vb = vb.reshape(B, T, GDN_HEADS, GDN_DV)
    beta = jax.nn.sigmoid(parts['beta_b'].astype(jnp.float32))
    g = -jnp.exp(a_log.astype(jnp.float32)) * jax.nn.softplus(parts['a_b'].astype(jnp.float32) + dt_bias.astype(jnp.float32))
    o_b, ssm_new = _gated_delta(qb, kb, vb, g, beta, ssm0)
    z = parts['z_b'].reshape(B, T, GDN_HEADS, GDN_DV).astype(jnp.float32)
    o_b = (_rms_norm(o_b, gdn_norm_g) * jax.nn.silu(z)).reshape(B, T, D_MODEL).astype(x.dtype)
    merged = jax.nn.sigmoid(parts['gate_a']) * o_a + jax.nn.sigmoid(parts['gate_b']) * o_b
    x = _layer_norm(DN_ALPHA * x + merged @ w_o, ln2_g, ln2_b)
    x = _layer_norm(DN_ALPHA * x + 0.5 * _swiglu(x, ffn2_g, ffn2_u, ffn2_d), ln3_g, ln3_b)
    return x, (k, v, ki, ssm_new.astype(ssm0.dtype), conv_new)


def setup_inputs(seed: int = 0) -> dict:
    key = jax.random.key(seed)
    ks = iter(jax.random.split(key, 40))
    f32 = jnp.float32

    def nrm(shape, scale):
        return jax.random.normal(next(ks), shape, f32) * scale

    n_pages = PAST_LEN // PAGE_SIZE
    n_phys = (DEC_BATCH * n_pages * 5) // 4
    page_table = jax.random.permutation(next(ks), n_phys)[: DEC_BATCH * n_pages].reshape(DEC_BATCH, n_pages).astype(jnp.int32)
    dt = jnp.exp(jax.random.uniform(next(ks), (DEPTH, GDN_HEADS), f32, math.log(1e-3), math.log(1e-1)))
    a_log = jnp.log(jax.random.uniform(next(ks), (DEPTH, GDN_HEADS), f32, 1.0, 16.0))
    return {
        'x_prompt': nrm((BATCH, SEQ, D_MODEL), 1.0),
        'x_sample': nrm((DEC_BATCH, DEC_SEQ, D_MODEL), 1.0),
        'cache_k': nrm((DEPTH, n_phys, PAGE_SIZE, N_KV_HEADS, HEAD_DIM), 1.0),
        'cache_v': nrm((DEPTH, n_phys, PAGE_SIZE, N_KV_HEADS, HEAD_DIM), 1.0),
        'cache_idx_k': nrm((DEPTH, n_phys, PAGE_SIZE, IDX_DIM), 1.0),
        'state_ssm': nrm((DEPTH, DEC_BATCH, GDN_HEADS, GDN_DK, GDN_DV), 0.1),
        'state_conv': nrm((DEPTH, DEC_BATCH, CONV_W - 1, CONV_DIM), 1.0),
        'page_table': page_table,
        'ffn1_w_gate': nrm((DEPTH, D_MODEL, D_FF), D_MODEL ** -0.5),
        'ffn1_w_up': nrm((DEPTH, D_MODEL, D_FF), D_MODEL ** -0.5),
        'ffn1_w_down': nrm((DEPTH, D_FF, D_MODEL), DN_BETA * D_FF ** -0.5),
        'ln1_g': 1.0 + nrm((DEPTH, D_MODEL), 0.05),
        'ln1_b': nrm((DEPTH, D_MODEL), 0.05),
        'w_in': nrm((DEPTH, D_MODEL, IN_COLS), D_MODEL ** -0.5),
        'conv_w': nrm((DEPTH, CONV_W, CONV_DIM), CONV_W ** -0.5),
        'a_log': a_log,
        'dt_bias': jnp.log(jnp.expm1(dt)),
        'gdn_norm_g': 1.0 + nrm((DEPTH, GDN_DV), 0.05),
        'w_o': nrm((DEPTH, D_MODEL, D_MODEL), DN_BETA * D_MODEL ** -0.5),
        'ln2_g': 1.0 + nrm((DEPTH, D_MODEL), 0.05),
        'ln2_b': nrm((DEPTH, D_MODEL), 0.05),
        'ffn2_w_gate': nrm((DEPTH, D_MODEL, D_FF), D_MODEL ** -0.5),
        'ffn2_w_up': nrm((DEPTH, D_MODEL, D_FF), D_MODEL ** -0.5),
        'ffn2_w_down': nrm((DEPTH, D_FF, D_MODEL), DN_BETA * D_FF ** -0.5),
        'ln3_g': 1.0 + nrm((DEPTH, D_MODEL), 0.05),
        'ln3_b': nrm((DEPTH, D_MODEL), 0.05),
    }


def reference(x_prompt, x_sample, cache_k, cache_v, cache_idx_k, state_ssm, state_conv, page_table,
              ffn1_w_gate, ffn1_w_up, ffn1_w_down, ln1_g, ln1_b, w_in, conv_w, a_log, dt_bias,
              gdn_norm_g, w_o, ln2_g, ln2_b, ffn2_w_gate, ffn2_w_up, ffn2_w_down, ln3_g, ln3_b):
    B, S, _ = x_prompt.shape
    DB, T, _ = x_sample.shape
    pos_p = jnp.arange(S, dtype=jnp.int32)
    pos_s = PAST_LEN + jnp.arange(T, dtype=jnp.int32)
    yp, ys = x_prompt, x_sample
    outs_p, outs_s = [], []
    for l in range(DEPTH):
        w = (ffn1_w_gate[l], ffn1_w_up[l], ffn1_w_down[l], ln1_g[l], ln1_b[l], w_in[l], conv_w[l],
             a_log[l], dt_bias[l], gdn_norm_g[l], w_o[l], ln2_g[l], ln2_b[l],
             ffn2_w_gate[l], ffn2_w_up[l], ffn2_w_down[l], ln3_g[l], ln3_b[l])
        conv0 = jnp.zeros((B, CONV_W - 1, CONV_DIM), x_prompt.dtype)
        ssm0 = jnp.zeros((B, GDN_HEADS, GDN_DK, GDN_DV), state_ssm.dtype)
        yp, st_p = _layer(yp, pos_p, w, _dsa_prompt, conv0, ssm0)
        dsa_s = functools.partial(_dsa_sample, cache_k=cache_k[l], cache_v=cache_v[l],
                                  cache_ki=cache_idx_k[l], page_table=page_table)
        ys, st_s = _layer(ys, pos_s, w, dsa_s, state_conv[l], state_ssm[l])
        outs_p.append(st_p)
        outs_s.append(st_s)
    k_p, v_p, ki_p, ssm_p, conv_p = [jnp.stack(a) for a in zip(*outs_p)]
    k_s, v_s, ki_s, ssm_s, conv_s = [jnp.stack(a) for a in zip(*outs_s)]
    return (yp, ys, k_p, v_p, ki_p, ssm_p, conv_p, k_s, v_s, ki_s, ssm_s, conv_s)
```

```python
import functools

import jax
import jax.numpy as jnp
import numpy as np
from jax import lax
from jax.experimental import pallas as pl
from jax.experimental.pallas import tpu as pltpu

D_MODEL = 2048
PAST_LEN = 16384
PAGE_SIZE = 128
HEAD_DIM = 128
N_HEADS = D_MODEL // HEAD_DIM
N_KV_HEADS = 4
GROUP = N_HEADS // N_KV_HEADS
ROPE_DIM = HEAD_DIM // 4
IDX_HEADS = 16
IDX_DIM = 64
IDX_ROPE_DIM = IDX_DIM // 4
TOPK_MAX = 256
ROPE_THETA = 500000.0
GDN_DK = 128
GDN_DV = 128
GDN_HEADS = D_MODEL // GDN_DV
CONV_W = 4
CONV_DIM = 2 * GDN_HEADS * GDN_DK + GDN_HEADS * GDN_DV
CHUNK = 64
D_FF = 5632
LN_EPS = 1e-5
NORM_EPS = 1e-6
DEPTH = 1
DN_ALPHA = (2 * DEPTH) ** 0.25

IN_SPLITS = (
    ('q_a', N_HEADS * HEAD_DIM), ('k_a', N_KV_HEADS * HEAD_DIM), ('v_a', N_KV_HEADS * HEAD_DIM),
    ('q_idx', IDX_HEADS * IDX_DIM), ('k_idx', IDX_DIM), ('w_idx', IDX_HEADS),
    ('qkv_b', CONV_DIM), ('a_b', GDN_HEADS), ('beta_b', GDN_HEADS), ('z_b', GDN_HEADS * GDN_DV),
    ('gate_a', D_MODEL), ('gate_b', D_MODEL),
)
BIG_ORDER = ('qkv_b', 'q_a', 'z_b', 'gate_a', 'gate_b', 'q_idx', 'k_a', 'v_a')
SMALL_ORDER = ('k_idx', 'w_idx', 'a_b', 'beta_b')
LANES = 128
VMEM_LIMIT = 56 * 1024 * 1024
NEG_BIG = -1e30
INT_MIN = -2 ** 31

F32 = jnp.float32
BF16 = jnp.bfloat16


def _offsets(order):
    sizes = dict(IN_SPLITS)
    offs, o = {}, 0
    for nm in order:
        offs[nm] = o
        o += sizes[nm]
    return offs, o


BIG_OFF, BIG_COLS = _offsets(BIG_ORDER)
SMALL_OFF, SMALL_USED = _offsets(SMALL_ORDER)
SIZES = dict(IN_SPLITS)


def _cparams(*sem):
    return pltpu.CompilerParams(dimension_semantics=sem, vmem_limit_bytes=VMEM_LIMIT)


def _dot(a, b):
    return jnp.dot(a, b, preferred_element_type=F32)


def _dot_nt(a, b):
    return lax.dot_general(a, b, (((1,), (1,)), ((), ())), preferred_element_type=F32)


def _dot_hi(a, b):
    return jnp.dot(a, b, preferred_element_type=F32, precision=lax.Precision.HIGHEST)


def _dot_nt_hi(a, b):
    return lax.dot_general(a, b, (((1,), (1,)), ((), ())), preferred_element_type=F32,
                           precision=lax.Precision.HIGHEST)


def _dot_tn_hi(a, b):
    return lax.dot_general(a, b, (((0,), (0,)), ((), ())), preferred_element_type=F32,
                           precision=lax.Precision.HIGHEST)


def _dot_bf16(a, b):
    return _dot(a.astype(BF16), b.astype(BF16))


def _split_bf16(a):
    hi = a.astype(BF16)
    return hi, (a - hi.astype(F32)).astype(BF16)


def _dot_split(a, b):
    ah, al = _split_bf16(a)
    bh, bl = _split_bf16(b)
    return _dot(ah, bh) + (_dot(al, bh) + _dot(ah, bl))


def _silu(x):
    return x * jax.nn.sigmoid(x)


def _layer_norm(y, g, b):
    mu = jnp.mean(y, axis=-1, keepdims=True)
    d = y - mu
    var = jnp.mean(d * d, axis=-1, keepdims=True)
    return d * lax.rsqrt(var + LN_EPS) * g + b


def _ffn_ln_kernel(x_ref, wg_ref, wu_ref, wd_ref, g_ref, b_ref, o_ref, acc_ref, xb_ref):
    j = pl.program_id(1)

    @pl.when(j == 0)
    def _():
        acc_ref[...] = jnp.zeros_like(acc_ref)
        xb_ref[...] = x_ref[...].astype(BF16)

    xb = xb_ref[...]
    hg = _dot(xb, wg_ref[...])
    hu = _dot(xb, wu_ref[...])
    h = _silu(hg) * hu
    acc_ref[...] += _dot(h.astype(BF16), wd_ref[...])

    @pl.when(j == pl.num_programs(1) - 1)
    def _():
        y = DN_ALPHA * x_ref[...] + 0.5 * acc_ref[...]
        o_ref[...] = _layer_norm(y, g_ref[...], b_ref[...])


def _ffn_ln(x, wg, wu, wd, g, b, tm, tf):
    M, D = x.shape
    F = wg.shape[1]
    return pl.pallas_call(
        _ffn_ln_kernel,
        grid=(M // tm, F // tf),
        in_specs=[
            pl.BlockSpec((tm, D), lambda i, j: (i, 0)),
            pl.BlockSpec((D, tf), lambda i, j: (0, j)),
            pl.BlockSpec((D, tf), lambda i, j: (0, j)),
            pl.BlockSpec((tf, D), lambda i, j: (j, 0)),
            pl.BlockSpec((1, D), lambda i, j: (0, 0)),
            pl.BlockSpec((1, D), lambda i, j: (0, 0)),
        ],
        out_specs=pl.BlockSpec((tm, D), lambda i, j: (i, 0)),
        out_shape=jax.ShapeDtypeStruct((M, D), F32),
        scratch_shapes=[pltpu.VMEM((tm, D), F32), pltpu.VMEM((tm, D), BF16)],
        compiler_params=_cparams("parallel", "arbitrary"),
        name="ffn_ln",
    )(x, wg, wu, wd, g, b)


def _proj_kernel(x_ref, w_ref, o_ref, xb_ref):
    @pl.when(pl.program_id(1) == 0)
    def _():
        xb_ref[...] = x_ref[...].astype(BF16)

    o_ref[...] = _dot(xb_ref[...], w_ref[...])


def _proj(x, w, tm, tn):
    M, K = x.shape
    N = w.shape[1]
    return pl.pallas_call(
        _proj_kernel,
        grid=(M // tm, N // tn),
        in_specs=[
            pl.BlockSpec((tm, K), lambda i, j: (i, 0)),
            pl.BlockSpec((K, tn), lambda i, j: (0, j)),
        ],
        out_specs=pl.BlockSpec((tm, tn), lambda i, j: (i, j)),
        out_shape=jax.ShapeDtypeStruct((M, N), F32),
        scratch_shapes=[pltpu.VMEM((tm, K), BF16)],
        compiler_params=_cparams("parallel", "arbitrary"),
        name="in_proj",
    )(x, w)


def _rope_tables(pos, rot_dim, period, live_lanes=LANES):
    half = rot_dim // 2
    inv = ROPE_THETA ** (-jnp.arange(half, dtype=F32) / half)
    ang = pos.astype(F32)[:, None] * inv[None, :]
    cos, sin = jnp.cos(ang), jnp.sin(ang)
    lane = np.arange(LANES)
    lp = lane % period
    idx = lp % half
    live = (lp < rot_dim) & (lane < live_lanes)
    c = jnp.where(live[None, :], cos[:, idx], 1.0)
    s = jnp.where(live[None, :], jnp.where((lp < half)[None, :], -sin[:, idx], sin[:, idx]), 0.0)
    return c.astype(F32), s.astype(F32)


def _rope_tile(x, c, s, half, period):
    lane = lax.broadcasted_iota(jnp.int32, x.shape, 1)
    first = (lane & (period - 1)) < half
    partner = jnp.where(first, pltpu.roll(x, LANES - half, 1), pltpu.roll(x, half, 1))
    return x * c + partner * s


def _rope_kernel(q_ref, k_ref, v_ref, qi_ref, sm_ref, ca_ref, sa_ref, ci_ref, si_ref, cs_ref, ss_ref,
                 qo_ref, ko_ref, kb_ref, vb_ref, qio_ref, smo_ref, smb_ref):
    ca, sa = ca_ref[...], sa_ref[...]
    ci, si = ci_ref[...], si_ref[...]
    for h in range(N_HEADS):
        sl = slice(h * LANES, (h + 1) * LANES)
        qo_ref[:, sl] = _rope_tile(q_ref[:, sl], ca, sa, ROPE_DIM // 2, HEAD_DIM).astype(BF16)
    for h in range(N_KV_HEADS):
        sl = slice(h * LANES, (h + 1) * LANES)
        kr = _rope_tile(k_ref[:, sl], ca, sa, ROPE_DIM // 2, HEAD_DIM)
        ko_ref[:, sl] = kr
        kb_ref[:, sl] = kr.astype(BF16)
    vb_ref[...] = v_ref[...].astype(BF16)
    for h in range(IDX_HEADS * IDX_DIM // LANES):
        sl = slice(h * LANES, (h + 1) * LANES)
        qio_ref[:, sl] = _rope_tile(qi_ref[:, sl], ci, si, IDX_ROPE_DIM // 2, IDX_DIM).astype(BF16)
    sm = _rope_tile(sm_ref[...], cs_ref[...], ss_ref[...], IDX_ROPE_DIM // 2, IDX_DIM)
    smo_ref[...] = sm
    smb_ref[...] = sm.astype(BF16)


def _rope_prep(u_big, u_small, tabs, tm):
    M = u_big.shape[0]
    tpos = tabs[0].shape[0]
    nt = tpos // tm

    def col(name):
        w = SIZES[name]
        return pl.BlockSpec((tm, w), lambda i, o=BIG_OFF[name] // w: (i, o))

    tab_spec = pl.BlockSpec((tm, LANES), lambda i: (i % nt, 0))
    row = lambda w: pl.BlockSpec((tm, w), lambda i: (i, 0))
    return pl.pallas_call(
        _rope_kernel,
        grid=(M // tm,),
        in_specs=[col('q_a'), col('k_a'), col('v_a'), col('q_idx'), row(LANES)] + [tab_spec] * 6,
        out_specs=[row(SIZES['q_a']), row(SIZES['k_a']), row(SIZES['k_a']), row(SIZES['v_a']),
                   row(SIZES['q_idx']), row(LANES), row(LANES)],
        out_shape=[
            jax.ShapeDtypeStruct((M, SIZES['q_a']), BF16),
            jax.ShapeDtypeStruct((M, SIZES['k_a']), F32),
            jax.ShapeDtypeStruct((M, SIZES['k_a']), BF16),
            jax.ShapeDtypeStruct((M, SIZES['v_a']), BF16),
            jax.ShapeDtypeStruct((M, SIZES['q_idx']), BF16),
            jax.ShapeDtypeStruct((M, LANES), F32),
            jax.ShapeDtypeStruct((M, LANES), BF16),
        ],
        compiler_params=_cparams("parallel"),
        name="rope_prep",
    )(u_big, u_big, u_big, u_big, u_small, *tabs)


BISECT_UNROLL = 4


def _count(pred):
    return jnp.sum(jnp.where(pred, 1.0, 0.0), axis=-1, keepdims=True)


def _tie_index(score, kidx, thr):
    return jnp.where(score == thr, kidx, jnp.int32(2 ** 31 - 1))


def _topk_threshold(score, kidx, n_allowed, k, idx_bits):
    rows = score.shape[0]
    take_all = n_allowed <= k
    lo0 = jnp.min(jnp.where(score == -jnp.inf, jnp.inf, score), axis=-1, keepdims=True)
    hi0 = jnp.max(score, axis=-1, keepdims=True)
    lo0 = jnp.where(take_all, 0.0, lo0)
    hi0 = jnp.where(take_all, 0.0, hi0)

    def step(lo, hi):
        mid = 0.5 * lo + 0.5 * hi
        ge = _count(score >= mid) >= k
        return jnp.where(ge, mid, lo), jnp.where(ge, hi, mid)

    def body(state):
        lo, hi, _ = state
        for _ in range(BISECT_UNROLL):
            lo, hi = step(lo, hi)
        mid = 0.5 * lo + 0.5 * hi
        still_open = jnp.max(jnp.where((mid > lo) & (mid < hi), 1.0, 0.0))
        return lo, hi, still_open

    lo, hi, _ = lax.while_loop(lambda state: state[2] > 0.5, body, (lo0, hi0, jnp.float32(1.0)))
    thr = jnp.where(_count(score >= hi) >= k, hi, lo)
    need = k - _count(score > thr)
    tie = _tie_index(score, kidx, thr)

    def ibody(t, j):
        cand = j + jnp.left_shift(jnp.int32(1), idx_bits - 1 - t)
        return jnp.where(_count(tie < cand) < need, cand, j)

    jmax = lax.fori_loop(0, idx_bits, ibody, jnp.zeros((rows, 1), jnp.int32))
    return thr, jmax, take_all


def _selected(score, kidx, thr, jmax, take_all):
    return take_all | (score > thr) | (_tie_index(score, kidx, thr) <= jmax)


def _dsa_prompt_kernel(q_ref, qi_ref, sm_ref, k_ref, v_ref, kis_ref, o_ref, *, ktop, key_step):
    tq = q_ref.shape[0]
    S = k_ref.shape[0]
    i = pl.program_id(1)
    w0 = SMALL_OFF['w_idx']

    def attend(L):
        ki = kis_ref[:L, :IDX_DIM]
        w = sm_ref[:, w0:w0 + IDX_HEADS] * (IDX_HEADS ** -0.5 * IDX_DIM ** -0.5)
        score = jnp.zeros((tq, L), F32)
        for h in range(IDX_HEADS):
            s = _dot_nt(qi_ref[:, h * IDX_DIM:(h + 1) * IDX_DIM], ki)
            score = score + w[:, h:h + 1] * jnp.maximum(s, 0.0)
        qpos = i * tq + lax.broadcasted_iota(jnp.int32, (tq, 1), 0)
        kidx = lax.broadcasted_iota(jnp.int32, (tq, L), 1)
        allowed = kidx <= qpos
        score = jnp.where(allowed, score, -jnp.inf)
        thr, jmax, take_all = _topk_threshold(score, kidx, qpos + 1, ktop, int(L - 1).bit_length())
        mask = _selected(score, kidx, thr, jmax, take_all) & allowed
        for h in range(N_HEADS):
            n = h // GROUP
            sl = slice(h * HEAD_DIM, (h + 1) * HEAD_DIM)
            kv = slice(n * HEAD_DIM, (n + 1) * HEAD_DIM)
            s = _dot_nt(q_ref[:, sl], k_ref[:L, kv]) * (HEAD_DIM ** -0.5)
            s = jnp.where(mask, s, -jnp.inf)
            m = jnp.max(s, axis=-1, keepdims=True)
            p = jnp.exp(s - m)
            l = jnp.sum(p, axis=-1, keepdims=True)
            o_ref[:, sl] = _dot(p.astype(BF16), v_ref[:L, kv]) / l

    level = ((i + 1) * tq - 1) // key_step
    for lv in range(S // key_step):
        pl.when(level == lv)(functools.partial(attend, (lv + 1) * key_step))


def _dsa_prompt(q_bf, qi_bf, small_rot, k_bf, v_bf, small_bf, B, S, tq):
    ktop = min(TOPK_MAX, S // 4)
    nq = S // tq
    key_step = min(S, 512)
    row = lambda w: pl.BlockSpec((tq, w), lambda b, i: (b * nq + i, 0))
    full = lambda w: pl.BlockSpec((S, w), lambda b, i: (b, 0))
    return pl.pallas_call(
        functools.partial(_dsa_prompt_kernel, ktop=ktop, key_step=key_step),
        grid=(B, nq),
        in_specs=[row(q_bf.shape[1]), row(qi_bf.shape[1]), row(LANES),
                  full(k_bf.shape[1]), full(v_bf.shape[1]), full(LANES)],
        out_specs=row(q_bf.shape[1]),
        out_shape=jax.ShapeDtypeStruct((B * S, q_bf.shape[1]), F32),
        compiler_params=_cparams("parallel", "arbitrary"),
        name="dsa_prompt",
    )(q_bf, qi_bf, small_rot, k_bf, v_bf, small_bf)


def _idx_score_rows(qi, w, kpage):
    s = _dot_nt(qi, kpage)
    return jnp.sum(w * jnp.maximum(s, 0.0), axis=0, keepdims=True)


def _sample_scores_kernel(pt_ref, qi_ref, w_ref, knew_ref, *refs):
    page_refs, (o_ref, onew_ref) = refs[:-2], refs[-2:]
    npp = len(page_refs)
    p = pl.program_id(1)
    qi = qi_ref[0]
    w = w_ref[0] * (IDX_HEADS ** -0.5 * IDX_DIM ** -0.5)
    for j, page_ref in enumerate(page_refs):
        o_ref[0, pl.ds(p * npp + j, 1), :] = _idx_score_rows(qi, w, page_ref[0, 0].astype(BF16))

    @pl.when(p == 0)
    def _():
        kn = jnp.broadcast_to(knew_ref[0], (PAGE_SIZE, IDX_DIM))
        sc = _idx_score_rows(qi, w, kn)
        lane = lax.broadcasted_iota(jnp.int32, (1, PAGE_SIZE), 1)
        onew_ref[0] = jnp.where(lane == 0, sc, -jnp.inf)


def _page_specs(row_shape, npp, layer):
    zeros = (0,) * (1 + len(row_shape))
    return [pl.BlockSpec((1, 1, PAGE_SIZE) + row_shape,
                         lambda b, p, pt, j=j: (layer, pt[b, p * npp + j]) + zeros)
            for j in range(npp)]


def _pages_per_step(n_pages, cap):
    npp = min(cap, n_pages)
    while n_pages % npp:
        npp -= 1
    return npp


def _sample_scores(page_table, qi3, w3, knew3, cache_ki, layer):
    DB, n_pages = page_table.shape
    npp = _pages_per_step(n_pages, 16)
    return pl.pallas_call(
        _sample_scores_kernel,
        grid_spec=pltpu.PrefetchScalarGridSpec(
            num_scalar_prefetch=1,
            grid=(DB, n_pages // npp),
            in_specs=[
                pl.BlockSpec((1, IDX_HEADS, IDX_DIM), lambda b, p, pt: (b, 0, 0)),
                pl.BlockSpec((1, IDX_HEADS, 1), lambda b, p, pt: (b, 0, 0)),
                pl.BlockSpec((1, 1, IDX_DIM), lambda b, p, pt: (b, 0, 0)),
            ] + _page_specs((IDX_DIM,), npp, layer),
            out_specs=[
                pl.BlockSpec((1, n_pages, PAGE_SIZE), lambda b, p, pt: (b, 0, 0)),
                pl.BlockSpec((1, 1, PAGE_SIZE), lambda b, p, pt: (b, 0, 0)),
            ],
        ),
        out_shape=[jax.ShapeDtypeStruct((DB, n_pages, PAGE_SIZE), F32),
                   jax.ShapeDtypeStruct((DB, 1, PAGE_SIZE), F32)],
        compiler_params=_cparams("parallel", "arbitrary"),
        name="sample_scores",
    )(page_table, qi3, w3, knew3, *([cache_ki] * npp))


def _sample_thr_kernel(s_ref, thr_ref, jmax_ref, all_ref, *, ktop, n_valid, idx_bits):
    kidx = lax.broadcasted_iota(jnp.int32, s_ref.shape, 1)
    score = jnp.where(kidx < n_valid, s_ref[...], -jnp.inf)
    n_allowed = jnp.full((s_ref.shape[0], 1), n_valid, jnp.int32)
    thr, jmax, take_all = _topk_threshold(score, kidx, n_allowed, ktop, idx_bits)
    thr_ref[...] = jnp.broadcast_to(thr, thr_ref.shape)
    jmax_ref[...] = jnp.broadcast_to(jmax, jmax_ref.shape)
    all_ref[...] = jnp.broadcast_to(take_all.astype(jnp.int32), all_ref.shape)


def _sample_threshold(scores, ktop, n_valid):
    DB, L = scores.shape
    out = lambda dt: jax.ShapeDtypeStruct((DB, LANES), dt)
    return pl.pallas_call(
        functools.partial(_sample_thr_kernel, ktop=ktop, n_valid=n_valid, idx_bits=int(L - 1).bit_length()),
        out_shape=[out(F32), out(jnp.int32), out(jnp.int32)],
        compiler_params=pltpu.CompilerParams(vmem_limit_bytes=VMEM_LIMIT),
        name="sample_threshold",
    )(scores)


def _sample_attn_kernel(pt_ref, q_ref, s_ref, thr_ref, jmax_ref, all_ref, knew_ref, vnew_ref, *refs, n_pages):
    npp = (len(refs) - 4) // 2
    kp_refs, vp_refs = refs[:npp], refs[npp:2 * npp]
    o_ref, m_ref, l_ref, acc_ref = refs[2 * npp:]
    b = pl.program_id(0)
    p = pl.program_id(1)

    @pl.when(p == 0)
    def _():
        m_ref[...] = jnp.full_like(m_ref, NEG_BIG)
        l_ref[...] = jnp.zeros_like(l_ref)
        acc_ref[...] = jnp.zeros_like(acc_ref)

    thr = thr_ref[pl.ds(b, 1), 0:1]
    jmax = jmax_ref[pl.ds(b, 1), 0:1]
    take_all = all_ref[pl.ds(b, 1), 0:1] > 0

    def update(scores_row, base, kheads, vheads):
        P = scores_row.shape[1]
        kidx = base + lax.broadcasted_iota(jnp.int32, (1, P), 1)
        sel = _selected(scores_row, kidx, thr, jmax, take_all) & (kidx <= n_pages * PAGE_SIZE)
        q = q_ref[0]
        group = lax.broadcasted_iota(jnp.int32, (N_HEADS, 1), 0) // GROUP
        s = jnp.zeros((N_HEADS, P), F32)
        for n in range(N_KV_HEADS):
            s = jnp.where(group == n, _dot_nt(q, kheads[n]), s)
        s = jnp.where(sel, s * (HEAD_DIM ** -0.5), -jnp.inf)
        m_old = m_ref[...]
        m_new = jnp.maximum(m_old, jnp.max(s, axis=-1, keepdims=True))
        corr = jnp.exp(m_old - m_new)
        pr = jnp.exp(s - m_new)
        l_ref[...] = l_ref[...] * corr + jnp.sum(pr, axis=-1, keepdims=True)
        acc = acc_ref[...] * corr
        for n in range(N_KV_HEADS):
            acc = acc + _dot(jnp.where(group == n, pr, 0.0).astype(BF16), vheads[n])
        acc_ref[...] = acc
        m_ref[...] = m_new

    def page_heads(page_refs):
        return [jnp.concatenate([r[0, 0, :, n, :].astype(BF16) for r in page_refs], axis=0)
                for n in range(N_KV_HEADS)]

    update(jnp.concatenate([s_ref[0, pl.ds(p * npp + j, 1), :] for j in range(npp)], axis=1),
           p * (npp * PAGE_SIZE), page_heads(kp_refs), page_heads(vp_refs))

    @pl.when(p == pl.num_programs(1) - 1)
    def _():
        row0 = lax.broadcasted_iota(jnp.int32, (PAGE_SIZE, HEAD_DIM), 0) == 0

        def new_heads(ref):
            return [jnp.where(row0, jnp.broadcast_to(ref[0, :, n * HEAD_DIM:(n + 1) * HEAD_DIM],
                                                     (PAGE_SIZE, HEAD_DIM)), 0.0).astype(BF16)
                    for n in range(N_KV_HEADS)]

        update(s_ref[0, pl.ds(n_pages, 1), :], n_pages * PAGE_SIZE, new_heads(knew_ref), new_heads(vnew_ref))
        o_ref[0] = acc_ref[...] / l_ref[...]


def _sample_attn(page_table, q3, scores3, thr, jmax, take_all, knew3, vnew3, cache_k, cache_v, layer):
    DB, n_pages = page_table.shape
    width = N_KV_HEADS * HEAD_DIM
    npp = _pages_per_step(n_pages, 8)
    bsel = lambda *shape: pl.BlockSpec((1,) + shape, lambda b, p, pt: (b,) + (0,) * len(shape))
    whole = pl.BlockSpec((DB, LANES), lambda b, p, pt: (0, 0))
    pages = _page_specs((N_KV_HEADS, HEAD_DIM), npp, layer)
    return pl.pallas_call(
        functools.partial(_sample_attn_kernel, n_pages=n_pages),
        grid_spec=pltpu.PrefetchScalarGridSpec(
            num_scalar_prefetch=1,
            grid=(DB, n_pages // npp),
            in_specs=[bsel(N_HEADS, HEAD_DIM), bsel(n_pages + 1, PAGE_SIZE), whole, whole, whole,
                      bsel(1, width), bsel(1, width)] + pages + pages,
            out_specs=bsel(N_HEADS, HEAD_DIM),
            scratch_shapes=[pltpu.VMEM((N_HEADS, 1), F32), pltpu.VMEM((N_HEADS, 1), F32),
                            pltpu.VMEM((N_HEADS, HEAD_DIM), F32)],
        ),
        out_shape=jax.ShapeDtypeStruct((DB, N_HEADS, HEAD_DIM), F32),
        compiler_params=_cparams("parallel", "arbitrary"),
        name="sample_attn",
    )(page_table, q3, scores3, thr, jmax, take_all, knew3, vnew3, *([cache_k] * npp), *([cache_v] * npp))


CARRY = 8
MXU_DIM = 256
GDN_GROUP = MXU_DIM // CHUNK
GDN_SPLIT_STAGES = 2


def _spread(a, row_head, hg):
    return jnp.concatenate([jnp.where(row_head == i, a, 0.0) for i in range(hg)], axis=1)


def _gdn_prep_kernel(x_ref, w_ref, buf_ref, q_ref, k_ref, v_ref, conv_ref, xpad_ref):
    tt = x_ref.shape[1]
    t = pl.program_id(1)
    lo = CARRY - (CONV_W - 1)

    @pl.when(t == 0)
    def _():
        xpad_ref[lo:CARRY, :] = buf_ref[0]

    xpad_ref[CARRY:CARRY + tt, :] = x_ref[0]
    nh = GDN_HEADS
    for c in range(CONV_DIM // LANES):
        sl = slice(c * LANES, (c + 1) * LANES)
        y = w_ref[0:1, sl] * xpad_ref[lo:lo + tt, sl]
        for j in range(1, CONV_W):
            y = y + w_ref[j:j + 1, sl] * xpad_ref[lo + j:lo + j + tt, sl]
        y = _silu(y)
        if c < 2 * nh:
            y = y * lax.rsqrt(jnp.sum(y * y, axis=-1, keepdims=True) + NORM_EPS)
        if c < nh:
            q_ref[0, :, sl] = y * (GDN_DK ** -0.5)
        elif c < 2 * nh:
            k_ref[0, :, slice((c - nh) * LANES, (c - nh + 1) * LANES)] = y
        else:
            v_ref[0, :, slice((c - 2 * nh) * LANES, (c - 2 * nh + 1) * LANES)] = y
    last = xpad_ref[lo + tt:CARRY + tt, :]
    xpad_ref[lo:CARRY, :] = last

    @pl.when(t == pl.num_programs(1) - 1)
    def _():
        conv_ref[0] = last


def _gdn_prep(u_big3, conv_w, buf, tt):
    B, T, _ = u_big3.shape
    w = SIZES['qkv_b']
    hd = GDN_HEADS * GDN_DK
    out = jax.ShapeDtypeStruct((B, T, hd), F32)
    ospec = pl.BlockSpec((1, tt, hd), lambda b, t: (b, t, 0))
    return pl.pallas_call(
        _gdn_prep_kernel,
        grid=(B, T // tt),
        in_specs=[
            pl.BlockSpec((1, tt, w), lambda b, t, o=BIG_OFF['qkv_b'] // w: (b, t, o)),
            pl.BlockSpec((CONV_W, w), lambda b, t: (0, 0)),
            pl.BlockSpec((1, CONV_W - 1, w), lambda b, t: (b, 0, 0)),
        ],
        out_specs=[ospec, ospec, ospec, pl.BlockSpec((1, CONV_W - 1, w), lambda b, t: (b, 0, 0))],
        out_shape=[out, out, out, jax.ShapeDtypeStruct((B, CONV_W - 1, w), F32)],
        scratch_shapes=[pltpu.VMEM((CARRY + tt, w), F32)],
        compiler_params=_cparams("parallel", "arbitrary"),
        name="gdn_prep",
    )(u_big3, conv_w, buf)


def _gdn_chunk_kernel(q_ref, k_ref, v_ref, z_ref, sm_ref, alog_ref, dtb_ref, gn_ref, s0_ref,
                      o_ref, sout_ref, state_ref, *, t_valid):
    C = q_ref.shape[1]
    c = pl.program_id(1)

    @pl.when(c == 0)
    def _():
        state_ref[...] = s0_ref[0]

    H = GDN_HEADS
    a0, b0 = SMALL_OFF['a_b'], SMALL_OFF['beta_b']
    live = (c * C + lax.broadcasted_iota(jnp.int32, (C, H), 0)) < t_valid
    xs = sm_ref[0, :, a0:a0 + H] + dtb_ref[...]
    softplus = jnp.maximum(xs, 0.0) + jnp.log1p(jnp.exp(-jnp.abs(xs)))
    g_all = jnp.where(live, -jnp.exp(alog_ref[...]) * softplus, 0.0)
    beta_all = jnp.where(live, jax.nn.sigmoid(sm_ref[0, :, b0:b0 + H]), 0.0)
    tri_f = (lax.broadcasted_iota(jnp.int32, (C, C), 0) >= lax.broadcasted_iota(jnp.int32, (C, C), 1)).astype(F32)
    eye_h = (lax.broadcasted_iota(jnp.int32, (H, H), 0) == lax.broadcasted_iota(jnp.int32, (H, H), 1)).astype(F32)
    gcum = _dot_hi(tri_f, g_all)
    gcum_t = _dot_nt_hi(eye_h, gcum)

    HG = GDN_GROUP
    R = HG * C
    ri = lax.broadcasted_iota(jnp.int32, (R, R), 0)
    ci = lax.broadcasted_iota(jnp.int32, (R, R), 1)
    same_head = (ri // C) == (ci // C)
    mask_incl = same_head & (ri >= ci)
    mask_strict = same_head & (ri > ci)
    row_head = lax.broadcasted_iota(jnp.int32, (R, 1), 0) // C
    row_head2 = jnp.concatenate([row_head, row_head], axis=0)

    for grp in range(H // HG):
        heads = range(grp * HG, (grp + 1) * HG)
        sls = [slice(h * GDN_DK, (h + 1) * GDN_DK) for h in heads]
        rows = lambda ref: jnp.concatenate([ref[0, :, sl] for sl in sls], axis=0)
        cols = lambda a: jnp.concatenate([a[:, h:h + 1] for h in heads], axis=0)
        q, k, v = rows(q_ref), rows(k_ref), rows(v_ref)
        beta = cols(beta_all)
        gcol = cols(gcum)
        grow = jnp.concatenate([gcum_t[h:h + 1, :] for h in heads], axis=1)
        glast = jnp.concatenate([jnp.broadcast_to(gcum[C - 1:C, h:h + 1], (C, 1)) for h in heads], axis=0)
        decay = jnp.where(mask_incl, jnp.exp(jnp.where(mask_incl, gcol - grow, 0.0)), 0.0)
        kb = k * beta
        kk_qk = _dot_nt(jnp.concatenate([kb, q], axis=0).astype(BF16), k.astype(BF16))
        nmat = jnp.where(mask_strict, kk_qk[:R] * decay, 0.0)
        qk = kk_qk[R:] * decay
        x = jnp.concatenate([v * beta, kb * jnp.exp(gcol)], axis=1)
        pw = nmat
        x = x - _dot_split(pw, x)
        for stage in range(max(C - 1, 1).bit_length() - 1):
            mm = _dot_split if stage < GDN_SPLIT_STAGES else _dot_bf16
            pw = mm(pw, pw)
            x = x + mm(pw, x)
        u, w = x[:, :GDN_DV], x[:, GDN_DV:]
        s_stack = state_ref[grp * HG:(grp + 1) * HG].reshape(HG * GDN_DK, GDN_DV)
        w_q = jnp.concatenate([w, q * jnp.exp(gcol)], axis=0)
        ws_qs = _dot(_spread(w_q, row_head2, HG).astype(BF16), s_stack.astype(BF16))
        v_new = u - ws_qs[:R]
        o = ws_qs[R:] + _dot(qk.astype(BF16), v_new.astype(BF16))
        kdec_t = (k * jnp.exp(glast - gcol)).T
        s_add = _dot(kdec_t.astype(BF16), _spread(v_new, row_head, HG).astype(BF16))
        on = o * lax.rsqrt(jnp.mean(o * o, axis=-1, keepdims=True) + NORM_EPS) * gn_ref[...]
        for a, h in enumerate(heads):
            state_ref[h] = (state_ref[h] * jnp.exp(gcum[C - 1:C, h:h + 1])
                            + s_add[:, a * GDN_DV:(a + 1) * GDN_DV])
            o_ref[0, :, sls[a]] = on[a * C:(a + 1) * C] * _silu(z_ref[0, :, sls[a]])

    @pl.when(c == pl.num_programs(1) - 1)
    def _():
        sout_ref[0] = state_ref[...]


def _gdn_chunks(qn, kn, vv, u_big3, small3, a_log, dt_bias, gn, s0, t_valid):
    B, Tp, hd = qn.shape
    nc = Tp // CHUNK
    blk = pl.BlockSpec((1, CHUNK, hd), lambda b, c: (b, c, 0))
    vec = lambda w: pl.BlockSpec((1, w), lambda b, c: (0, 0))
    st = pl.BlockSpec((1, GDN_HEADS, GDN_DK, GDN_DV), lambda b, c: (b, 0, 0, 0))
    return pl.pallas_call(
        functools.partial(_gdn_chunk_kernel, t_valid=t_valid),
        grid=(B, nc),
        in_specs=[blk, blk, blk,
                  pl.BlockSpec((1, CHUNK, hd), lambda b, c, o=BIG_OFF['z_b'] // hd: (b, c, o)),
                  pl.BlockSpec((1, CHUNK, LANES), lambda b, c: (b, c, 0)),
                  vec(GDN_HEADS), vec(GDN_HEADS), vec(GDN_DV), st],
        out_specs=[blk, st],
        out_shape=[jax.ShapeDtypeStruct((B, Tp, hd), F32),
                   jax.ShapeDtypeStruct((B, GDN_HEADS, GDN_DK, GDN_DV), F32)],
        scratch_shapes=[pltpu.VMEM((GDN_HEADS, GDN_DK, GDN_DV), F32)],
        compiler_params=_cparams("parallel", "arbitrary"),
        name="gdn_chunks",
    )(qn, kn, vv, u_big3, small3, a_log, dt_bias, gn, s0)


def _merge_kernel(x_ref, oa_ref, ob_ref, ga_ref, gb_ref, wo_ref, g_ref, b_ref, o_ref):
    merged = jax.nn.sigmoid(ga_ref[...]) * oa_ref[...] + jax.nn.sigmoid(gb_ref[...]) * ob_ref[...]
    y = DN_ALPHA * x_ref[...] + _dot(merged.astype(BF16), wo_ref[...])
    o_ref[...] = _layer_norm(y, g_ref[...], b_ref[...])


def _merge_proj_ln(x, o_a, o_b, u_big, w_o, g, b, tm):
    M, D = x.shape
    row = pl.BlockSpec((tm, D), lambda i: (i, 0))
    col = lambda name: pl.BlockSpec((tm, D), lambda i, o=BIG_OFF[name] // D: (i, o))
    vec = pl.BlockSpec((1, D), lambda i: (0, 0))
    return pl.pallas_call(
        _merge_kernel,
        grid=(M // tm,),
        in_specs=[row, row, row, col('gate_a'), col('gate_b'),
                  pl.BlockSpec((D, D), lambda i: (0, 0)), vec, vec],
        out_specs=row,
        out_shape=jax.ShapeDtypeStruct((M, D), F32),
        compiler_params=_cparams("parallel"),
        name="merge_proj_ln",
    )(x, o_a, o_b, u_big, u_big, w_o, g, b)


def _tiles(M):
    return (512, 256) if M % 512 == 0 else (M, M)


def _layer(x, B, T, pos, wts, conv_buf, ssm0, dsa_fn):
    M = B * T
    tm, te = _tiles(M)
    row = lambda a: a.reshape(1, -1)
    x1 = _ffn_ln(x, wts['ffn1_g'], wts['ffn1_u'], wts['ffn1_d'], row(wts['ln1_g']), row(wts['ln1_b']),
                 tm, 512)
    u_big = _proj(x1, wts['w_big'], tm, 1024)
    u_small = _proj(x1, wts['w_small'], tm, LANES)

    pos_rows = pos if T > 1 else jnp.broadcast_to(pos, (M,))
    tabs = (_rope_tables(pos_rows, ROPE_DIM, HEAD_DIM) + _rope_tables(pos_rows, IDX_ROPE_DIM, IDX_DIM)
            + _rope_tables(pos_rows, IDX_ROPE_DIM, IDX_DIM, live_lanes=IDX_DIM))
    q_bf, k_rot, k_bf, v_bf, qi_bf, small_rot, small_bf = _rope_prep(u_big, u_small, tabs, te)
    o_a = dsa_fn(u_big, q_bf, k_rot, k_bf, v_bf, qi_bf, small_rot, small_bf)

    u_big3 = u_big.reshape(B, T, BIG_COLS)
    qn, kn, vv, conv_new = _gdn_prep(u_big3, wts['conv_w'], conv_buf, min(T, 256))
    pad = (-T) % CHUNK
    pad3 = lambda a: jnp.pad(a, ((0, 0), (0, pad), (0, 0)))
    o_b, ssm_new = _gdn_chunks(pad3(qn), pad3(kn), pad3(vv), pad3(u_big3), pad3(u_small.reshape(B, T, LANES)),
                               row(wts['a_log']), row(wts['dt_bias']), row(wts['gdn_norm_g']), ssm0, T)
    o_b = o_b[:, :T].reshape(M, D_MODEL)

    x2 = _merge_proj_ln(x1, o_a, o_b, u_big, wts['w_o'], row(wts['ln2_g']), row(wts['ln2_b']), te)
    y = _ffn_ln(x2, wts['ffn2_g'], wts['ffn2_u'], wts['ffn2_d'], row(wts['ln3_g']), row(wts['ln3_b']),
                tm, 512)
    k0, v0 = BIG_OFF['v_a'], SMALL_OFF['k_idx']
    v_rows = u_big[:, k0:k0 + SIZES['v_a']]
    ki_rows = small_rot[:, v0:v0 + IDX_DIM]
    return y, (k_rot, v_rows, ki_rows, ssm_new, conv_new)


def _split_w_in(w_in):
    offs = np.cumsum([0] + [n for _, n in IN_SPLITS])
    cols = {nm: w_in[:, offs[i]:offs[i + 1]] for i, (nm, _) in enumerate(IN_SPLITS)}
    w_big = jnp.concatenate([cols[nm] for nm in BIG_ORDER], axis=1).astype(BF16)
    w_small = jnp.concatenate([cols[nm] for nm in SMALL_ORDER], axis=1)
    w_small = jnp.pad(w_small, ((0, 0), (0, LANES - SMALL_USED))).astype(BF16)
    return w_big, w_small


def kernel(x_prompt, x_sample, cache_k, cache_v, cache_idx_k, state_ssm, state_conv, page_table, ffn1_w_gate, ffn1_w_up, ffn1_w_down, ln1_g, ln1_b, w_in, conv_w, a_log, dt_bias, gdn_norm_g, w_o, ln2_g, ln2_b, ffn2_w_gate, ffn2_w_up, ffn2_w_down, ln3_g, ln3_b):
    B, S, _ = x_prompt.shape
    DB, T, _ = x_sample.shape
    assert T == 1, "the sample path handles one new token per sequence"
    n_pages = page_table.shape[1]
    yp = x_prompt.reshape(B * S, D_MODEL)
    ys = x_sample.reshape(DB * T, D_MODEL)
    outs_p, outs_s = [], []
    for l in range(ffn1_w_gate.shape[0]):
        w_big, w_small = _split_w_in(w_in[l])
        wts = dict(
            ffn1_g=ffn1_w_gate[l].astype(BF16), ffn1_u=ffn1_w_up[l].astype(BF16), ffn1_d=ffn1_w_down[l].astype(BF16),
            ffn2_g=ffn2_w_gate[l].astype(BF16), ffn2_u=ffn2_w_up[l].astype(BF16), ffn2_d=ffn2_w_down[l].astype(BF16),
            ln1_g=ln1_g[l], ln1_b=ln1_b[l], ln2_g=ln2_g[l], ln2_b=ln2_b[l], ln3_g=ln3_g[l], ln3_b=ln3_b[l],
            w_big=w_big, w_small=w_small, w_o=w_o[l].astype(BF16), conv_w=conv_w[l],
            a_log=a_log[l], dt_bias=dt_bias[l], gdn_norm_g=gdn_norm_g[l],
        )

        def dsa_p(u_big, q_bf, k_rot, k_bf, v_bf, qi_bf, small_rot, small_bf):
            return _dsa_prompt(q_bf, qi_bf, small_rot, k_bf, v_bf, small_bf, B, S, 128)

        def dsa_s(u_big, q_bf, k_rot, k_bf, v_bf, qi_bf, small_rot, small_bf, l=l):
            w0 = SMALL_OFF['w_idx']
            w3 = small_rot[:, w0:w0 + IDX_HEADS].reshape(DB, IDX_HEADS, 1)
            qi3 = qi_bf.reshape(DB, IDX_HEADS, IDX_DIM)
            knew_i = small_bf[:, :IDX_DIM].reshape(DB, 1, IDX_DIM)
            width = N_KV_HEADS * HEAD_DIM
            past, new = _sample_scores(page_table, qi3, w3, knew_i, cache_idx_k, l)
            scores3 = jnp.concatenate([past, new], axis=1)
            n_keys = n_pages * PAGE_SIZE + T
            ktop = min(TOPK_MAX, n_keys // 4)
            thr, jmax, take_all = _sample_threshold(scores3.reshape(DB, -1), ktop, n_keys)
            v0 = BIG_OFF['v_a']
            o = _sample_attn(page_table, q_bf.reshape(DB, N_HEADS, HEAD_DIM), scores3, thr, jmax, take_all,
                             k_rot.reshape(DB, 1, width), u_big[:, v0:v0 + width].reshape(DB, 1, width),
                             cache_k, cache_v, l)
            return o.reshape(DB, D_MODEL)

        conv0 = jnp.zeros((B, CONV_W - 1, CONV_DIM), F32)
        ssm_zero = jnp.zeros((B, GDN_HEADS, GDN_DK, GDN_DV), F32)
        yp, st_p = _layer(yp, B, S, jnp.arange(S, dtype=jnp.int32), wts, conv0, ssm_zero, dsa_p)
        past_len = n_pages * PAGE_SIZE
        ys, st_s = _layer(ys, DB, T, past_len + jnp.arange(T, dtype=jnp.int32), wts, state_conv[l], state_ssm[l], dsa_s)
        outs_p.append(st_p)
        outs_s.append(st_s)

    def stack(outs, nb, nt):
        k, v, ki, ssm, conv = [jnp.stack(a) for a in zip(*outs)]
        d = len(outs)
        return (k.reshape(d, nb, nt, N_KV_HEADS, HEAD_DIM), v.reshape(d, nb, nt, N_KV_HEADS, HEAD_DIM),
                ki.reshape(d, nb, nt, IDX_DIM), ssm, conv)

    return (yp.reshape(B, S, D_MODEL), ys.reshape(DB, T, D_MODEL)) + stack(outs_p, B, S) + stack(outs_s, DB, T)
```

```python
import functools

import jax
import jax.numpy as jnp
import numpy as np
from jax import lax
from jax.experimental import pallas as pl
from jax.experimental.pallas import tpu as pltpu

D_MODEL = 2048
PAST_LEN = 16384
PAGE_SIZE = 128
HEAD_DIM = 128
N_HEADS = D_MODEL // HEAD_DIM
N_KV_HEADS = 4
GROUP = N_HEADS // N_KV_HEADS
ROPE_DIM = HEAD_DIM // 4
IDX_HEADS = 16
IDX_DIM = 64
IDX_ROPE_DIM = IDX_DIM // 4
TOPK_MAX = 256
ROPE_THETA = 500000.0
GDN_DK = 128
GDN_DV = 128
GDN_HEADS = D_MODEL // GDN_DV
CONV_W = 4
CONV_DIM = 2 * GDN_HEADS * GDN_DK + GDN_HEADS * GDN_DV
CHUNK = 64
D_FF = 5632
LN_EPS = 1e-5
NORM_EPS = 1e-6
DEPTH = 1
DN_ALPHA = (2 * DEPTH) ** 0.25

IN_SPLITS = (
    ('q_a', N_HEADS * HEAD_DIM), ('k_a', N_KV_HEADS * HEAD_DIM), ('v_a', N_KV_HEADS * HEAD_DIM),
    ('q_idx', IDX_HEADS * IDX_DIM), ('k_idx', IDX_DIM), ('w_idx', IDX_HEADS),
    ('qkv_b', CONV_DIM), ('a_b', GDN_HEADS), ('beta_b', GDN_HEADS), ('z_b', GDN_HEADS * GDN_DV),
    ('gate_a', D_MODEL), ('gate_b', D_MODEL),
)
ATTN_ORDER = ('q_a', 'k_a', 'v_a', 'q_idx')
GDN_ORDER = ('qkv_b',)
GATE_ORDER = ('z_b', 'gate_a', 'gate_b')
SMALL_ORDER = ('k_idx', 'w_idx', 'a_b', 'beta_b')
LANES = 128
VMEM_LIMIT = 56 * 1024 * 1024
NEG_BIG = -1e30
INT_MIN = -2 ** 31

F32 = jnp.float32
BF16 = jnp.bfloat16


def _offsets(order):
    sizes = dict(IN_SPLITS)
    offs, o = {}, 0
    for nm in order:
        offs[nm] = o
        o += sizes[nm]
    return offs, o


ATTN_OFF, ATTN_COLS = _offsets(ATTN_ORDER)
GATE_OFF, GATE_COLS = _offsets(GATE_ORDER)
SMALL_OFF, SMALL_USED = _offsets(SMALL_ORDER)
SIZES = dict(IN_SPLITS)


def _cparams(*sem):
    return pltpu.CompilerParams(dimension_semantics=sem, vmem_limit_bytes=VMEM_LIMIT)


def _dot(a, b):
    return jnp.dot(a, b, preferred_element_type=F32)


def _dot_nt(a, b):
    return lax.dot_general(a, b, (((1,), (1,)), ((), ())), preferred_element_type=F32)


def _dot_hi(a, b):
    return jnp.dot(a, b, preferred_element_type=F32, precision=lax.Precision.HIGHEST)


def _dot_nt_hi(a, b):
    return lax.dot_general(a, b, (((1,), (1,)), ((), ())), preferred_element_type=F32,
                           precision=lax.Precision.HIGHEST)


def _dot_tn_hi(a, b):
    return lax.dot_general(a, b, (((0,), (0,)), ((), ())), preferred_element_type=F32,
                           precision=lax.Precision.HIGHEST)


def _dot_bf16(a, b):
    return _dot(a.astype(BF16), b.astype(BF16))


def _split_bf16(a):
    hi = a.astype(BF16)
    return hi, (a - hi.astype(F32)).astype(BF16)


def _dot_split(a, b):
    ah, al = _split_bf16(a)
    bh, bl = _split_bf16(b)
    return _dot(ah, bh) + (_dot(al, bh) + _dot(ah, bl))


def _silu(x):
    return x * jax.nn.sigmoid(x)


def _layer_norm(y, g, b):
    mu = jnp.mean(y, axis=-1, keepdims=True)
    d = y - mu
    var = jnp.mean(d * d, axis=-1, keepdims=True)
    return d * lax.rsqrt(var + LN_EPS) * g + b


def _ffn_ln_kernel(x_ref, wg_ref, wu_ref, wd_ref, g_ref, b_ref, o_ref, acc_ref, xb_ref):
    j = pl.program_id(1)

    @pl.when(j == 0)
    def _():
        acc_ref[...] = jnp.zeros_like(acc_ref)
        xb_ref[...] = x_ref[...].astype(BF16)

    xb = xb_ref[...]
    hg = _dot(xb, wg_ref[...])
    hu = _dot(xb, wu_ref[...])
    h = _silu(hg) * hu
    acc_ref[...] += _dot(h.astype(BF16), wd_ref[...])

    @pl.when(j == pl.num_programs(1) - 1)
    def _():
        y = DN_ALPHA * x_ref[...] + 0.5 * acc_ref[...]
        o_ref[...] = _layer_norm(y, g_ref[...], b_ref[...])


def _ffn_ln(x, wg, wu, wd, g, b, tm, tf):
    M, D = x.shape
    F = wg.shape[1]
    return pl.pallas_call(
        _ffn_ln_kernel,
        grid=(M // tm, F // tf),
        in_specs=[
            pl.BlockSpec((tm, D), lambda i, j: (i, 0)),
            pl.BlockSpec((D, tf), lambda i, j: (0, j)),
            pl.BlockSpec((D, tf), lambda i, j: (0, j)),
            pl.BlockSpec((tf, D), lambda i, j: (j, 0)),
            pl.BlockSpec((1, D), lambda i, j: (0, 0)),
            pl.BlockSpec((1, D), lambda i, j: (0, 0)),
        ],
        out_specs=pl.BlockSpec((tm, D), lambda i, j: (i, 0)),
        out_shape=jax.ShapeDtypeStruct((M, D), F32),
        scratch_shapes=[pltpu.VMEM((tm, D), F32), pltpu.VMEM((tm, D), BF16)],
        compiler_params=_cparams("parallel", "arbitrary"),
        name="ffn_ln",
    )(x, wg, wu, wd, g, b)


def _proj_kernel(x_ref, w_ref, o_ref, xb_ref):
    @pl.when(pl.program_id(1) == 0)
    def _():
        xb_ref[...] = x_ref[...].astype(BF16)

    o_ref[...] = _dot(xb_ref[...], w_ref[...])


def _proj(x, w, tm, tn):
    M, K = x.shape
    N = w.shape[1]
    return pl.pallas_call(
        _proj_kernel,
        grid=(M // tm, N // tn),
        in_specs=[
            pl.BlockSpec((tm, K), lambda i, j: (i, 0)),
            pl.BlockSpec((K, tn), lambda i, j: (0, j)),
        ],
        out_specs=pl.BlockSpec((tm, tn), lambda i, j: (i, j)),
        out_shape=jax.ShapeDtypeStruct((M, N), F32),
        scratch_shapes=[pltpu.VMEM((tm, K), BF16)],
        compiler_params=_cparams("parallel", "arbitrary"),
        name="in_proj",
    )(x, w)


def _rope_tables(pos, rot_dim, period, live_lanes=LANES):
    half = rot_dim // 2
    inv = ROPE_THETA ** (-jnp.arange(half, dtype=F32) / half)
    ang = pos.astype(F32)[:, None] * inv[None, :]
    cos, sin = jnp.cos(ang), jnp.sin(ang)
    lane = np.arange(LANES)
    lp = lane % period
    idx = lp % half
    live = (lp < rot_dim) & (lane < live_lanes)
    c = jnp.where(live[None, :], cos[:, idx], 1.0)
    s = jnp.where(live[None, :], jnp.where((lp < half)[None, :], -sin[:, idx], sin[:, idx]), 0.0)
    return c.astype(F32), s.astype(F32)


def _rope_tile(x, c, s, half, period):
    lane = lax.broadcasted_iota(jnp.int32, x.shape, 1)
    first = (lane & (period - 1)) < half
    partner = jnp.where(first, pltpu.roll(x, LANES - half, 1), pltpu.roll(x, half, 1))
    return x * c + partner * s


def _rope_kernel(q_ref, k_ref, v_ref, qi_ref, sm_ref, ca_ref, sa_ref, ci_ref, si_ref, cs_ref, ss_ref,
                 qo_ref, ko_ref, kb_ref, vb_ref, qio_ref, smo_ref, smb_ref):
    ca, sa = ca_ref[...], sa_ref[...]
    ci, si = ci_ref[...], si_ref[...]
    for h in range(N_HEADS):
        sl = slice(h * LANES, (h + 1) * LANES)
        qo_ref[:, sl] = (_rope_tile(q_ref[:, sl], ca, sa, ROPE_DIM // 2, HEAD_DIM) * (HEAD_DIM ** -0.5)).astype(BF16)
    for h in range(N_KV_HEADS):
        sl = slice(h * LANES, (h + 1) * LANES)
        kr = _rope_tile(k_ref[:, sl], ca, sa, ROPE_DIM // 2, HEAD_DIM)
        ko_ref[:, sl] = kr
        kb_ref[:, sl] = kr.astype(BF16)
    vb_ref[...] = v_ref[...].astype(BF16)
    for h in range(IDX_HEADS * IDX_DIM // LANES):
        sl = slice(h * LANES, (h + 1) * LANES)
        qio_ref[:, sl] = _rope_tile(qi_ref[:, sl], ci, si, IDX_ROPE_DIM // 2, IDX_DIM).astype(BF16)
    sm = _rope_tile(sm_ref[...], cs_ref[...], ss_ref[...], IDX_ROPE_DIM // 2, IDX_DIM)
    smo_ref[...] = sm
    smb_ref[...] = sm.astype(BF16)


def _rope_prep(u_attn, u_small, tabs, tm):
    M = u_attn.shape[0]
    tpos = tabs[0].shape[0]
    nt = tpos // tm

    def col(name):
        w = SIZES[name]
        return pl.BlockSpec((tm, w), lambda i, o=ATTN_OFF[name] // w: (i, o))

    tab_spec = pl.BlockSpec((tm, LANES), lambda i: (i % nt, 0))
    row = lambda w: pl.BlockSpec((tm, w), lambda i: (i, 0))
    return pl.pallas_call(
        _rope_kernel,
        grid=(M // tm,),
        in_specs=[col('q_a'), col('k_a'), col('v_a'), col('q_idx'), row(LANES)] + [tab_spec] * 6,
        out_specs=[row(SIZES['q_a']), row(SIZES['k_a']), row(SIZES['k_a']), row(SIZES['v_a']),
                   row(SIZES['q_idx']), row(LANES), row(LANES)],
        out_shape=[
            jax.ShapeDtypeStruct((M, SIZES['q_a']), BF16),
            jax.ShapeDtypeStruct((M, SIZES['k_a']), F32),
            jax.ShapeDtypeStruct((M, SIZES['k_a']), BF16),
            jax.ShapeDtypeStruct((M, SIZES['v_a']), BF16),
            jax.ShapeDtypeStruct((M, SIZES['q_idx']), BF16),
            jax.ShapeDtypeStruct((M, LANES), F32),
            jax.ShapeDtypeStruct((M, LANES), BF16),
        ],
        compiler_params=_cparams("parallel"),
        name="rope_prep",
    )(u_attn, u_attn, u_attn, u_attn, u_small, *tabs)


BISECT_UNROLL = 4


def _count(pred):
    return jnp.sum(jnp.where(pred, 1.0, 0.0), axis=-1, keepdims=True)


def _tie_index(score, kidx, thr):
    return jnp.where(score == thr, kidx, jnp.int32(2 ** 31 - 1))


def _topk_threshold(score, kidx, n_allowed, k, idx_bits):
    rows = score.shape[0]
    take_all = n_allowed <= k
    lo0 = jnp.min(jnp.where(score == -jnp.inf, jnp.inf, score), axis=-1, keepdims=True)
    hi0 = jnp.max(score, axis=-1, keepdims=True)
    lo0 = jnp.where(take_all, 0.0, lo0)
    hi0 = jnp.where(take_all, 0.0, hi0)

    def step(lo, hi):
        mid = 0.5 * lo + 0.5 * hi
        ge = _count(score >= mid) >= k
        return jnp.where(ge, mid, lo), jnp.where(ge, hi, mid)

    def body(state):
        lo, hi, _ = state
        for _ in range(BISECT_UNROLL):
            lo, hi = step(lo, hi)
        mid = 0.5 * lo + 0.5 * hi
        still_open = jnp.max(jnp.where((mid > lo) & (mid < hi), 1.0, 0.0))
        return lo, hi, still_open

    lo, hi, _ = lax.while_loop(lambda state: state[2] > 0.5, body, (lo0, hi0, jnp.float32(1.0)))
    thr = jnp.where(_count(score >= hi) >= k, hi, lo)
    need = k - _count(score > thr)
    tie = _tie_index(score, kidx, thr)

    def ibody(t, j):
        cand = j + jnp.left_shift(jnp.int32(1), idx_bits - 1 - t)
        return jnp.where(_count(tie < cand) < need, cand, j)

    jmax = lax.fori_loop(0, idx_bits, ibody, jnp.zeros((rows, 1), jnp.int32))
    return thr, jmax, take_all


def _selected(score, kidx, thr, jmax, take_all):
    return take_all | (score > thr) | (_tie_index(score, kidx, thr) <= jmax)


def _dsa_prompt_kernel(q_ref, qi_ref, sm_ref, k_ref, v_ref, kis_ref, o_ref, *, ktop, key_step):
    tq = q_ref.shape[0]
    S = k_ref.shape[0]
    i = pl.program_id(1)
    w0 = SMALL_OFF['w_idx']

    def attend(L):
        ki = kis_ref[:L, :IDX_DIM]
        w = sm_ref[:, w0:w0 + IDX_HEADS] * (IDX_HEADS ** -0.5 * IDX_DIM ** -0.5)
        score = jnp.zeros((tq, L), F32)
        for h in range(IDX_HEADS):
            s = _dot_nt(qi_ref[:, h * IDX_DIM:(h + 1) * IDX_DIM], ki)
            score = score + w[:, h:h + 1] * jnp.maximum(s, 0.0)
        qpos = i * tq + lax.broadcasted_iota(jnp.int32, (tq, 1), 0)
        kidx = lax.broadcasted_iota(jnp.int32, (tq, L), 1)
        allowed = kidx <= qpos
        score = jnp.where(allowed, score, -jnp.inf)
        thr, jmax, take_all = _topk_threshold(score, kidx, qpos + 1, ktop, int(L - 1).bit_length())
        bias = jnp.where(_selected(score, kidx, thr, jmax, take_all) & allowed, 0.0, -jnp.inf)
        for h in range(N_HEADS):
            n = h // GROUP
            sl = slice(h * HEAD_DIM, (h + 1) * HEAD_DIM)
            kv = slice(n * HEAD_DIM, (n + 1) * HEAD_DIM)
            s = _dot_nt(q_ref[:, sl], k_ref[:L, kv]) + bias
            m = jnp.max(s, axis=-1, keepdims=True)
            p = jnp.exp(s - m)
            l = jnp.sum(p, axis=-1, keepdims=True)
            o_ref[:, sl] = _dot(p.astype(BF16), v_ref[:L, kv]) / l

    level = ((i + 1) * tq - 1) // key_step
    for lv in range(S // key_step):
        pl.when(level == lv)(functools.partial(attend, (lv + 1) * key_step))


def _dsa_prompt(q_bf, qi_bf, small_rot, k_bf, v_bf, small_bf, B, S, tq):
    ktop = min(TOPK_MAX, S // 4)
    nq = S // tq
    key_step = min(S, 512)
    row = lambda w: pl.BlockSpec((tq, w), lambda b, i: (b * nq + i, 0))
    full = lambda w: pl.BlockSpec((S, w), lambda b, i: (b, 0))
    return pl.pallas_call(
        functools.partial(_dsa_prompt_kernel, ktop=ktop, key_step=key_step),
        grid=(B, nq),
        in_specs=[row(q_bf.shape[1]), row(qi_bf.shape[1]), row(LANES),
                  full(k_bf.shape[1]), full(v_bf.shape[1]), full(LANES)],
        out_specs=row(q_bf.shape[1]),
        out_shape=jax.ShapeDtypeStruct((B * S, q_bf.shape[1]), F32),
        compiler_params=_cparams("parallel", "arbitrary"),
        name="dsa_prompt",
    )(q_bf, qi_bf, small_rot, k_bf, v_bf, small_bf)


def _idx_score_rows(qi, w, kpage):
    s = _dot_nt(qi, kpage)
    return jnp.sum(w * jnp.maximum(s, 0.0), axis=0, keepdims=True)


def _sample_scores_kernel(pt_ref, qi_ref, w_ref, knew_ref, *refs):
    page_refs, (o_ref, onew_ref) = refs[:-2], refs[-2:]
    npp = len(page_refs)
    p = pl.program_id(1)
    qi = qi_ref[0]
    w = w_ref[0] * (IDX_HEADS ** -0.5 * IDX_DIM ** -0.5)
    for j, page_ref in enumerate(page_refs):
        o_ref[0, pl.ds(p * npp + j, 1), :] = _idx_score_rows(qi, w, page_ref[0, 0].astype(BF16))

    @pl.when(p == 0)
    def _():
        kn = jnp.broadcast_to(knew_ref[0], (PAGE_SIZE, IDX_DIM))
        sc = _idx_score_rows(qi, w, kn)
        lane = lax.broadcasted_iota(jnp.int32, (1, PAGE_SIZE), 1)
        onew_ref[0] = jnp.where(lane == 0, sc, -jnp.inf)


def _page_specs(row_shape, npp, layer):
    zeros = (0,) * (1 + len(row_shape))
    return [pl.BlockSpec((1, 1, PAGE_SIZE) + row_shape,
                         lambda b, p, pt, j=j: (layer, pt[b, p * npp + j]) + zeros)
            for j in range(npp)]


def _pages_per_step(n_pages, cap):
    npp = min(cap, n_pages)
    while n_pages % npp:
        npp -= 1
    return npp


def _sample_scores(page_table, qi3, w3, knew3, cache_ki, layer):
    DB, n_pages = page_table.shape
    npp = _pages_per_step(n_pages, 16)
    return pl.pallas_call(
        _sample_scores_kernel,
        grid_spec=pltpu.PrefetchScalarGridSpec(
            num_scalar_prefetch=1,
            grid=(DB, n_pages // npp),
            in_specs=[
                pl.BlockSpec((1, IDX_HEADS, IDX_DIM), lambda b, p, pt: (b, 0, 0)),
                pl.BlockSpec((1, IDX_HEADS, 1), lambda b, p, pt: (b, 0, 0)),
                pl.BlockSpec((1, 1, IDX_DIM), lambda b, p, pt: (b, 0, 0)),
            ] + _page_specs((IDX_DIM,), npp, layer),
            out_specs=[
                pl.BlockSpec((1, n_pages, PAGE_SIZE), lambda b, p, pt: (b, 0, 0)),
                pl.BlockSpec((1, 1, PAGE_SIZE), lambda b, p, pt: (b, 0, 0)),
            ],
        ),
        out_shape=[jax.ShapeDtypeStruct((DB, n_pages, PAGE_SIZE), F32),
                   jax.ShapeDtypeStruct((DB, 1, PAGE_SIZE), F32)],
        compiler_params=_cparams("parallel", "arbitrary"),
        name="sample_scores",
    )(page_table, qi3, w3, knew3, *([cache_ki] * npp))


def _sample_thr_kernel(s_ref, thr_ref, jmax_ref, all_ref, *, ktop, n_valid, idx_bits):
    kidx = lax.broadcasted_iota(jnp.int32, s_ref.shape, 1)
    score = jnp.where(kidx < n_valid, s_ref[...], -jnp.inf)
    n_allowed = jnp.full((s_ref.shape[0], 1), n_valid, jnp.int32)
    thr, jmax, take_all = _topk_threshold(score, kidx, n_allowed, ktop, idx_bits)
    thr_ref[...] = jnp.broadcast_to(thr, thr_ref.shape)
    jmax_ref[...] = jnp.broadcast_to(jmax, jmax_ref.shape)
    all_ref[...] = jnp.broadcast_to(take_all.astype(jnp.int32), all_ref.shape)


def _sample_threshold(scores, ktop, n_valid):
    DB, L = scores.shape
    out = lambda dt: jax.ShapeDtypeStruct((DB, LANES), dt)
    return pl.pallas_call(
        functools.partial(_sample_thr_kernel, ktop=ktop, n_valid=n_valid, idx_bits=int(L - 1).bit_length()),
        out_shape=[out(F32), out(jnp.int32), out(jnp.int32)],
        compiler_params=pltpu.CompilerParams(vmem_limit_bytes=VMEM_LIMIT),
        name="sample_threshold",
    )(scores)


def _sample_attn_kernel(pt_ref, q_ref, s_ref, thr_ref, jmax_ref, all_ref, knew_ref, vnew_ref, *refs, n_pages):
    npp = (len(refs) - 4) // 2
    kp_refs, vp_refs = refs[:npp], refs[npp:2 * npp]
    o_ref, m_ref, l_ref, acc_ref = refs[2 * npp:]
    b = pl.program_id(0)
    p = pl.program_id(1)

    @pl.when(p == 0)
    def _():
        m_ref[...] = jnp.full_like(m_ref, NEG_BIG)
        l_ref[...] = jnp.zeros_like(l_ref)
        acc_ref[...] = jnp.zeros_like(acc_ref)

    thr = thr_ref[pl.ds(b, 1), 0:1]
    jmax = jmax_ref[pl.ds(b, 1), 0:1]
    take_all = all_ref[pl.ds(b, 1), 0:1] > 0

    q = q_ref[0]
    group = lax.broadcasted_iota(jnp.int32, (N_HEADS, 1), 0) // GROUP
    X = PAGE_SIZE * N_KV_HEADS
    own_head = (lax.broadcasted_iota(jnp.int32, (N_HEADS, X), 1) % N_KV_HEADS) == group
    repeat = (lax.broadcasted_iota(jnp.int32, (PAGE_SIZE, X), 1) // N_KV_HEADS
              == lax.broadcasted_iota(jnp.int32, (PAGE_SIZE, X), 0)).astype(BF16)

    def fold(logits, pv_fns):
        m_old = m_ref[...]
        m_new = functools.reduce(jnp.maximum, [jnp.max(s, axis=-1, keepdims=True) for s in logits], m_old)
        corr = jnp.exp(m_old - m_new)
        probs = [jnp.exp(s - m_new) for s in logits]
        l_ref[...] = l_ref[...] * corr + functools.reduce(
            lambda a, c: a + c, [jnp.sum(pr, axis=-1, keepdims=True) for pr in probs])
        acc_ref[...] = acc_ref[...] * corr + functools.reduce(
            lambda a, c: a + c, [fn(pr) for fn, pr in zip(pv_fns, probs)])
        m_ref[...] = m_new

    logits, pv_fns = [], []
    for j in range(npp):
        page = p * npp + j
        scores_row = s_ref[0, pl.ds(page, 1), :]
        kidx = page * PAGE_SIZE + lax.broadcasted_iota(jnp.int32, (1, PAGE_SIZE), 1)
        sel = _selected(scores_row, kidx, thr, jmax, take_all)
        sel_rows = _dot(jnp.broadcast_to(jnp.where(sel, 1.0, 0.0), (N_HEADS, PAGE_SIZE)).astype(BF16), repeat)
        s = _dot_nt(q, kp_refs[j][0].astype(BF16))
        logits.append(jnp.where(own_head & (sel_rows > 0.5), s, -jnp.inf))
        pv_fns.append(lambda pr, ref=vp_refs[j]: _dot(pr.astype(BF16), ref[0].astype(BF16)))
    fold(logits, pv_fns)

    @pl.when(p == pl.num_programs(1) - 1)
    def _():
        def per_head(ref):
            rows = [jnp.where(group == n, jnp.broadcast_to(ref[0, :, n * HEAD_DIM:(n + 1) * HEAD_DIM],
                                                           (N_HEADS, HEAD_DIM)), 0.0) for n in range(N_KV_HEADS)]
            return functools.reduce(lambda a, c: a + c, rows).astype(BF16).astype(F32)

        score_new = s_ref[0, pl.ds(n_pages, 1), 0:1]
        sel = _selected(score_new, jnp.full((1, 1), n_pages * PAGE_SIZE, jnp.int32), thr, jmax, take_all)
        s = jnp.sum(q.astype(F32) * per_head(knew_ref), axis=-1, keepdims=True)
        s = jnp.where(sel, s, -jnp.inf)
        vexp = per_head(vnew_ref)
        fold([s], [lambda pr: pr.astype(BF16).astype(F32) * vexp])
        o_ref[0] = acc_ref[...] / l_ref[...]


def _sample_attn(page_table, q3, scores3, thr, jmax, take_all, knew3, vnew3, cache_k, cache_v, layer):
    DB, n_pages = page_table.shape
    width = N_KV_HEADS * HEAD_DIM
    npp = _pages_per_step(n_pages, 16)
    bsel = lambda *shape: pl.BlockSpec((1,) + shape, lambda b, p, pt: (b,) + (0,) * len(shape))
    whole = pl.BlockSpec((DB, LANES), lambda b, p, pt: (0, 0))
    n_phys = cache_k.shape[1]
    rows = PAGE_SIZE * N_KV_HEADS
    cache_k = cache_k.reshape(-1, rows, HEAD_DIM)
    cache_v = cache_v.reshape(-1, rows, HEAD_DIM)
    pages = [pl.BlockSpec((1, rows, HEAD_DIM),
                          lambda b, p, pt, j=j: (layer * n_phys + pt[b, p * npp + j], 0, 0)) for j in range(npp)]
    return pl.pallas_call(
        functools.partial(_sample_attn_kernel, n_pages=n_pages),
        grid_spec=pltpu.PrefetchScalarGridSpec(
            num_scalar_prefetch=1,
            grid=(DB, n_pages // npp),
            in_specs=[bsel(N_HEADS, HEAD_DIM), bsel(n_pages + 1, PAGE_SIZE), whole, whole, whole,
                      bsel(1, width), bsel(1, width)] + pages + pages,
            out_specs=bsel(N_HEADS, HEAD_DIM),
            scratch_shapes=[pltpu.VMEM((N_HEADS, 1), F32), pltpu.VMEM((N_HEADS, 1), F32),
                            pltpu.VMEM((N_HEADS, HEAD_DIM), F32)],
        ),
        out_shape=jax.ShapeDtypeStruct((DB, N_HEADS, HEAD_DIM), F32),
        compiler_params=_cparams("parallel", "arbitrary"),
        name="sample_attn",
    )(page_table, q3, scores3, thr, jmax, take_all, knew3, vnew3, *([cache_k] * npp), *([cache_v] * npp))


CARRY = 8
MXU_DIM = 256
GDN_GROUP = MXU_DIM // CHUNK
GDN_SPLIT_STAGES = 2


def _spread(a, row_head, hg):
    return jnp.concatenate([jnp.where(row_head == i, a, 0.0) for i in range(hg)], axis=1)


def _gdn_prep_kernel(x_ref, w_ref, buf_ref, q_ref, k_ref, v_ref, conv_ref, xpad_ref):
    tt = x_ref.shape[1]
    t = pl.program_id(1)
    lo = CARRY - (CONV_W - 1)

    @pl.when(t == 0)
    def _():
        xpad_ref[lo:CARRY, :] = buf_ref[0]

    xpad_ref[CARRY:CARRY + tt, :] = x_ref[0]
    nh = GDN_HEADS
    for c in range(CONV_DIM // LANES):
        sl = slice(c * LANES, (c + 1) * LANES)
        y = w_ref[0:1, sl] * xpad_ref[lo:lo + tt, sl]
        for j in range(1, CONV_W):
            y = y + w_ref[j:j + 1, sl] * xpad_ref[lo + j:lo + j + tt, sl]
        y = _silu(y)
        if c < 2 * nh:
            y = y * lax.rsqrt(jnp.sum(y * y, axis=-1, keepdims=True) + NORM_EPS)
        if c < nh:
            q_ref[0, :, sl] = y * (GDN_DK ** -0.5)
        elif c < 2 * nh:
            k_ref[0, :, slice((c - nh) * LANES, (c - nh + 1) * LANES)] = y
        else:
            v_ref[0, :, slice((c - 2 * nh) * LANES, (c - 2 * nh + 1) * LANES)] = y
    last = xpad_ref[lo + tt:CARRY + tt, :]
    xpad_ref[lo:CARRY, :] = last

    @pl.when(t == pl.num_programs(1) - 1)
    def _():
        conv_ref[0] = last


def _gdn_prep(u_gdn3, conv_w, buf, tt):
    B, T, _ = u_gdn3.shape
    w = SIZES['qkv_b']
    hd = GDN_HEADS * GDN_DK
    out = jax.ShapeDtypeStruct((B, T, hd), F32)
    ospec = pl.BlockSpec((1, tt, hd), lambda b, t: (b, t, 0))
    return pl.pallas_call(
        _gdn_prep_kernel,
        grid=(B, T // tt),
        in_specs=[
            pl.BlockSpec((1, tt, w), lambda b, t: (b, t, 0)),
            pl.BlockSpec((CONV_W, w), lambda b, t: (0, 0)),
            pl.BlockSpec((1, CONV_W - 1, w), lambda b, t: (b, 0, 0)),
        ],
        out_specs=[ospec, ospec, ospec, pl.BlockSpec((1, CONV_W - 1, w), lambda b, t: (b, 0, 0))],
        out_shape=[out, out, out, jax.ShapeDtypeStruct((B, CONV_W - 1, w), F32)],
        scratch_shapes=[pltpu.VMEM((CARRY + tt, w), F32)],
        compiler_params=_cparams("parallel", "arbitrary"),
        name="gdn_prep",
    )(u_gdn3, conv_w, buf)


def _gdn_chunk_kernel(q_ref, k_ref, v_ref, z_ref, sm_ref, alog_ref, dtb_ref, gn_ref, s0_ref,
                      o_ref, sout_ref, state_ref, *, t_valid):
    C = q_ref.shape[1]
    c = pl.program_id(1)

    @pl.when(c == 0)
    def _():
        state_ref[...] = s0_ref[0]

    H = GDN_HEADS
    a0, b0 = SMALL_OFF['a_b'], SMALL_OFF['beta_b']
    live = (c * C + lax.broadcasted_iota(jnp.int32, (C, H), 0)) < t_valid
    xs = sm_ref[0, :, a0:a0 + H] + dtb_ref[...]
    softplus = jnp.maximum(xs, 0.0) + jnp.log1p(jnp.exp(-jnp.abs(xs)))
    g_all = jnp.where(live, -jnp.exp(alog_ref[...]) * softplus, 0.0)
    beta_all = jnp.where(live, jax.nn.sigmoid(sm_ref[0, :, b0:b0 + H]), 0.0)
    tri_f = (lax.broadcasted_iota(jnp.int32, (C, C), 0) >= lax.broadcasted_iota(jnp.int32, (C, C), 1)).astype(F32)
    eye_h = (lax.broadcasted_iota(jnp.int32, (H, H), 0) == lax.broadcasted_iota(jnp.int32, (H, H), 1)).astype(F32)
    gcum = _dot_hi(tri_f, g_all)
    gcum_t = _dot_nt_hi(eye_h, gcum)

    HG = GDN_GROUP
    R = HG * C
    ri = lax.broadcasted_iota(jnp.int32, (R, R), 0)
    ci = lax.broadcasted_iota(jnp.int32, (R, R), 1)
    same_head = (ri // C) == (ci // C)
    mask_incl = same_head & (ri >= ci)
    mask_strict = same_head & (ri > ci)
    row_head = lax.broadcasted_iota(jnp.int32, (R, 1), 0) // C
    row_head2 = jnp.concatenate([row_head, row_head], axis=0)

    for grp in range(H // HG):
        heads = range(grp * HG, (grp + 1) * HG)
        sls = [slice(h * GDN_DK, (h + 1) * GDN_DK) for h in heads]
        rows = lambda ref: jnp.concatenate([ref[0, :, sl] for sl in sls], axis=0)
        cols = lambda a: jnp.concatenate([a[:, h:h + 1] for h in heads], axis=0)
        q, k, v = rows(q_ref), rows(k_ref), rows(v_ref)
        beta = cols(beta_all)
        gcol = cols(gcum)
        grow = jnp.concatenate([gcum_t[h:h + 1, :] for h in heads], axis=1)
        glast = jnp.concatenate([jnp.broadcast_to(gcum[C - 1:C, h:h + 1], (C, 1)) for h in heads], axis=0)
        decay = jnp.where(mask_incl, jnp.exp(jnp.where(mask_incl, gcol - grow, 0.0)), 0.0)
        kb = k * beta
        kk_qk = _dot_nt(jnp.concatenate([kb, q], axis=0).astype(BF16), k.astype(BF16))
        nmat = jnp.where(mask_strict, kk_qk[:R] * decay, 0.0)
        qk = kk_qk[R:] * decay
        x = jnp.concatenate([v * beta, kb * jnp.exp(gcol)], axis=1)
        pw = nmat
        x = x - _dot_split(pw, x)
        for stage in range(max(C - 1, 1).bit_length() - 1):
            mm = _dot_split if stage < GDN_SPLIT_STAGES else _dot_bf16
            pw = mm(pw, pw)
            x = x + mm(pw, x)
        u, w = x[:, :GDN_DV], x[:, GDN_DV:]
        s_stack = state_ref[grp * HG:(grp + 1) * HG].reshape(HG * GDN_DK, GDN_DV)
        w_q = jnp.concatenate([w, q * jnp.exp(gcol)], axis=0)
        ws_qs = _dot(_spread(w_q, row_head2, HG).astype(BF16), s_stack.astype(BF16))
        v_new = u - ws_qs[:R]
        o = ws_qs[R:] + _dot(qk.astype(BF16), v_new.astype(BF16))
        kdec_t = (k * jnp.exp(glast - gcol)).T
        s_add = _dot(kdec_t.astype(BF16), _spread(v_new, row_head, HG).astype(BF16))
        on = o * lax.rsqrt(jnp.mean(o * o, axis=-1, keepdims=True) + NORM_EPS) * gn_ref[...]
        for a, h in enumerate(heads):
            state_ref[h] = (state_ref[h] * jnp.exp(gcum[C - 1:C, h:h + 1])
                            + s_add[:, a * GDN_DV:(a + 1) * GDN_DV])
            o_ref[0, :, sls[a]] = on[a * C:(a + 1) * C] * _silu(z_ref[0, :, sls[a]])

    @pl.when(c == pl.num_programs(1) - 1)
    def _():
        sout_ref[0] = state_ref[...]


def _gdn_chunks(qn, kn, vv, u_gate3, small3, a_log, dt_bias, gn, s0, t_valid):
    B, Tp, hd = qn.shape
    nc = Tp // CHUNK
    blk = pl.BlockSpec((1, CHUNK, hd), lambda b, c: (b, c, 0))
    vec = lambda w: pl.BlockSpec((1, w), lambda b, c: (0, 0))
    st = pl.BlockSpec((1, GDN_HEADS, GDN_DK, GDN_DV), lambda b, c: (b, 0, 0, 0))
    return pl.pallas_call(
        functools.partial(_gdn_chunk_kernel, t_valid=t_valid),
        grid=(B, nc),
        in_specs=[blk, blk, blk,
                  pl.BlockSpec((1, CHUNK, hd), lambda b, c, o=GATE_OFF['z_b'] // hd: (b, c, o)),
                  pl.BlockSpec((1, CHUNK, LANES), lambda b, c: (b, c, 0)),
                  vec(GDN_HEADS), vec(GDN_HEADS), vec(GDN_DV), st],
        out_specs=[blk, st],
        out_shape=[jax.ShapeDtypeStruct((B, Tp, hd), F32),
                   jax.ShapeDtypeStruct((B, GDN_HEADS, GDN_DK, GDN_DV), F32)],
        scratch_shapes=[pltpu.VMEM((GDN_HEADS, GDN_DK, GDN_DV), F32)],
        compiler_params=_cparams("parallel", "arbitrary"),
        name="gdn_chunks",
    )(qn, kn, vv, u_gate3, small3, a_log, dt_bias, gn, s0)


def _merge_kernel(x_ref, oa_ref, ob_ref, ga_ref, gb_ref, wo_ref, g_ref, b_ref, o_ref):
    merged = jax.nn.sigmoid(ga_ref[...]) * oa_ref[...] + jax.nn.sigmoid(gb_ref[...]) * ob_ref[...]
    y = DN_ALPHA * x_ref[...] + _dot(merged.astype(BF16), wo_ref[...])
    o_ref[...] = _layer_norm(y, g_ref[...], b_ref[...])


def _merge_proj_ln(x, o_a, o_b, u_gate, w_o, g, b, tm):
    M, D = x.shape
    row = pl.BlockSpec((tm, D), lambda i: (i, 0))
    col = lambda name: pl.BlockSpec((tm, D), lambda i, o=GATE_OFF[name] // D: (i, o))
    vec = pl.BlockSpec((1, D), lambda i: (0, 0))
    return pl.pallas_call(
        _merge_kernel,
        grid=(M // tm,),
        in_specs=[row, row, row, col('gate_a'), col('gate_b'),
                  pl.BlockSpec((D, D), lambda i: (0, 0)), vec, vec],
        out_specs=row,
        out_shape=jax.ShapeDtypeStruct((M, D), F32),
        compiler_params=_cparams("parallel"),
        name="merge_proj_ln",
    )(x, o_a, o_b, u_gate, u_gate, w_o, g, b)


def _tiles(M):
    return (512, 256) if M % 512 == 0 else (M, M)


def _layer(x, B, T, pos, wts, conv_buf, ssm0, dsa_fn):
    M = B * T
    tm, te = _tiles(M)
    row = lambda a: a.reshape(1, -1)
    x1 = _ffn_ln(x, wts['ffn1_g'], wts['ffn1_u'], wts['ffn1_d'], row(wts['ln1_g']), row(wts['ln1_b']),
                 tm, 512)
    u_attn = _proj(x1, wts['w_attn'], tm, 1024)
    u_gdn = _proj(x1, wts['w_gdn'], tm, 1024)
    u_gate = _proj(x1, wts['w_gate'], tm, 1024)
    u_small = _proj(x1, wts['w_small'], tm, LANES)

    pos_rows = pos if T > 1 else jnp.broadcast_to(pos, (M,))
    tabs = (_rope_tables(pos_rows, ROPE_DIM, HEAD_DIM) + _rope_tables(pos_rows, IDX_ROPE_DIM, IDX_DIM)
            + _rope_tables(pos_rows, IDX_ROPE_DIM, IDX_DIM, live_lanes=IDX_DIM))
    q_bf, k_rot, k_bf, v_bf, qi_bf, small_rot, small_bf = _rope_prep(u_attn, u_small, tabs, te)
    v_rows = u_attn[:, ATTN_OFF['v_a']:ATTN_OFF['v_a'] + SIZES['v_a']]
    o_a = dsa_fn(v_rows, q_bf, k_rot, k_bf, v_bf, qi_bf, small_rot, small_bf)

    qn, kn, vv, conv_new = _gdn_prep(u_gdn.reshape(B, T, -1), wts['conv_w'], conv_buf, min(T, 256))
    pad = (-T) % CHUNK
    pad3 = lambda a: jnp.pad(a, ((0, 0), (0, pad), (0, 0))) if pad else a
    o_b, ssm_new = _gdn_chunks(pad3(qn), pad3(kn), pad3(vv), pad3(u_gate.reshape(B, T, -1)),
                               pad3(u_small.reshape(B, T, LANES)),
                               row(wts['a_log']), row(wts['dt_bias']), row(wts['gdn_norm_g']), ssm0, T)
    o_b = o_b[:, :T].reshape(M, D_MODEL)

    x2 = _merge_proj_ln(x1, o_a, o_b, u_gate, wts['w_o'], row(wts['ln2_g']), row(wts['ln2_b']), te)
    y = _ffn_ln(x2, wts['ffn2_g'], wts['ffn2_u'], wts['ffn2_d'], row(wts['ln3_g']), row(wts['ln3_b']),
                tm, 512)
    ki_rows = small_rot[:, SMALL_OFF['k_idx']:SMALL_OFF['k_idx'] + IDX_DIM]
    return y, (k_rot, v_rows, ki_rows, ssm_new, conv_new)


def _split_w_in(w_in):
    offs = dict(zip([nm for nm, _ in IN_SPLITS], np.cumsum([0] + [n for _, n in IN_SPLITS])))

    def span(order):
        lo = offs[order[0]]
        hi = offs[order[-1]] + SIZES[order[-1]]
        assert hi - lo == sum(SIZES[nm] for nm in order)
        return w_in[:, lo:hi].astype(BF16)

    w_small = jnp.concatenate([w_in[:, offs[nm]:offs[nm] + SIZES[nm]] for nm in SMALL_ORDER], axis=1)
    w_small = jnp.pad(w_small, ((0, 0), (0, LANES - SMALL_USED))).astype(BF16)
    return span(ATTN_ORDER), span(GDN_ORDER), span(GATE_ORDER), w_small


def kernel(x_prompt, x_sample, cache_k, cache_v, cache_idx_k, state_ssm, state_conv, page_table, ffn1_w_gate, ffn1_w_up, ffn1_w_down, ln1_g, ln1_b, w_in, conv_w, a_log, dt_bias, gdn_norm_g, w_o, ln2_g, ln2_b, ffn2_w_gate, ffn2_w_up, ffn2_w_down, ln3_g, ln3_b):
    B, S, _ = x_prompt.shape
    DB, T, _ = x_sample.shape
    assert T == 1, "the sample path handles one new token per sequence"
    n_pages = page_table.shape[1]
    yp = x_prompt.reshape(B * S, D_MODEL)
    ys = x_sample.reshape(DB * T, D_MODEL)
    outs_p, outs_s = [], []
    for l in range(ffn1_w_gate.shape[0]):
        w_attn, w_gdn, w_gate, w_small = _split_w_in(w_in[l])
        wts = dict(
            ffn1_g=ffn1_w_gate[l].astype(BF16), ffn1_u=ffn1_w_up[l].astype(BF16), ffn1_d=ffn1_w_down[l].astype(BF16),
            ffn2_g=ffn2_w_gate[l].astype(BF16), ffn2_u=ffn2_w_up[l].astype(BF16), ffn2_d=ffn2_w_down[l].astype(BF16),
            ln1_g=ln1_g[l], ln1_b=ln1_b[l], ln2_g=ln2_g[l], ln2_b=ln2_b[l], ln3_g=ln3_g[l], ln3_b=ln3_b[l],
            w_attn=w_attn, w_gdn=w_gdn, w_gate=w_gate, w_small=w_small, w_o=w_o[l].astype(BF16), conv_w=conv_w[l],
            a_log=a_log[l], dt_bias=dt_bias[l], gdn_norm_g=gdn_norm_g[l],
        )

        def dsa_p(v_rows, q_bf, k_rot, k_bf, v_bf, qi_bf, small_rot, small_bf):
            return _dsa_prompt(q_bf, qi_bf, small_rot, k_bf, v_bf, small_bf, B, S, 128)

        def dsa_s(v_rows, q_bf, k_rot, k_bf, v_bf, qi_bf, small_rot, small_bf, l=l):
            w0 = SMALL_OFF['w_idx']
            w3 = small_rot[:, w0:w0 + IDX_HEADS].reshape(DB, IDX_HEADS, 1)
            qi3 = qi_bf.reshape(DB, IDX_HEADS, IDX_DIM)
            knew_i = small_bf[:, :IDX_DIM].reshape(DB, 1, IDX_DIM)
            width = N_KV_HEADS * HEAD_DIM
            past, new = _sample_scores(page_table, qi3, w3, knew_i, cache_idx_k, l)
            scores3 = jnp.concatenate([past, new], axis=1)
            n_keys = n_pages * PAGE_SIZE + T
            ktop = min(TOPK_MAX, n_keys // 4)
            thr, jmax, take_all = _sample_threshold(scores3.reshape(DB, -1), ktop, n_keys)
            o = _sample_attn(page_table, q_bf.reshape(DB, N_HEADS, HEAD_DIM), scores3, thr, jmax, take_all,
                             k_rot.reshape(DB, 1, width), v_rows.reshape(DB, 1, width), cache_k, cache_v, l)
            return o.reshape(DB, D_MODEL)

        conv0 = jnp.zeros((B, CONV_W - 1, CONV_DIM), F32)
        ssm_zero = jnp.zeros((B, GDN_HEADS, GDN_DK, GDN_DV), F32)
        yp, st_p = _layer(yp, B, S, jnp.arange(S, dtype=jnp.int32), wts, conv0, ssm_zero, dsa_p)
        past_len = n_pages * PAGE_SIZE
        ys, st_s = _layer(ys, DB, T, past_len + jnp.arange(T, dtype=jnp.int32), wts, state_conv[l], state_ssm[l], dsa_s)
        outs_p.append(st_p)
        outs_s.append(st_s)

    def stack(outs, nb, nt):
        d = len(outs)
        k, v, ki, ssm, conv = [a[0][None] if d == 1 else jnp.stack(a) for a in zip(*outs)]
        return (k.reshape(d, nb, nt, N_KV_HEADS, HEAD_DIM), v.reshape(d, nb, nt, N_KV_HEADS, HEAD_DIM),
                ki.reshape(d, nb, nt, IDX_DIM), ssm, conv)

    return (yp.reshape(B, S, D_MODEL), ys.reshape(DB, T, D_MODEL)) + stack(outs_p, B, S) + stack(outs_s, DB, T)
```

```python
import functools

import jax
import jax.numpy as jnp
import numpy as np
from jax import lax
from jax.experimental import pallas as pl
from jax.experimental.pallas import tpu as pltpu

D_MODEL = 2048
PAST_LEN = 16384
PAGE_SIZE = 128
HEAD_DIM = 128
N_HEADS = D_MODEL // HEAD_DIM
N_KV_HEADS = 4
GROUP = N_HEADS // N_KV_HEADS
ROPE_DIM = HEAD_DIM // 4
IDX_HEADS = 16
IDX_DIM = 64
IDX_ROPE_DIM = IDX_DIM // 4
TOPK_MAX = 256
ROPE_THETA = 500000.0
GDN_DK = 128
GDN_DV = 128
GDN_HEADS = D_MODEL // GDN_DV
CONV_W = 4
CONV_DIM = 2 * GDN_HEADS * GDN_DK + GDN_HEADS * GDN_DV
CHUNK = 64
D_FF = 5632
LN_EPS = 1e-5
NORM_EPS = 1e-6
DEPTH = 1
DN_ALPHA = (2 * DEPTH) ** 0.25

IN_SPLITS = (
    ('q_a', N_HEADS * HEAD_DIM), ('k_a', N_KV_HEADS * HEAD_DIM), ('v_a', N_KV_HEADS * HEAD_DIM),
    ('q_idx', IDX_HEADS * IDX_DIM), ('k_idx', IDX_DIM), ('w_idx', IDX_HEADS),
    ('qkv_b', CONV_DIM), ('a_b', GDN_HEADS), ('beta_b', GDN_HEADS), ('z_b', GDN_HEADS * GDN_DV),
    ('gate_a', D_MODEL), ('gate_b', D_MODEL),
)
ATTN_ORDER = ('q_a', 'k_a', 'v_a', 'q_idx')
GDN_ORDER = ('qkv_b',)
GATE_ORDER = ('z_b', 'gate_a', 'gate_b')
SMALL_ORDER = ('k_idx', 'w_idx', 'a_b', 'beta_b')
LANES = 128
VMEM_LIMIT = 56 * 1024 * 1024
NEG_BIG = -1e30
INT_MIN = -2 ** 31

F32 = jnp.float32
BF16 = jnp.bfloat16


def _offsets(order):
    sizes = dict(IN_SPLITS)
    offs, o = {}, 0
    for nm in order:
        offs[nm] = o
        o += sizes[nm]
    return offs, o


ATTN_OFF, ATTN_COLS = _offsets(ATTN_ORDER)
GATE_OFF, GATE_COLS = _offsets(GATE_ORDER)
SMALL_OFF, SMALL_USED = _offsets(SMALL_ORDER)
SIZES = dict(IN_SPLITS)


def _cparams(*sem):
    return pltpu.CompilerParams(dimension_semantics=sem, vmem_limit_bytes=VMEM_LIMIT)


def _dot(a, b):
    return jnp.dot(a, b, preferred_element_type=F32)


def _dot_nt(a, b):
    return lax.dot_general(a, b, (((1,), (1,)), ((), ())), preferred_element_type=F32)


def _dot_hi(a, b):
    return jnp.dot(a, b, preferred_element_type=F32, precision=lax.Precision.HIGHEST)


def _dot_nt_hi(a, b):
    return lax.dot_general(a, b, (((1,), (1,)), ((), ())), preferred_element_type=F32,
                           precision=lax.Precision.HIGHEST)


def _dot_tn_hi(a, b):
    return lax.dot_general(a, b, (((0,), (0,)), ((), ())), preferred_element_type=F32,
                           precision=lax.Precision.HIGHEST)


def _dot_bf16(a, b):
    return _dot(a.astype(BF16), b.astype(BF16))


def _split_bf16(a):
    hi = a.astype(BF16)
    return hi, (a - hi.astype(F32)).astype(BF16)


def _dot_split(a, b):
    ah, al = _split_bf16(a)
    bh, bl = _split_bf16(b)
    return _dot(ah, bh) + (_dot(al, bh) + _dot(ah, bl))


def _silu(x):
    return x * jax.nn.sigmoid(x)


def _layer_norm(y, g, b):
    mu = jnp.mean(y, axis=-1, keepdims=True)
    d = y - mu
    var = jnp.mean(d * d, axis=-1, keepdims=True)
    return d * lax.rsqrt(var + LN_EPS) * g + b


def _ffn_ln_kernel(x_ref, wg_ref, wu_ref, wd_ref, g_ref, b_ref, o_ref, acc_ref, xb_ref):
    j = pl.program_id(1)

    @pl.when(j == 0)
    def _():
        acc_ref[...] = jnp.zeros_like(acc_ref)
        xb_ref[...] = x_ref[...].astype(BF16)

    xb = xb_ref[...]
    hg = _dot(xb, wg_ref[...])
    hu = _dot(xb, wu_ref[...])
    h = _silu(hg) * hu
    acc_ref[...] += _dot(h.astype(BF16), wd_ref[...])

    @pl.when(j == pl.num_programs(1) - 1)
    def _():
        y = DN_ALPHA * x_ref[...] + 0.5 * acc_ref[...]
        o_ref[...] = _layer_norm(y, g_ref[...], b_ref[...])


def _ffn_ln(x, wg, wu, wd, g, b, tm, tf):
    M, D = x.shape
    F = wg.shape[1]
    return pl.pallas_call(
        _ffn_ln_kernel,
        grid=(M // tm, F // tf),
        in_specs=[
            pl.BlockSpec((tm, D), lambda i, j: (i, 0)),
            pl.BlockSpec((D, tf), lambda i, j: (0, j)),
            pl.BlockSpec((D, tf), lambda i, j: (0, j)),
            pl.BlockSpec((tf, D), lambda i, j: (j, 0)),
            pl.BlockSpec((1, D), lambda i, j: (0, 0)),
            pl.BlockSpec((1, D), lambda i, j: (0, 0)),
        ],
        out_specs=pl.BlockSpec((tm, D), lambda i, j: (i, 0)),
        out_shape=jax.ShapeDtypeStruct((M, D), F32),
        scratch_shapes=[pltpu.VMEM((tm, D), F32), pltpu.VMEM((tm, D), BF16)],
        compiler_params=_cparams("parallel", "arbitrary"),
        name="ffn_ln",
    )(x, wg, wu, wd, g, b)


def _proj_kernel(x_ref, w_ref, o_ref, xb_ref):
    @pl.when(pl.program_id(1) == 0)
    def _():
        xb_ref[...] = x_ref[...].astype(BF16)

    o_ref[...] = _dot(xb_ref[...], w_ref[...])


def _proj(x, w, tm, tn):
    M, K = x.shape
    N = w.shape[1]
    return pl.pallas_call(
        _proj_kernel,
        grid=(M // tm, N // tn),
        in_specs=[
            pl.BlockSpec((tm, K), lambda i, j: (i, 0)),
            pl.BlockSpec((K, tn), lambda i, j: (0, j)),
        ],
        out_specs=pl.BlockSpec((tm, tn), lambda i, j: (i, j)),
        out_shape=jax.ShapeDtypeStruct((M, N), F32),
        scratch_shapes=[pltpu.VMEM((tm, K), BF16)],
        compiler_params=_cparams("parallel", "arbitrary"),
        name="in_proj",
    )(x, w)


def _rope_tables(pos, rot_dim, period, live_lanes=LANES):
    half = rot_dim // 2
    inv = ROPE_THETA ** (-jnp.arange(half, dtype=F32) / half)
    ang = pos.astype(F32)[:, None] * inv[None, :]
    cos, sin = jnp.cos(ang), jnp.sin(ang)
    lane = np.arange(LANES)
    lp = lane % period
    idx = lp % half
    live = (lp < rot_dim) & (lane < live_lanes)
    c = jnp.where(live[None, :], cos[:, idx], 1.0)
    s = jnp.where(live[None, :], jnp.where((lp < half)[None, :], -sin[:, idx], sin[:, idx]), 0.0)
    return c.astype(F32), s.astype(F32)


def _rope_tile(x, c, s, half, period):
    lane = lax.broadcasted_iota(jnp.int32, x.shape, 1)
    first = (lane & (period - 1)) < half
    partner = jnp.where(first, pltpu.roll(x, LANES - half, 1), pltpu.roll(x, half, 1))
    return x * c + partner * s


def _rope_kernel(q_ref, k_ref, v_ref, qi_ref, sm_ref, ca_ref, sa_ref, ci_ref, si_ref, cs_ref, ss_ref,
                 qo_ref, ko_ref, kb_ref, vb_ref, qio_ref, smo_ref, smb_ref):
    ca, sa = ca_ref[...], sa_ref[...]
    ci, si = ci_ref[...], si_ref[...]
    for h in range(N_HEADS):
        sl = slice(h * LANES, (h + 1) * LANES)
        qo_ref[:, sl] = (_rope_tile(q_ref[:, sl], ca, sa, ROPE_DIM // 2, HEAD_DIM) * (HEAD_DIM ** -0.5)).astype(BF16)
    for h in range(N_KV_HEADS):
        sl = slice(h * LANES, (h + 1) * LANES)
        kr = _rope_tile(k_ref[:, sl], ca, sa, ROPE_DIM // 2, HEAD_DIM)
        ko_ref[:, sl] = kr
        kb_ref[:, sl] = kr.astype(BF16)
    vb_ref[...] = v_ref[...].astype(BF16)
    for h in range(IDX_HEADS * IDX_DIM // LANES):
        sl = slice(h * LANES, (h + 1) * LANES)
        qio_ref[:, sl] = _rope_tile(qi_ref[:, sl], ci, si, IDX_ROPE_DIM // 2, IDX_DIM).astype(BF16)
    sm = _rope_tile(sm_ref[...], cs_ref[...], ss_ref[...], IDX_ROPE_DIM // 2, IDX_DIM)
    smo_ref[...] = sm
    smb_ref[...] = sm.astype(BF16)


def _rope_prep(u_attn, u_small, tabs, tm):
    M = u_attn.shape[0]
    tpos = tabs[0].shape[0]
    nt = tpos // tm

    def col(name):
        w = SIZES[name]
        return pl.BlockSpec((tm, w), lambda i, o=ATTN_OFF[name] // w: (i, o))

    tab_spec = pl.BlockSpec((tm, LANES), lambda i: (i % nt, 0))
    row = lambda w: pl.BlockSpec((tm, w), lambda i: (i, 0))
    return pl.pallas_call(
        _rope_kernel,
        grid=(M // tm,),
        in_specs=[col('q_a'), col('k_a'), col('v_a'), col('q_idx'), row(LANES)] + [tab_spec] * 6,
        out_specs=[row(SIZES['q_a']), row(SIZES['k_a']), row(SIZES['k_a']), row(SIZES['v_a']),
                   row(SIZES['q_idx']), row(LANES), row(LANES)],
        out_shape=[
            jax.ShapeDtypeStruct((M, SIZES['q_a']), BF16),
            jax.ShapeDtypeStruct((M, SIZES['k_a']), F32),
            jax.ShapeDtypeStruct((M, SIZES['k_a']), BF16),
            jax.ShapeDtypeStruct((M, SIZES['v_a']), BF16),
            jax.ShapeDtypeStruct((M, SIZES['q_idx']), BF16),
            jax.ShapeDtypeStruct((M, LANES), F32),
            jax.ShapeDtypeStruct((M, LANES), BF16),
        ],
        compiler_params=_cparams("parallel"),
        name="rope_prep",
    )(u_attn, u_attn, u_attn, u_attn, u_small, *tabs)


BISECT_UNROLL = 4


def _count(pred):
    return jnp.sum(jnp.where(pred, 1.0, 0.0), axis=-1, keepdims=True)


def _tie_index(score, kidx, thr):
    return jnp.where(score == thr, kidx, jnp.int32(2 ** 31 - 1))


def _topk_threshold(score, kidx, n_allowed, k, idx_bits):
    rows = score.shape[0]
    take_all = n_allowed <= k
    lo0 = jnp.min(jnp.where(score == -jnp.inf, jnp.inf, score), axis=-1, keepdims=True)
    hi0 = jnp.max(score, axis=-1, keepdims=True)
    lo0 = jnp.where(take_all, 0.0, lo0)
    hi0 = jnp.where(take_all, 0.0, hi0)

    def step(lo, hi):
        mid = 0.5 * lo + 0.5 * hi
        ge = _count(score >= mid) >= k
        return jnp.where(ge, mid, lo), jnp.where(ge, hi, mid)

    def body(state):
        lo, hi, _ = state
        for _ in range(BISECT_UNROLL):
            lo, hi = step(lo, hi)
        mid = 0.5 * lo + 0.5 * hi
        still_open = jnp.max(jnp.where((mid > lo) & (mid < hi), 1.0, 0.0))
        return lo, hi, still_open

    lo, hi, _ = lax.while_loop(lambda state: state[2] > 0.5, body, (lo0, hi0, jnp.float32(1.0)))
    thr = jnp.where(_count(score >= hi) >= k, hi, lo)
    tie = _tie_index(score, kidx, thr)

    def tie_search():
        need = k - _count(score > thr)

        def ibody(t, j):
            cand = j + jnp.left_shift(jnp.int32(1), idx_bits - 1 - t)
            return jnp.where(_count(tie < cand) < need, cand, j)

        return lax.fori_loop(0, idx_bits, ibody, jnp.zeros((rows, 1), jnp.int32))

    repeated = jnp.max(jnp.where(take_all, 0.0, _count(score == thr))) > 1.5
    jmax = lax.cond(repeated, tie_search, lambda: jnp.full((rows, 1), 2 ** 31 - 2, jnp.int32))
    return thr, jmax, take_all


def _selected(score, kidx, thr, jmax, take_all):
    return take_all | (score > thr) | (_tie_index(score, kidx, thr) <= jmax)


def _dsa_prompt_kernel(q_ref, qi_ref, sm_ref, k_ref, v_ref, kis_ref, o_ref, *, ktop, key_step):
    tq = q_ref.shape[0]
    S = k_ref.shape[0]
    i = pl.program_id(1)
    w0 = SMALL_OFF['w_idx']

    def attend(L):
        ki = kis_ref[:L, :IDX_DIM]
        w = sm_ref[:, w0:w0 + IDX_HEADS] * (IDX_HEADS ** -0.5 * IDX_DIM ** -0.5)
        score = jnp.zeros((tq, L), F32)
        for h in range(IDX_HEADS):
            s = _dot_nt(qi_ref[:, h * IDX_DIM:(h + 1) * IDX_DIM], ki)
            score = score + w[:, h:h + 1] * jnp.maximum(s, 0.0)
        qpos = i * tq + lax.broadcasted_iota(jnp.int32, (tq, 1), 0)
        kidx = lax.broadcasted_iota(jnp.int32, (tq, L), 1)
        allowed = kidx <= qpos
        score = jnp.where(allowed, score, -jnp.inf)
        thr, jmax, take_all = _topk_threshold(score, kidx, qpos + 1, ktop, int(L - 1).bit_length())
        bias = jnp.where(_selected(score, kidx, thr, jmax, take_all) & allowed, 0.0, -jnp.inf)
        for h in range(N_HEADS):
            n = h // GROUP
            sl = slice(h * HEAD_DIM, (h + 1) * HEAD_DIM)
            kv = slice(n * HEAD_DIM, (n + 1) * HEAD_DIM)
            s = _dot_nt(q_ref[:, sl], k_ref[:L, kv]) + bias
            m = jnp.max(s, axis=-1, keepdims=True)
            p = jnp.exp(s - m)
            l = jnp.sum(p, axis=-1, keepdims=True)
            o_ref[:, sl] = _dot(p.astype(BF16), v_ref[:L, kv]) / l

    level = ((i + 1) * tq - 1) // key_step
    for lv in range(S // key_step):
        pl.when(level == lv)(functools.partial(attend, (lv + 1) * key_step))


def _dsa_prompt(q_bf, qi_bf, small_rot, k_bf, v_bf, small_bf, B, S, tq):
    ktop = min(TOPK_MAX, S // 4)
    nq = S // tq
    key_step = min(S, 512)
    row = lambda w: pl.BlockSpec((tq, w), lambda b, i: (b * nq + i, 0))
    full = lambda w: pl.BlockSpec((S, w), lambda b, i: (b, 0))
    return pl.pallas_call(
        functools.partial(_dsa_prompt_kernel, ktop=ktop, key_step=key_step),
        grid=(B, nq),
        in_specs=[row(q_bf.shape[1]), row(qi_bf.shape[1]), row(LANES),
                  full(k_bf.shape[1]), full(v_bf.shape[1]), full(LANES)],
        out_specs=row(q_bf.shape[1]),
        out_shape=jax.ShapeDtypeStruct((B * S, q_bf.shape[1]), F32),
        compiler_params=_cparams("parallel", "arbitrary"),
        name="dsa_prompt",
    )(q_bf, qi_bf, small_rot, k_bf, v_bf, small_bf)


def _idx_score_rows(qi, w, kpage):
    s = _dot_nt(qi, kpage)
    return jnp.sum(w * jnp.maximum(s, 0.0), axis=0, keepdims=True)


def _sample_scores_kernel(pt_ref, qi_ref, w_ref, knew_ref, *refs):
    page_refs, (o_ref, onew_ref) = refs[:-2], refs[-2:]
    npp = len(page_refs)
    p = pl.program_id(1)
    qi = qi_ref[0]
    w = w_ref[0] * (IDX_HEADS ** -0.5 * IDX_DIM ** -0.5)
    for j, page_ref in enumerate(page_refs):
        o_ref[0, pl.ds(p * npp + j, 1), :] = _idx_score_rows(qi, w, page_ref[0, 0].astype(BF16))

    @pl.when(p == 0)
    def _():
        kn = jnp.broadcast_to(knew_ref[0], (PAGE_SIZE, IDX_DIM))
        sc = _idx_score_rows(qi, w, kn)
        lane = lax.broadcasted_iota(jnp.int32, (1, PAGE_SIZE), 1)
        onew_ref[0] = jnp.where(lane == 0, sc, -jnp.inf)


def _page_specs(row_shape, npp, layer):
    zeros = (0,) * (1 + len(row_shape))
    return [pl.BlockSpec((1, 1, PAGE_SIZE) + row_shape,
                         lambda b, p, pt, j=j: (layer, pt[b, p * npp + j]) + zeros)
            for j in range(npp)]


def _pages_per_step(n_pages, cap):
    npp = min(cap, n_pages)
    while n_pages % npp:
        npp -= 1
    return npp


def _sample_scores(page_table, qi3, w3, knew3, cache_ki, layer):
    DB, n_pages = page_table.shape
    npp = _pages_per_step(n_pages, 16)
    return pl.pallas_call(
        _sample_scores_kernel,
        grid_spec=pltpu.PrefetchScalarGridSpec(
            num_scalar_prefetch=1,
            grid=(DB, n_pages // npp),
            in_specs=[
                pl.BlockSpec((1, IDX_HEADS, IDX_DIM), lambda b, p, pt: (b, 0, 0)),
                pl.BlockSpec((1, IDX_HEADS, 1), lambda b, p, pt: (b, 0, 0)),
                pl.BlockSpec((1, 1, IDX_DIM), lambda b, p, pt: (b, 0, 0)),
            ] + _page_specs((IDX_DIM,), npp, layer),
            out_specs=[
                pl.BlockSpec((1, n_pages, PAGE_SIZE), lambda b, p, pt: (b, 0, 0)),
                pl.BlockSpec((1, 1, PAGE_SIZE), lambda b, p, pt: (b, 0, 0)),
            ],
        ),
        out_shape=[jax.ShapeDtypeStruct((DB, n_pages, PAGE_SIZE), F32),
                   jax.ShapeDtypeStruct((DB, 1, PAGE_SIZE), F32)],
        compiler_params=_cparams("parallel", "arbitrary"),
        name="sample_scores",
    )(page_table, qi3, w3, knew3, *([cache_ki] * npp))


def _sample_thr_kernel(s_ref, thr_ref, jmax_ref, all_ref, *, ktop, n_valid, idx_bits):
    kidx = lax.broadcasted_iota(jnp.int32, s_ref.shape, 1)
    score = jnp.where(kidx < n_valid, s_ref[...], -jnp.inf)
    n_allowed = jnp.full((s_ref.shape[0], 1), n_valid, jnp.int32)
    thr, jmax, take_all = _topk_threshold(score, kidx, n_allowed, ktop, idx_bits)
    thr_ref[...] = jnp.broadcast_to(thr, thr_ref.shape)
    jmax_ref[...] = jnp.broadcast_to(jmax, jmax_ref.shape)
    all_ref[...] = jnp.broadcast_to(take_all.astype(jnp.int32), all_ref.shape)


def _sample_threshold(scores, ktop, n_valid):
    DB, L = scores.shape
    out = lambda dt: jax.ShapeDtypeStruct((DB, LANES), dt)
    return pl.pallas_call(
        functools.partial(_sample_thr_kernel, ktop=ktop, n_valid=n_valid, idx_bits=int(L - 1).bit_length()),
        out_shape=[out(F32), out(jnp.int32), out(jnp.int32)],
        compiler_params=pltpu.CompilerParams(vmem_limit_bytes=VMEM_LIMIT),
        name="sample_threshold",
    )(scores)


def _sample_attn_kernel(pt_ref, q_ref, s_ref, thr_ref, jmax_ref, all_ref, knew_ref, vnew_ref, *refs, n_pages):
    npp = (len(refs) - 4) // 2
    kp_refs, vp_refs = refs[:npp], refs[npp:2 * npp]
    o_ref, m_ref, l_ref, acc_ref = refs[2 * npp:]
    b = pl.program_id(0)
    p = pl.program_id(1)

    @pl.when(p == 0)
    def _():
        m_ref[...] = jnp.full_like(m_ref, NEG_BIG)
        l_ref[...] = jnp.zeros_like(l_ref)
        acc_ref[...] = jnp.zeros_like(acc_ref)

    thr = thr_ref[pl.ds(b, 1), 0:1]
    jmax = jmax_ref[pl.ds(b, 1), 0:1]
    take_all = all_ref[pl.ds(b, 1), 0:1] > 0

    q = q_ref[0]
    group = lax.broadcasted_iota(jnp.int32, (N_HEADS, 1), 0) // GROUP
    X = PAGE_SIZE * N_KV_HEADS
    own_head = (lax.broadcasted_iota(jnp.int32, (N_HEADS, X), 1) % N_KV_HEADS) == group
    repeat = (lax.broadcasted_iota(jnp.int32, (PAGE_SIZE, X), 1) // N_KV_HEADS
              == lax.broadcasted_iota(jnp.int32, (PAGE_SIZE, X), 0)).astype(BF16)

    def fold(logits, pv_fns):
        m_old = m_ref[...]
        m_new = functools.reduce(jnp.maximum, [jnp.max(s, axis=-1, keepdims=True) for s in logits], m_old)
        corr = jnp.exp(m_old - m_new)
        probs = [jnp.exp(s - m_new) for s in logits]
        l_ref[...] = l_ref[...] * corr + functools.reduce(
            lambda a, c: a + c, [jnp.sum(pr, axis=-1, keepdims=True) for pr in probs])
        acc_ref[...] = acc_ref[...] * corr + functools.reduce(
            lambda a, c: a + c, [fn(pr) for fn, pr in zip(pv_fns, probs)])
        m_ref[...] = m_new

    logits, pv_fns = [], []
    for j in range(npp):
        page = p * npp + j
        scores_row = s_ref[0, pl.ds(page, 1), :]
        kidx = page * PAGE_SIZE + lax.broadcasted_iota(jnp.int32, (1, PAGE_SIZE), 1)
        sel = _selected(scores_row, kidx, thr, jmax, take_all)
        sel_rows = _dot(jnp.broadcast_to(jnp.where(sel, 1.0, 0.0), (N_HEADS, PAGE_SIZE)).astype(BF16), repeat)
        s = _dot_nt(q, kp_refs[j][0].astype(BF16))
        logits.append(jnp.where(own_head & (sel_rows > 0.5), s, -jnp.inf))
        pv_fns.append(lambda pr, ref=vp_refs[j]: _dot(pr.astype(BF16), ref[0].astype(BF16)))
    fold(logits, pv_fns)

    @pl.when(p == pl.num_programs(1) - 1)
    def _():
        def per_head(ref):
            rows = [jnp.where(group == n, jnp.broadcast_to(ref[0, :, n * HEAD_DIM:(n + 1) * HEAD_DIM],
                                                           (N_HEADS, HEAD_DIM)), 0.0) for n in range(N_KV_HEADS)]
            return functools.reduce(lambda a, c: a + c, rows).astype(BF16).astype(F32)

        score_new = s_ref[0, pl.ds(n_pages, 1), 0:1]
        sel = _selected(score_new, jnp.full((1, 1), n_pages * PAGE_SIZE, jnp.int32), thr, jmax, take_all)
        s = jnp.sum(q.astype(F32) * per_head(knew_ref), axis=-1, keepdims=True)
        s = jnp.where(sel, s, -jnp.inf)
        vexp = per_head(vnew_ref)
        fold([s], [lambda pr: pr.astype(BF16).astype(F32) * vexp])
        o_ref[0] = acc_ref[...] / l_ref[...]


def _sample_attn(page_table, q3, scores3, thr, jmax, take_all, knew3, vnew3, cache_k, cache_v, layer):
    DB, n_pages = page_table.shape
    width = N_KV_HEADS * HEAD_DIM
    npp = _pages_per_step(n_pages, 16)
    bsel = lambda *shape: pl.BlockSpec((1,) + shape, lambda b, p, pt: (b,) + (0,) * len(shape))
    whole = pl.BlockSpec((DB, LANES), lambda b, p, pt: (0, 0))
    n_phys = cache_k.shape[1]
    rows = PAGE_SIZE * N_KV_HEADS
    cache_k = cache_k.reshape(-1, rows, HEAD_DIM)
    cache_v = cache_v.reshape(-1, rows, HEAD_DIM)
    pages = [pl.BlockSpec((1, rows, HEAD_DIM),
                          lambda b, p, pt, j=j: (layer * n_phys + pt[b, p * npp + j], 0, 0)) for j in range(npp)]
    return pl.pallas_call(
        functools.partial(_sample_attn_kernel, n_pages=n_pages),
        grid_spec=pltpu.PrefetchScalarGridSpec(
            num_scalar_prefetch=1,
            grid=(DB, n_pages // npp),
            in_specs=[bsel(N_HEADS, HEAD_DIM), bsel(n_pages + 1, PAGE_SIZE), whole, whole, whole,
                      bsel(1, width), bsel(1, width)] + pages + pages,
            out_specs=bsel(N_HEADS, HEAD_DIM),
            scratch_shapes=[pltpu.VMEM((N_HEADS, 1), F32), pltpu.VMEM((N_HEADS, 1), F32),
                            pltpu.VMEM((N_HEADS, HEAD_DIM), F32)],
        ),
        out_shape=jax.ShapeDtypeStruct((DB, N_HEADS, HEAD_DIM), F32),
        compiler_params=_cparams("parallel", "arbitrary"),
        name="sample_attn",
    )(page_table, q3, scores3, thr, jmax, take_all, knew3, vnew3, *([cache_k] * npp), *([cache_v] * npp))


CARRY = 8
MXU_DIM = 256
GDN_GROUP = MXU_DIM // CHUNK
GDN_SPLIT_STAGES = 1


def _spread(a, row_head, hg):
    return jnp.concatenate([jnp.where(row_head == i, a, 0.0) for i in range(hg)], axis=1)


def _gdn_prep_kernel(x_ref, w_ref, buf_ref, q_ref, k_ref, v_ref, conv_ref, xpad_ref):
    tt = x_ref.shape[1]
    t = pl.program_id(1)
    lo = CARRY - (CONV_W - 1)

    @pl.when(t == 0)
    def _():
        xpad_ref[lo:CARRY, :] = buf_ref[0]

    xpad_ref[CARRY:CARRY + tt, :] = x_ref[0]
    nh = GDN_HEADS
    for c in range(CONV_DIM // LANES):
        sl = slice(c * LANES, (c + 1) * LANES)
        y = w_ref[0:1, sl] * xpad_ref[lo:lo + tt, sl]
        for j in range(1, CONV_W):
            y = y + w_ref[j:j + 1, sl] * xpad_ref[lo + j:lo + j + tt, sl]
        y = _silu(y)
        if c < 2 * nh:
            y = y * lax.rsqrt(jnp.sum(y * y, axis=-1, keepdims=True) + NORM_EPS)
        if c < nh:
            q_ref[0, :, sl] = y * (GDN_DK ** -0.5)
        elif c < 2 * nh:
            k_ref[0, :, slice((c - nh) * LANES, (c - nh + 1) * LANES)] = y
        else:
            v_ref[0, :, slice((c - 2 * nh) * LANES, (c - 2 * nh + 1) * LANES)] = y
    last = xpad_ref[lo + tt:CARRY + tt, :]
    xpad_ref[lo:CARRY, :] = last

    @pl.when(t == pl.num_programs(1) - 1)
    def _():
        conv_ref[0] = last


def _gdn_prep(u_gdn3, conv_w, buf, tt):
    B, T, _ = u_gdn3.shape
    w = SIZES['qkv_b']
    hd = GDN_HEADS * GDN_DK
    out = jax.ShapeDtypeStruct((B, T, hd), F32)
    ospec = pl.BlockSpec((1, tt, hd), lambda b, t: (b, t, 0))
    return pl.pallas_call(
        _gdn_prep_kernel,
        grid=(B, T // tt),
        in_specs=[
            pl.BlockSpec((1, tt, w), lambda b, t: (b, t, 0)),
            pl.BlockSpec((CONV_W, w), lambda b, t: (0, 0)),
            pl.BlockSpec((1, CONV_W - 1, w), lambda b, t: (b, 0, 0)),
        ],
        out_specs=[ospec, ospec, ospec, pl.BlockSpec((1, CONV_W - 1, w), lambda b, t: (b, 0, 0))],
        out_shape=[out, out, out, jax.ShapeDtypeStruct((B, CONV_W - 1, w), F32)],
        scratch_shapes=[pltpu.VMEM((CARRY + tt, w), F32)],
        compiler_params=_cparams("parallel", "arbitrary"),
        name="gdn_prep",
    )(u_gdn3, conv_w, buf)


def _gdn_chunk_kernel(q_ref, k_ref, v_ref, z_ref, sm_ref, alog_ref, dtb_ref, gn_ref, s0_ref,
                      o_ref, sout_ref, state_ref, *, t_valid):
    C = q_ref.shape[1]
    c = pl.program_id(1)

    @pl.when(c == 0)
    def _():
        state_ref[...] = s0_ref[0]

    H = GDN_HEADS
    a0, b0 = SMALL_OFF['a_b'], SMALL_OFF['beta_b']
    live = (c * C + lax.broadcasted_iota(jnp.int32, (C, H), 0)) < t_valid
    xs = sm_ref[0, :, a0:a0 + H] + dtb_ref[...]
    softplus = jnp.maximum(xs, 0.0) + jnp.log1p(jnp.exp(-jnp.abs(xs)))
    g_all = jnp.where(live, -jnp.exp(alog_ref[...]) * softplus, 0.0)
    beta_all = jnp.where(live, jax.nn.sigmoid(sm_ref[0, :, b0:b0 + H]), 0.0)
    tri_f = (lax.broadcasted_iota(jnp.int32, (C, C), 0) >= lax.broadcasted_iota(jnp.int32, (C, C), 1)).astype(F32)
    eye_h = (lax.broadcasted_iota(jnp.int32, (H, H), 0) == lax.broadcasted_iota(jnp.int32, (H, H), 1)).astype(F32)
    gcum = _dot_hi(tri_f, g_all)
    gcum_t = _dot_nt_hi(eye_h, gcum)

    HG = GDN_GROUP
    R = HG * C
    ri = lax.broadcasted_iota(jnp.int32, (R, R), 0)
    ci = lax.broadcasted_iota(jnp.int32, (R, R), 1)
    same_head = (ri // C) == (ci // C)
    mask_incl = same_head & (ri >= ci)
    mask_strict = same_head & (ri > ci)
    row_head = lax.broadcasted_iota(jnp.int32, (R, 1), 0) // C
    row_head2 = jnp.concatenate([row_head, row_head], axis=0)

    for grp in range(H // HG):
        heads = range(grp * HG, (grp + 1) * HG)
        sls = [slice(h * GDN_DK, (h + 1) * GDN_DK) for h in heads]
        rows = lambda ref: jnp.concatenate([ref[0, :, sl] for sl in sls], axis=0)
        cols = lambda a: jnp.concatenate([a[:, h:h + 1] for h in heads], axis=0)
        q, k, v = rows(q_ref), rows(k_ref), rows(v_ref)
        beta = cols(beta_all)
        gcol = cols(gcum)
        grow = jnp.concatenate([gcum_t[h:h + 1, :] for h in heads], axis=1)
        glast = jnp.concatenate([jnp.broadcast_to(gcum[C - 1:C, h:h + 1], (C, 1)) for h in heads], axis=0)
        decay = jnp.where(mask_incl, jnp.exp(jnp.where(mask_incl, gcol - grow, 0.0)), 0.0)
        kb = k * beta
        kk_qk = _dot_nt(jnp.concatenate([kb, q], axis=0).astype(BF16), k.astype(BF16))
        nmat = jnp.where(mask_strict, kk_qk[:R] * decay, 0.0)
        qk = kk_qk[R:] * decay
        x = jnp.concatenate([v * beta, kb * jnp.exp(gcol)], axis=1)
        pw = nmat
        x = x - _dot_split(pw, x)
        for stage in range(max(C - 1, 1).bit_length() - 1):
            mm = _dot_split if stage < GDN_SPLIT_STAGES else _dot_bf16
            pw = mm(pw, pw)
            x = x + mm(pw, x)
        u, w = x[:, :GDN_DV], x[:, GDN_DV:]
        s_stack = state_ref[grp * HG:(grp + 1) * HG].reshape(HG * GDN_DK, GDN_DV)
        w_q = jnp.concatenate([w, q * jnp.exp(gcol)], axis=0)
        ws_qs = _dot(_spread(w_q, row_head2, HG).astype(BF16), s_stack.astype(BF16))
        v_new = u - ws_qs[:R]
        o = ws_qs[R:] + _dot(qk.astype(BF16), v_new.astype(BF16))
        kdec_t = (k * jnp.exp(glast - gcol)).T
        s_add = _dot(kdec_t.astype(BF16), _spread(v_new, row_head, HG).astype(BF16))
        on = o * lax.rsqrt(jnp.mean(o * o, axis=-1, keepdims=True) + NORM_EPS) * gn_ref[...]
        for a, h in enumerate(heads):
            state_ref[h] = (state_ref[h] * jnp.exp(gcum[C - 1:C, h:h + 1])
                            + s_add[:, a * GDN_DV:(a + 1) * GDN_DV])
            o_ref[0, :, sls[a]] = on[a * C:(a + 1) * C] * _silu(z_ref[0, :, sls[a]])

    @pl.when(c == pl.num_programs(1) - 1)
    def _():
        sout_ref[0] = state_ref[...]


def _gdn_chunks(qn, kn, vv, u_gate3, small3, a_log, dt_bias, gn, s0, t_valid):
    B, Tp, hd = qn.shape
    nc = Tp // CHUNK
    blk = pl.BlockSpec((1, CHUNK, hd), lambda b, c: (b, c, 0))
    vec = lambda w: pl.BlockSpec((1, w), lambda b, c: (0, 0))
    st = pl.BlockSpec((1, GDN_HEADS, GDN_DK, GDN_DV), lambda b, c: (b, 0, 0, 0))
    return pl.pallas_call(
        functools.partial(_gdn_chunk_kernel, t_valid=t_valid),
        grid=(B, nc),
        in_specs=[blk, blk, blk,
                  pl.BlockSpec((1, CHUNK, hd), lambda b, c, o=GATE_OFF['z_b'] // hd: (b, c, o)),
                  pl.BlockSpec((1, CHUNK, LANES), lambda b, c: (b, c, 0)),
                  vec(GDN_HEADS), vec(GDN_HEADS), vec(GDN_DV), st],
        out_specs=[blk, st],
        out_shape=[jax.ShapeDtypeStruct((B, Tp, hd), F32),
                   jax.ShapeDtypeStruct((B, GDN_HEADS, GDN_DK, GDN_DV), F32)],
        scratch_shapes=[pltpu.VMEM((GDN_HEADS, GDN_DK, GDN_DV), F32)],
        compiler_params=_cparams("parallel", "arbitrary"),
        name="gdn_chunks",
    )(qn, kn, vv, u_gate3, small3, a_log, dt_bias, gn, s0)


def _merge_kernel(x_ref, oa_ref, ob_ref, ga_ref, gb_ref, wo_ref, g_ref, b_ref, o_ref):
    merged = jax.nn.sigmoid(ga_ref[...]) * oa_ref[...] + jax.nn.sigmoid(gb_ref[...]) * ob_ref[...]
    y = DN_ALPHA * x_ref[...] + _dot(merged.astype(BF16), wo_ref[...])
    o_ref[...] = _layer_norm(y, g_ref[...], b_ref[...])


def _merge_proj_ln(x, o_a, o_b, u_gate, w_o, g, b, tm):
    M, D = x.shape
    row = pl.BlockSpec((tm, D), lambda i: (i, 0))
    col = lambda name: pl.BlockSpec((tm, D), lambda i, o=GATE_OFF[name] // D: (i, o))
    vec = pl.BlockSpec((1, D), lambda i: (0, 0))
    return pl.pallas_call(
        _merge_kernel,
        grid=(M // tm,),
        in_specs=[row, row, row, col('gate_a'), col('gate_b'),
                  pl.BlockSpec((D, D), lambda i: (0, 0)), vec, vec],
        out_specs=row,
        out_shape=jax.ShapeDtypeStruct((M, D), F32),
        compiler_params=_cparams("parallel"),
        name="merge_proj_ln",
    )(x, o_a, o_b, u_gate, u_gate, w_o, g, b)


def _tiles(M):
    return (512, 256) if M % 512 == 0 else (M, M)


def _layer(x, B, T, pos, wts, conv_buf, ssm0, dsa_fn):
    M = B * T
    tm, te = _tiles(M)
    row = lambda a: a.reshape(1, -1)
    x1 = _ffn_ln(x, wts['ffn1_g'], wts['ffn1_u'], wts['ffn1_d'], row(wts['ln1_g']), row(wts['ln1_b']),
                 tm, 512)
    tp = 1024 if M % 1024 == 0 else tm
    u_attn = _proj(x1, wts['w_attn'], tp, 1024)
    u_gdn = _proj(x1, wts['w_gdn'], tp, 1024)
    u_gate = _proj(x1, wts['w_gate'], tp, 1024)
    u_small = _proj(x1, wts['w_small'], tp, LANES)

    pos_rows = pos if T > 1 else jnp.broadcast_to(pos, (M,))
    tabs = (_rope_tables(pos_rows, ROPE_DIM, HEAD_DIM) + _rope_tables(pos_rows, IDX_ROPE_DIM, IDX_DIM)
            + _rope_tables(pos_rows, IDX_ROPE_DIM, IDX_DIM, live_lanes=IDX_DIM))
    q_bf, k_rot, k_bf, v_bf, qi_bf, small_rot, small_bf = _rope_prep(u_attn, u_small, tabs, te)
    v_rows = u_attn[:, ATTN_OFF['v_a']:ATTN_OFF['v_a'] + SIZES['v_a']]
    o_a = dsa_fn(v_rows, q_bf, k_rot, k_bf, v_bf, qi_bf, small_rot, small_bf)

    qn, kn, vv, conv_new = _gdn_prep(u_gdn.reshape(B, T, -1), wts['conv_w'], conv_buf, min(T, 256))
    pad = (-T) % CHUNK
    pad3 = lambda a: jnp.pad(a, ((0, 0), (0, pad), (0, 0))) if pad else a
    o_b, ssm_new = _gdn_chunks(pad3(qn), pad3(kn), pad3(vv), pad3(u_gate.reshape(B, T, -1)),
                               pad3(u_small.reshape(B, T, LANES)),
                               row(wts['a_log']), row(wts['dt_bias']), row(wts['gdn_norm_g']), ssm0, T)
    o_b = o_b[:, :T].reshape(M, D_MODEL)

    x2 = _merge_proj_ln(x1, o_a, o_b, u_gate, wts['w_o'], row(wts['ln2_g']), row(wts['ln2_b']), te)
    y = _ffn_ln(x2, wts['ffn2_g'], wts['ffn2_u'], wts['ffn2_d'], row(wts['ln3_g']), row(wts['ln3_b']),
                tm, 512)
    ki_rows = small_rot[:, SMALL_OFF['k_idx']:SMALL_OFF['k_idx'] + IDX_DIM]
    return y, (k_rot, v_rows, ki_rows, ssm_new, conv_new)


def _split_w_in(w_in):
    offs = dict(zip([nm for nm, _ in IN_SPLITS], np.cumsum([0] + [n for _, n in IN_SPLITS])))

    def span(order):
        lo = offs[order[0]]
        hi = offs[order[-1]] + SIZES[order[-1]]
        assert hi - lo == sum(SIZES[nm] for nm in order)
        return w_in[:, lo:hi].astype(BF16)

    w_small = jnp.concatenate([w_in[:, offs[nm]:offs[nm] + SIZES[nm]] for nm in SMALL_ORDER], axis=1)
    w_small = jnp.pad(w_small, ((0, 0), (0, LANES - SMALL_USED))).astype(BF16)
    return span(ATTN_ORDER), span(GDN_ORDER), span(GATE_ORDER), w_small


def kernel(x_prompt, x_sample, cache_k, cache_v, cache_idx_k, state_ssm, state_conv, page_table, ffn1_w_gate, ffn1_w_up, ffn1_w_down, ln1_g, ln1_b, w_in, conv_w, a_log, dt_bias, gdn_norm_g, w_o, ln2_g, ln2_b, ffn2_w_gate, ffn2_w_up, ffn2_w_down, ln3_g, ln3_b):
    B, S, _ = x_prompt.shape
    DB, T, _ = x_sample.shape
    assert T == 1, "the sample path handles one new token per sequence"
    n_pages = page_table.shape[1]
    yp = x_prompt.reshape(B * S, D_MODEL)
    ys = x_sample.reshape(DB * T, D_MODEL)
    outs_p, outs_s = [], []
    for l in range(ffn1_w_gate.shape[0]):
        w_attn, w_gdn, w_gate, w_small = _split_w_in(w_in[l])
        wts = dict(
            ffn1_g=ffn1_w_gate[l].astype(BF16), ffn1_u=ffn1_w_up[l].astype(BF16), ffn1_d=ffn1_w_down[l].astype(BF16),
            ffn2_g=ffn2_w_gate[l].astype(BF16), ffn2_u=ffn2_w_up[l].astype(BF16), ffn2_d=ffn2_w_down[l].astype(BF16),
            ln1_g=ln1_g[l], ln1_b=ln1_b[l], ln2_g=ln2_g[l], ln2_b=ln2_b[l], ln3_g=ln3_g[l], ln3_b=ln3_b[l],
            w_attn=w_attn, w_gdn=w_gdn, w_gate=w_gate, w_small=w_small, w_o=w_o[l].astype(BF16), conv_w=conv_w[l],
            a_log=a_log[l], dt_bias=dt_bias[l], gdn_norm_g=gdn_norm_g[l],
        )

        def dsa_p(v_rows, q_bf, k_rot, k_bf, v_bf, qi_bf, small_rot, small_bf):
            return _dsa_prompt(q_bf, qi_bf, small_rot, k_bf, v_bf, small_bf, B, S, 128)

        def dsa_s(v_rows, q_bf, k_rot, k_bf, v_bf, qi_bf, small_rot, small_bf, l=l):
            w0 = SMALL_OFF['w_idx']
            w3 = small_rot[:, w0:w0 + IDX_HEADS].reshape(DB, IDX_HEADS, 1)
            qi3 = qi_bf.reshape(DB, IDX_HEADS, IDX_DIM)
            knew_i = small_bf[:, :IDX_DIM].reshape(DB, 1, IDX_DIM)
            width = N_KV_HEADS * HEAD_DIM
            past, new = _sample_scores(page_table, qi3, w3, knew_i, cache_idx_k, l)
            scores3 = jnp.concatenate([past, new], axis=1)
            n_keys = n_pages * PAGE_SIZE + T
            ktop = min(TOPK_MAX, n_keys // 4)
            thr, jmax, take_all = _sample_threshold(scores3.reshape(DB, -1), ktop, n_keys)
            o = _sample_attn(page_table, q_bf.reshape(DB, N_HEADS, HEAD_DIM), scores3, thr, jmax, take_all,
                             k_rot.reshape(DB, 1, width), v_rows.reshape(DB, 1, width), cache_k, cache_v, l)
            return o.reshape(DB, D_MODEL)

        conv0 = jnp.zeros((B, CONV_W - 1, CONV_DIM), F32)
        ssm_zero = jnp.zeros((B, GDN_HEADS, GDN_DK, GDN_DV), F32)
        yp, st_p = _layer(yp, B, S, jnp.arange(S, dtype=jnp.int32), wts, conv0, ssm_zero, dsa_p)
        past_len = n_pages * PAGE_SIZE
        ys, st_s = _layer(ys, DB, T, past_len + jnp.arange(T, dtype=jnp.int32), wts, state_conv[l], state_ssm[l], dsa_s)
        outs_p.append(st_p)
        outs_s.append(st_s)

    def stack(outs, nb, nt):
        d = len(outs)
        k, v, ki, ssm, conv = [a[0][None] if d == 1 else jnp.stack(a) for a in zip(*outs)]
        return (k.reshape(d, nb, nt, N_KV_HEADS, HEAD_DIM), v.reshape(d, nb, nt, N_KV_HEADS, HEAD_DIM),
                ki.reshape(d, nb, nt, IDX_DIM), ssm, conv)

    return (yp.reshape(B, S, D_MODEL), ys.reshape(DB, T, D_MODEL)) + stack(outs_p, B, S) + stack(outs_s, DB, T)
```

```python
import functools

import jax
import jax.numpy as jnp
import numpy as np
from jax import lax
from jax.experimental import pallas as pl
from jax.experimental.pallas import tpu as pltpu

D_MODEL = 2048
PAST_LEN = 16384
PAGE_SIZE = 128
HEAD_DIM = 128
N_HEADS = D_MODEL // HEAD_DIM
N_KV_HEADS = 4
GROUP = N_HEADS // N_KV_HEADS
ROPE_DIM = HEAD_DIM // 4
IDX_HEADS = 16
IDX_DIM = 64
IDX_ROPE_DIM = IDX_DIM // 4
TOPK_MAX = 256
ROPE_THETA = 500000.0
GDN_DK = 128
GDN_DV = 128
GDN_HEADS = D_MODEL // GDN_DV
CONV_W = 4
CONV_DIM = 2 * GDN_HEADS * GDN_DK + GDN_HEADS * GDN_DV
CHUNK = 64
D_FF = 5632
LN_EPS = 1e-5
NORM_EPS = 1e-6
DEPTH = 1
DN_ALPHA = (2 * DEPTH) ** 0.25

IN_SPLITS = (
    ('q_a', N_HEADS * HEAD_DIM), ('k_a', N_KV_HEADS * HEAD_DIM), ('v_a', N_KV_HEADS * HEAD_DIM),
    ('q_idx', IDX_HEADS * IDX_DIM), ('k_idx', IDX_DIM), ('w_idx', IDX_HEADS),
    ('qkv_b', CONV_DIM), ('a_b', GDN_HEADS), ('beta_b', GDN_HEADS), ('z_b', GDN_HEADS * GDN_DV),
    ('gate_a', D_MODEL), ('gate_b', D_MODEL),
)
ATTN_ORDER = ('q_a', 'k_a', 'v_a', 'q_idx')
GDN_ORDER = ('qkv_b',)
GATE_ORDER = ('z_b', 'gate_a', 'gate_b')
SMALL_ORDER = ('k_idx', 'w_idx', 'a_b', 'beta_b')
LANES = 128
VMEM_LIMIT = 56 * 1024 * 1024
NEG_BIG = -1e30
INT_MIN = -2 ** 31

F32 = jnp.float32
BF16 = jnp.bfloat16


def _offsets(order):
    sizes = dict(IN_SPLITS)
    offs, o = {}, 0
    for nm in order:
        offs[nm] = o
        o += sizes[nm]
    return offs, o


ATTN_OFF, ATTN_COLS = _offsets(ATTN_ORDER)
GATE_OFF, GATE_COLS = _offsets(GATE_ORDER)
SMALL_OFF, SMALL_USED = _offsets(SMALL_ORDER)
SIZES = dict(IN_SPLITS)


def _cparams(*sem):
    return pltpu.CompilerParams(dimension_semantics=sem, vmem_limit_bytes=VMEM_LIMIT)


def _dot(a, b):
    return jnp.dot(a, b, preferred_element_type=F32)


def _dot_nt(a, b):
    return lax.dot_general(a, b, (((1,), (1,)), ((), ())), preferred_element_type=F32)


def _dot_hi(a, b):
    return jnp.dot(a, b, preferred_element_type=F32, precision=lax.Precision.HIGHEST)


def _dot_nt_hi(a, b):
    return lax.dot_general(a, b, (((1,), (1,)), ((), ())), preferred_element_type=F32,
                           precision=lax.Precision.HIGHEST)


def _dot_tn_hi(a, b):
    return lax.dot_general(a, b, (((0,), (0,)), ((), ())), preferred_element_type=F32,
                           precision=lax.Precision.HIGHEST)


def _dot_bf16(a, b):
    return _dot(a.astype(BF16), b.astype(BF16))


def _split_bf16(a):
    hi = a.astype(BF16)
    return hi, (a - hi.astype(F32)).astype(BF16)


def _dot_split(a, b):
    ah, al = _split_bf16(a)
    bh, bl = _split_bf16(b)
    return _dot(ah, bh) + (_dot(al, bh) + _dot(ah, bl))


def _silu(x):
    return x * jax.nn.sigmoid(x)


def _layer_norm(y, g, b):
    mu = jnp.mean(y, axis=-1, keepdims=True)
    d = y - mu
    var = jnp.mean(d * d, axis=-1, keepdims=True)
    return d * lax.rsqrt(var + LN_EPS) * g + b


def _ffn_ln_kernel(x_ref, wg_ref, wu_ref, wd_ref, g_ref, b_ref, o_ref, acc_ref, xb_ref):
    j = pl.program_id(1)

    @pl.when(j == 0)
    def _():
        acc_ref[...] = jnp.zeros_like(acc_ref)
        xb_ref[...] = x_ref[...].astype(BF16)

    xb = xb_ref[...]
    hg = _dot(xb, wg_ref[...])
    hu = _dot(xb, wu_ref[...])
    h = _silu(hg) * hu
    acc_ref[...] += _dot(h.astype(BF16), wd_ref[...])

    @pl.when(j == pl.num_programs(1) - 1)
    def _():
        y = DN_ALPHA * x_ref[...] + 0.5 * acc_ref[...]
        o_ref[...] = _layer_norm(y, g_ref[...], b_ref[...])


def _ffn_ln(x, wg, wu, wd, g, b, tm, tf):
    M, D = x.shape
    F = wg.shape[1]
    return pl.pallas_call(
        _ffn_ln_kernel,
        grid=(M // tm, F // tf),
        in_specs=[
            pl.BlockSpec((tm, D), lambda i, j: (i, 0)),
            pl.BlockSpec((D, tf), lambda i, j: (0, j)),
            pl.BlockSpec((D, tf), lambda i, j: (0, j)),
            pl.BlockSpec((tf, D), lambda i, j: (j, 0)),
            pl.BlockSpec((1, D), lambda i, j: (0, 0)),
            pl.BlockSpec((1, D), lambda i, j: (0, 0)),
        ],
        out_specs=pl.BlockSpec((tm, D), lambda i, j: (i, 0)),
        out_shape=jax.ShapeDtypeStruct((M, D), F32),
        scratch_shapes=[pltpu.VMEM((tm, D), F32), pltpu.VMEM((tm, D), BF16)],
        compiler_params=_cparams("parallel", "arbitrary"),
        name="ffn_ln",
    )(x, wg, wu, wd, g, b)


def _proj_kernel(x_ref, w_ref, o_ref, xb_ref):
    @pl.when(pl.program_id(1) == 0)
    def _():
        xb_ref[...] = x_ref[...].astype(BF16)

    o_ref[...] = _dot(xb_ref[...], w_ref[...])


def _proj(x, w, tm, tn):
    M, K = x.shape
    N = w.shape[1]
    return pl.pallas_call(
        _proj_kernel,
        grid=(M // tm, N // tn),
        in_specs=[
            pl.BlockSpec((tm, K), lambda i, j: (i, 0)),
            pl.BlockSpec((K, tn), lambda i, j: (0, j)),
        ],
        out_specs=pl.BlockSpec((tm, tn), lambda i, j: (i, j)),
        out_shape=jax.ShapeDtypeStruct((M, N), F32),
        scratch_shapes=[pltpu.VMEM((tm, K), BF16)],
        compiler_params=_cparams("parallel", "arbitrary"),
        name="in_proj",
    )(x, w)


def _rope_tables(pos, rot_dim, period, live_lanes=LANES):
    half = rot_dim // 2
    inv = ROPE_THETA ** (-jnp.arange(half, dtype=F32) / half)
    ang = pos.astype(F32)[:, None] * inv[None, :]
    cos, sin = jnp.cos(ang), jnp.sin(ang)
    lane = np.arange(LANES)
    lp = lane % period
    idx = lp % half
    live = (lp < rot_dim) & (lane < live_lanes)
    c = jnp.where(live[None, :], cos[:, idx], 1.0)
    s = jnp.where(live[None, :], jnp.where((lp < half)[None, :], -sin[:, idx], sin[:, idx]), 0.0)
    return c.astype(F32), s.astype(F32)


def _rope_tile(x, c, s, half, period):
    lane = lax.broadcasted_iota(jnp.int32, x.shape, 1)
    first = (lane & (period - 1)) < half
    partner = jnp.where(first, pltpu.roll(x, LANES - half, 1), pltpu.roll(x, half, 1))
    return x * c + partner * s


def _rope_kernel(q_ref, k_ref, v_ref, qi_ref, sm_ref, ca_ref, sa_ref, ci_ref, si_ref, cs_ref, ss_ref,
                 qo_ref, ko_ref, kb_ref, vb_ref, qio_ref, smo_ref, smb_ref):
    ca, sa = ca_ref[...], sa_ref[...]
    ci, si = ci_ref[...], si_ref[...]
    for h in range(N_HEADS):
        sl = slice(h * LANES, (h + 1) * LANES)
        qo_ref[:, sl] = (_rope_tile(q_ref[:, sl], ca, sa, ROPE_DIM // 2, HEAD_DIM) * (HEAD_DIM ** -0.5)).astype(BF16)
    for h in range(N_KV_HEADS):
        sl = slice(h * LANES, (h + 1) * LANES)
        kr = _rope_tile(k_ref[:, sl], ca, sa, ROPE_DIM // 2, HEAD_DIM)
        ko_ref[:, sl] = kr
        kb_ref[:, sl] = kr.astype(BF16)
    vb_ref[...] = v_ref[...].astype(BF16)
    for h in range(IDX_HEADS * IDX_DIM // LANES):
        sl = slice(h * LANES, (h + 1) * LANES)
        qio_ref[:, sl] = _rope_tile(qi_ref[:, sl], ci, si, IDX_ROPE_DIM // 2, IDX_DIM).astype(BF16)
    sm = _rope_tile(sm_ref[...], cs_ref[...], ss_ref[...], IDX_ROPE_DIM // 2, IDX_DIM)
    smo_ref[...] = sm
    smb_ref[...] = sm.astype(BF16)


def _rope_prep(u_attn, u_small, tabs, tm):
    M = u_attn.shape[0]
    tpos = tabs[0].shape[0]
    nt = tpos // tm

    def col(name):
        w = SIZES[name]
        return pl.BlockSpec((tm, w), lambda i, o=ATTN_OFF[name] // w: (i, o))

    tab_spec = pl.BlockSpec((tm, LANES), lambda i: (i % nt, 0))
    row = lambda w: pl.BlockSpec((tm, w), lambda i: (i, 0))
    return pl.pallas_call(
        _rope_kernel,
        grid=(M // tm,),
        in_specs=[col('q_a'), col('k_a'), col('v_a'), col('q_idx'), row(LANES)] + [tab_spec] * 6,
        out_specs=[row(SIZES['q_a']), row(SIZES['k_a']), row(SIZES['k_a']), row(SIZES['v_a']),
                   row(SIZES['q_idx']), row(LANES), row(LANES)],
        out_shape=[
            jax.ShapeDtypeStruct((M, SIZES['q_a']), BF16),
            jax.ShapeDtypeStruct((M, SIZES['k_a']), F32),
            jax.ShapeDtypeStruct((M, SIZES['k_a']), BF16),
            jax.ShapeDtypeStruct((M, SIZES['v_a']), BF16),
            jax.ShapeDtypeStruct((M, SIZES['q_idx']), BF16),
            jax.ShapeDtypeStruct((M, LANES), F32),
            jax.ShapeDtypeStruct((M, LANES), BF16),
        ],
        compiler_params=_cparams("parallel"),
        name="rope_prep",
    )(u_attn, u_attn, u_attn, u_attn, u_small, *tabs)


BISECT_UNROLL = 4


def _count(pred):
    return jnp.sum(jnp.where(pred, 1.0, 0.0), axis=-1, keepdims=True)


def _tie_index(score, kidx, thr):
    return jnp.where(score == thr, kidx, jnp.int32(2 ** 31 - 1))


def _topk_threshold(score, kidx, n_allowed, k, idx_bits):
    rows = score.shape[0]
    take_all = n_allowed <= k
    lo0 = jnp.min(jnp.where(score == -jnp.inf, jnp.inf, score), axis=-1, keepdims=True)
    hi0 = jnp.max(score, axis=-1, keepdims=True)
    lo0 = jnp.where(take_all, 0.0, lo0)
    hi0 = jnp.where(take_all, 0.0, hi0)

    def step(lo, hi, n_lo):
        mid = 0.5 * lo + 0.5 * hi
        n_mid = _count(score >= mid)
        ge = n_mid >= k
        return jnp.where(ge, mid, lo), jnp.where(ge, hi, mid), jnp.where(ge, n_mid, n_lo)

    def body(state):
        lo, hi, n_lo, _ = state
        for _ in range(BISECT_UNROLL):
            lo, hi, n_lo = step(lo, hi, n_lo)
        mid = 0.5 * lo + 0.5 * hi
        still_open = jnp.max(jnp.where((mid > lo) & (mid < hi) & (n_lo > k), 1.0, 0.0))
        return lo, hi, n_lo, still_open

    n_lo0 = jnp.where(take_all, float(k), n_allowed.astype(F32))
    lo, hi, _, _ = lax.while_loop(lambda state: state[3] > 0.5, body, (lo0, hi0, n_lo0, jnp.float32(1.0)))
    thr = jnp.where(_count(score >= hi) >= k, hi, lo)
    tie = _tie_index(score, kidx, thr)

    def tie_search():
        need = k - _count(score > thr)

        def ibody(t, j):
            cand = j + jnp.left_shift(jnp.int32(1), idx_bits - 1 - t)
            return jnp.where(_count(tie < cand) < need, cand, j)

        return lax.fori_loop(0, idx_bits, ibody, jnp.zeros((rows, 1), jnp.int32))

    repeated = jnp.max(jnp.where(take_all, 0.0, _count(score == thr))) > 1.5
    jmax = lax.cond(repeated, tie_search, lambda: jnp.full((rows, 1), 2 ** 31 - 2, jnp.int32))
    return thr, jmax, take_all


def _selected(score, kidx, thr, jmax, take_all):
    return take_all | (score > thr) | (_tie_index(score, kidx, thr) <= jmax)


def _dsa_prompt_kernel(q_ref, qi_ref, sm_ref, k_ref, v_ref, kis_ref, o_ref, *, ktop, key_step):
    tq = q_ref.shape[0]
    S = k_ref.shape[0]
    i = pl.program_id(1)
    w0 = SMALL_OFF['w_idx']

    def attend(L):
        ki = kis_ref[:L, :IDX_DIM]
        w = sm_ref[:, w0:w0 + IDX_HEADS] * (IDX_HEADS ** -0.5 * IDX_DIM ** -0.5)
        score = jnp.zeros((tq, L), F32)
        for h in range(IDX_HEADS):
            s = _dot_nt(qi_ref[:, h * IDX_DIM:(h + 1) * IDX_DIM], ki)
            score = score + w[:, h:h + 1] * jnp.maximum(s, 0.0)
        qpos = i * tq + lax.broadcasted_iota(jnp.int32, (tq, 1), 0)
        kidx = lax.broadcasted_iota(jnp.int32, (tq, L), 1)
        allowed = kidx <= qpos
        score = jnp.where(allowed, score, -jnp.inf)
        thr, jmax, take_all = _topk_threshold(score, kidx, qpos + 1, ktop, int(L - 1).bit_length())
        bias = jnp.where(_selected(score, kidx, thr, jmax, take_all) & allowed, 0.0, -jnp.inf)
        for h in range(N_HEADS):
            n = h // GROUP
            sl = slice(h * HEAD_DIM, (h + 1) * HEAD_DIM)
            kv = slice(n * HEAD_DIM, (n + 1) * HEAD_DIM)
            s = _dot_nt(q_ref[:, sl], k_ref[:L, kv]) + bias
            m = jnp.max(s, axis=-1, keepdims=True)
            p = jnp.exp(s - m)
            l = jnp.sum(p, axis=-1, keepdims=True)
            o_ref[:, sl] = _dot(p.astype(BF16), v_ref[:L, kv]) / l

    level = ((i + 1) * tq - 1) // key_step
    for lv in range(S // key_step):
        pl.when(level == lv)(functools.partial(attend, (lv + 1) * key_step))


def _dsa_prompt(q_bf, qi_bf, small_rot, k_bf, v_bf, small_bf, B, S, tq):
    ktop = min(TOPK_MAX, S // 4)
    nq = S // tq
    key_step = min(S, 256)
    row = lambda w: pl.BlockSpec((tq, w), lambda b, i: (b * nq + i, 0))
    full = lambda w: pl.BlockSpec((S, w), lambda b, i: (b, 0))
    return pl.pallas_call(
        functools.partial(_dsa_prompt_kernel, ktop=ktop, key_step=key_step),
        grid=(B, nq),
        in_specs=[row(q_bf.shape[1]), row(qi_bf.shape[1]), row(LANES),
                  full(k_bf.shape[1]), full(v_bf.shape[1]), full(LANES)],
        out_specs=row(q_bf.shape[1]),
        out_shape=jax.ShapeDtypeStruct((B * S, q_bf.shape[1]), F32),
        compiler_params=_cparams("parallel", "arbitrary"),
        name="dsa_prompt",
    )(q_bf, qi_bf, small_rot, k_bf, v_bf, small_bf)


def _idx_score_rows(qi, w, kpage_t):
    s = _dot(qi, kpage_t)
    return jnp.sum(w * jnp.maximum(s, 0.0), axis=0, keepdims=True)


def _sample_scores_kernel(pt_ref, qi_ref, w_ref, knew_ref, *refs):
    page_refs, (o_ref, onew_ref) = refs[:-2], refs[-2:]
    npp = len(page_refs)
    p = pl.program_id(1)
    qi = qi_ref[0]
    w = w_ref[0] * (IDX_HEADS ** -0.5 * IDX_DIM ** -0.5)
    for j, page_ref in enumerate(page_refs):
        o_ref[0, pl.ds(p * npp + j, 1), :] = _idx_score_rows(qi, w, page_ref[0, 0].astype(BF16))

    @pl.when(p == 0)
    def _():
        kn = jnp.broadcast_to(knew_ref[0], (IDX_DIM, PAGE_SIZE))
        sc = _idx_score_rows(qi, w, kn)
        lane = lax.broadcasted_iota(jnp.int32, (1, PAGE_SIZE), 1)
        onew_ref[0] = jnp.where(lane == 0, sc, -jnp.inf)


def _page_specs(page_shape, npp, layer):
    zeros = (0,) * len(page_shape)
    return [pl.BlockSpec((1, 1) + page_shape, lambda b, p, pt, j=j: (layer, pt[b, p * npp + j]) + zeros)
            for j in range(npp)]


def _pages_per_step(n_pages, cap):
    npp = min(cap, n_pages)
    while n_pages % npp:
        npp -= 1
    return npp


def _sample_scores(page_table, qi3, w3, knew3, cache_ki, layer):
    DB, n_pages = page_table.shape
    npp = _pages_per_step(n_pages, 16)
    return pl.pallas_call(
        _sample_scores_kernel,
        grid_spec=pltpu.PrefetchScalarGridSpec(
            num_scalar_prefetch=1,
            grid=(DB, n_pages // npp),
            in_specs=[
                pl.BlockSpec((1, IDX_HEADS, IDX_DIM), lambda b, p, pt: (b, 0, 0)),
                pl.BlockSpec((1, IDX_HEADS, 1), lambda b, p, pt: (b, 0, 0)),
                pl.BlockSpec((1, IDX_DIM, 1), lambda b, p, pt: (b, 0, 0)),
            ] + _page_specs((IDX_DIM, PAGE_SIZE), npp, layer),
            out_specs=[
                pl.BlockSpec((1, n_pages, PAGE_SIZE), lambda b, p, pt: (b, 0, 0)),
                pl.BlockSpec((1, 1, PAGE_SIZE), lambda b, p, pt: (b, 0, 0)),
            ],
        ),
        out_shape=[jax.ShapeDtypeStruct((DB, n_pages, PAGE_SIZE), F32),
                   jax.ShapeDtypeStruct((DB, 1, PAGE_SIZE), F32)],
        compiler_params=_cparams("parallel", "arbitrary"),
        name="sample_scores",
    )(page_table, qi3, w3, knew3, *([cache_ki] * npp))


def _sample_thr_kernel(s_ref, thr_ref, jmax_ref, all_ref, *, ktop, n_valid, idx_bits):
    kidx = lax.broadcasted_iota(jnp.int32, s_ref.shape, 1)
    score = jnp.where(kidx < n_valid, s_ref[...], -jnp.inf)
    n_allowed = jnp.full((s_ref.shape[0], 1), n_valid, jnp.int32)
    thr, jmax, take_all = _topk_threshold(score, kidx, n_allowed, ktop, idx_bits)
    thr_ref[...] = jnp.broadcast_to(thr, thr_ref.shape)
    jmax_ref[...] = jnp.broadcast_to(jmax, jmax_ref.shape)
    all_ref[...] = jnp.broadcast_to(take_all.astype(jnp.int32), all_ref.shape)


def _sample_threshold(scores, ktop, n_valid):
    DB, L = scores.shape
    out = lambda dt: jax.ShapeDtypeStruct((DB, LANES), dt)
    return pl.pallas_call(
        functools.partial(_sample_thr_kernel, ktop=ktop, n_valid=n_valid, idx_bits=int(L - 1).bit_length()),
        out_shape=[out(F32), out(jnp.int32), out(jnp.int32)],
        compiler_params=pltpu.CompilerParams(vmem_limit_bytes=VMEM_LIMIT),
        name="sample_threshold",
    )(scores)


def _sample_attn_kernel(pt_ref, q_ref, s_ref, thr_ref, jmax_ref, all_ref, knew_ref, vnew_ref, *refs, n_pages):
    npp = (len(refs) - 4) // 2
    kp_refs, vp_refs = refs[:npp], refs[npp:2 * npp]
    o_ref, m_ref, l_ref, acc_ref = refs[2 * npp:]
    b = pl.program_id(0)
    p = pl.program_id(1)

    @pl.when(p == 0)
    def _():
        m_ref[...] = jnp.full_like(m_ref, NEG_BIG)
        l_ref[...] = jnp.zeros_like(l_ref)
        acc_ref[...] = jnp.zeros_like(acc_ref)

    thr = thr_ref[pl.ds(b, 1), 0:1]
    jmax = jmax_ref[pl.ds(b, 1), 0:1]
    take_all = all_ref[pl.ds(b, 1), 0:1] > 0

    q = q_ref[0]
    group = lax.broadcasted_iota(jnp.int32, (N_HEADS, 1), 0) // GROUP
    X = PAGE_SIZE * N_KV_HEADS
    own_head = (lax.broadcasted_iota(jnp.int32, (N_HEADS, X), 1) % N_KV_HEADS) == group
    repeat = (lax.broadcasted_iota(jnp.int32, (PAGE_SIZE, X), 1) // N_KV_HEADS
              == lax.broadcasted_iota(jnp.int32, (PAGE_SIZE, X), 0)).astype(BF16)

    def fold(logits, pv_fns):
        m_old = m_ref[...]
        m_new = functools.reduce(jnp.maximum, [jnp.max(s, axis=-1, keepdims=True) for s in logits], m_old)
        corr = jnp.exp(m_old - m_new)
        probs = [jnp.exp(s - m_new) for s in logits]
        l_ref[...] = l_ref[...] * corr + functools.reduce(
            lambda a, c: a + c, [jnp.sum(pr, axis=-1, keepdims=True) for pr in probs])
        acc_ref[...] = acc_ref[...] * corr + functools.reduce(
            lambda a, c: a + c, [fn(pr) for fn, pr in zip(pv_fns, probs)])
        m_ref[...] = m_new

    logits, pv_fns = [], []
    for j in range(npp):
        page = p * npp + j
        scores_row = s_ref[0, pl.ds(page, 1), :]
        kidx = page * PAGE_SIZE + lax.broadcasted_iota(jnp.int32, (1, PAGE_SIZE), 1)
        sel = _selected(scores_row, kidx, thr, jmax, take_all)
        sel_rows = _dot(jnp.broadcast_to(jnp.where(sel, 1.0, 0.0), (N_HEADS, PAGE_SIZE)).astype(BF16), repeat)
        s = _dot_nt(q, kp_refs[j][0].astype(BF16))
        logits.append(jnp.where(own_head & (sel_rows > 0.5), s, -jnp.inf))
        pv_fns.append(lambda pr, ref=vp_refs[j]: _dot(pr.astype(BF16), ref[0].astype(BF16)))
    fold(logits, pv_fns)

    @pl.when(p == pl.num_programs(1) - 1)
    def _():
        def per_head(ref):
            rows = [jnp.where(group == n, jnp.broadcast_to(ref[0, :, n * HEAD_DIM:(n + 1) * HEAD_DIM],
                                                           (N_HEADS, HEAD_DIM)), 0.0) for n in range(N_KV_HEADS)]
            return functools.reduce(lambda a, c: a + c, rows).astype(BF16).astype(F32)

        score_new = s_ref[0, pl.ds(n_pages, 1), 0:1]
        sel = _selected(score_new, jnp.full((1, 1), n_pages * PAGE_SIZE, jnp.int32), thr, jmax, take_all)
        s = jnp.sum(q.astype(F32) * per_head(knew_ref), axis=-1, keepdims=True)
        s = jnp.where(sel, s, -jnp.inf)
        vexp = per_head(vnew_ref)
        fold([s], [lambda pr: pr.astype(BF16).astype(F32) * vexp])
        o_ref[0] = acc_ref[...] / l_ref[...]


def _sample_attn(page_table, q3, scores3, thr, jmax, take_all, knew3, vnew3, cache_k, cache_v, layer):
    DB, n_pages = page_table.shape
    width = N_KV_HEADS * HEAD_DIM
    npp = _pages_per_step(n_pages, 16)
    bsel = lambda *shape: pl.BlockSpec((1,) + shape, lambda b, p, pt: (b,) + (0,) * len(shape))
    whole = pl.BlockSpec((DB, LANES), lambda b, p, pt: (0, 0))
    n_phys = cache_k.shape[1]
    rows = PAGE_SIZE * N_KV_HEADS
    cache_k = cache_k.reshape(-1, rows, HEAD_DIM)
    cache_v = cache_v.reshape(-1, rows, HEAD_DIM)
    pages = [pl.BlockSpec((1, rows, HEAD_DIM),
                          lambda b, p, pt, j=j: (layer * n_phys + pt[b, p * npp + j], 0, 0)) for j in range(npp)]
    return pl.pallas_call(
        functools.partial(_sample_attn_kernel, n_pages=n_pages),
        grid_spec=pltpu.PrefetchScalarGridSpec(
            num_scalar_prefetch=1,
            grid=(DB, n_pages // npp),
            in_specs=[bsel(N_HEADS, HEAD_DIM), bsel(n_pages + 1, PAGE_SIZE), whole, whole, whole,
                      bsel(1, width), bsel(1, width)] + pages + pages,
            out_specs=bsel(N_HEADS, HEAD_DIM),
            scratch_shapes=[pltpu.VMEM((N_HEADS, 1), F32), pltpu.VMEM((N_HEADS, 1), F32),
                            pltpu.VMEM((N_HEADS, HEAD_DIM), F32)],
        ),
        out_shape=jax.ShapeDtypeStruct((DB, N_HEADS, HEAD_DIM), F32),
        compiler_params=_cparams("parallel", "arbitrary"),
        name="sample_attn",
    )(page_table, q3, scores3, thr, jmax, take_all, knew3, vnew3, *([cache_k] * npp), *([cache_v] * npp))


CARRY = 8
MXU_DIM = 256
GDN_GROUP = MXU_DIM // CHUNK
GDN_SPLIT_STAGES = 1


def _spread(a, row_head, hg):
    return jnp.concatenate([jnp.where(row_head == i, a, 0.0) for i in range(hg)], axis=1)


def _gdn_prep_kernel(x_ref, w_ref, buf_ref, q_ref, k_ref, v_ref, conv_ref, xpad_ref):
    tt = x_ref.shape[1]
    t = pl.program_id(1)
    lo = CARRY - (CONV_W - 1)

    @pl.when(t == 0)
    def _():
        xpad_ref[lo:CARRY, :] = buf_ref[0]

    xpad_ref[CARRY:CARRY + tt, :] = x_ref[0]
    nh = GDN_HEADS
    for c in range(CONV_DIM // LANES):
        sl = slice(c * LANES, (c + 1) * LANES)
        y = w_ref[0:1, sl] * xpad_ref[lo:lo + tt, sl]
        for j in range(1, CONV_W):
            y = y + w_ref[j:j + 1, sl] * xpad_ref[lo + j:lo + j + tt, sl]
        y = _silu(y)
        if c < 2 * nh:
            y = y * lax.rsqrt(jnp.sum(y * y, axis=-1, keepdims=True) + NORM_EPS)
        if c < nh:
            q_ref[0, :, sl] = y * (GDN_DK ** -0.5)
        elif c < 2 * nh:
            k_ref[0, :, slice((c - nh) * LANES, (c - nh + 1) * LANES)] = y
        else:
            v_ref[0, :, slice((c - 2 * nh) * LANES, (c - 2 * nh + 1) * LANES)] = y
    last = xpad_ref[lo + tt:CARRY + tt, :]
    xpad_ref[lo:CARRY, :] = last

    @pl.when(t == pl.num_programs(1) - 1)
    def _():
        conv_ref[0] = last


def _gdn_prep(u_gdn3, conv_w, buf, tt):
    B, T, _ = u_gdn3.shape
    w = SIZES['qkv_b']
    hd = GDN_HEADS * GDN_DK
    out = jax.ShapeDtypeStruct((B, T, hd), F32)
    ospec = pl.BlockSpec((1, tt, hd), lambda b, t: (b, t, 0))
    return pl.pallas_call(
        _gdn_prep_kernel,
        grid=(B, T // tt),
        in_specs=[
            pl.BlockSpec((1, tt, w), lambda b, t: (b, t, 0)),
            pl.BlockSpec((CONV_W, w), lambda b, t: (0, 0)),
            pl.BlockSpec((1, CONV_W - 1, w), lambda b, t: (b, 0, 0)),
        ],
        out_specs=[ospec, ospec, ospec, pl.BlockSpec((1, CONV_W - 1, w), lambda b, t: (b, 0, 0))],
        out_shape=[out, out, out, jax.ShapeDtypeStruct((B, CONV_W - 1, w), F32)],
        scratch_shapes=[pltpu.VMEM((CARRY + tt, w), F32)],
        compiler_params=_cparams("parallel", "arbitrary"),
        name="gdn_prep",
    )(u_gdn3, conv_w, buf)


def _gdn_chunk_kernel(q_ref, k_ref, v_ref, z_ref, sm_ref, alog_ref, dtb_ref, gn_ref, s0_ref,
                      o_ref, sout_ref, state_ref, *, t_valid):
    C = q_ref.shape[1]
    c = pl.program_id(1)

    @pl.when(c == 0)
    def _():
        state_ref[...] = s0_ref[0]

    H = GDN_HEADS
    a0, b0 = SMALL_OFF['a_b'], SMALL_OFF['beta_b']
    live = (c * C + lax.broadcasted_iota(jnp.int32, (C, H), 0)) < t_valid
    xs = sm_ref[0, :, a0:a0 + H] + dtb_ref[...]
    softplus = jnp.maximum(xs, 0.0) + jnp.log1p(jnp.exp(-jnp.abs(xs)))
    g_all = jnp.where(live, -jnp.exp(alog_ref[...]) * softplus, 0.0)
    beta_all = jnp.where(live, jax.nn.sigmoid(sm_ref[0, :, b0:b0 + H]), 0.0)
    tri_f = (lax.broadcasted_iota(jnp.int32, (C, C), 0) >= lax.broadcasted_iota(jnp.int32, (C, C), 1)).astype(F32)
    eye_h = (lax.broadcasted_iota(jnp.int32, (H, H), 0) == lax.broadcasted_iota(jnp.int32, (H, H), 1)).astype(F32)
    gcum = _dot_hi(tri_f, g_all)
    gcum_t = _dot_nt_hi(eye_h, gcum)

    HG = GDN_GROUP
    R = HG * C
    ri = lax.broadcasted_iota(jnp.int32, (R, R), 0)
    ci = lax.broadcasted_iota(jnp.int32, (R, R), 1)
    same_head = (ri // C) == (ci // C)
    mask_incl = same_head & (ri >= ci)
    mask_strict = same_head & (ri > ci)
    row_head = lax.broadcasted_iota(jnp.int32, (R, 1), 0) // C
    row_head2 = jnp.concatenate([row_head, row_head], axis=0)

    for grp in range(H // HG):
        heads = range(grp * HG, (grp + 1) * HG)
        sls = [slice(h * GDN_DK, (h + 1) * GDN_DK) for h in heads]
        rows = lambda ref: jnp.concatenate([ref[0, :, sl] for sl in sls], axis=0)
        cols = lambda a: jnp.concatenate([a[:, h:h + 1] for h in heads], axis=0)
        q, k, v = rows(q_ref), rows(k_ref), rows(v_ref)
        beta = cols(beta_all)
        gcol = cols(gcum)
        grow = jnp.concatenate([gcum_t[h:h + 1, :] for h in heads], axis=1)
        glast = jnp.concatenate([jnp.broadcast_to(gcum[C - 1:C, h:h + 1], (C, 1)) for h in heads], axis=0)
        decay = jnp.where(mask_incl, jnp.exp(jnp.where(mask_incl, gcol - grow, 0.0)), 0.0)
        kb = k * beta
        kk_qk = _dot_nt(jnp.concatenate([kb, q], axis=0).astype(BF16), k.astype(BF16))
        nmat = jnp.where(mask_strict, kk_qk[:R] * decay, 0.0)
        qk = kk_qk[R:] * decay
        x = jnp.concatenate([v * beta, kb * jnp.exp(gcol)], axis=1)
        pw = nmat
        x = x - _dot_split(pw, x)
        for stage in range(max(C - 1, 1).bit_length() - 1):
            mm = _dot_split if stage < GDN_SPLIT_STAGES else _dot_bf16
            pw = mm(pw, pw)
            x = x + mm(pw, x)
        u, w = x[:, :GDN_DV], x[:, GDN_DV:]
        s_stack = state_ref[grp * HG:(grp + 1) * HG].reshape(HG * GDN_DK, GDN_DV)
        w_q = jnp.concatenate([w, q * jnp.exp(gcol)], axis=0)
        ws_qs = _dot(_spread(w_q, row_head2, HG).astype(BF16), s_stack.astype(BF16))
        v_new = u - ws_qs[:R]
        o = ws_qs[R:] + _dot(qk.astype(BF16), v_new.astype(BF16))
        kdec_t = (k * jnp.exp(glast - gcol)).T
        s_add = _dot(kdec_t.astype(BF16), _spread(v_new, row_head, HG).astype(BF16))
        on = o * lax.rsqrt(jnp.mean(o * o, axis=-1, keepdims=True) + NORM_EPS) * gn_ref[...]
        for a, h in enumerate(heads):
            state_ref[h] = (state_ref[h] * jnp.exp(gcum[C - 1:C, h:h + 1])
                            + s_add[:, a * GDN_DV:(a + 1) * GDN_DV])
            o_ref[0, :, sls[a]] = on[a * C:(a + 1) * C] * _silu(z_ref[0, :, sls[a]])

    @pl.when(c == pl.num_programs(1) - 1)
    def _():
        sout_ref[0] = state_ref[...]


def _gdn_chunks(qn, kn, vv, u_gate3, small3, a_log, dt_bias, gn, s0, t_valid):
    B, Tp, hd = qn.shape
    nc = Tp // CHUNK
    blk = pl.BlockSpec((1, CHUNK, hd), lambda b, c: (b, c, 0))
    vec = lambda w: pl.BlockSpec((1, w), lambda b, c: (0, 0))
    st = pl.BlockSpec((1, GDN_HEADS, GDN_DK, GDN_DV), lambda b, c: (b, 0, 0, 0))
    return pl.pallas_call(
        functools.partial(_gdn_chunk_kernel, t_valid=t_valid),
        grid=(B, nc),
        in_specs=[blk, blk, blk,
                  pl.BlockSpec((1, CHUNK, hd), lambda b, c, o=GATE_OFF['z_b'] // hd: (b, c, o)),
                  pl.BlockSpec((1, CHUNK, LANES), lambda b, c: (b, c, 0)),
                  vec(GDN_HEADS), vec(GDN_HEADS), vec(GDN_DV), st],
        out_specs=[blk, st],
        out_shape=[jax.ShapeDtypeStruct((B, Tp, hd), F32),
                   jax.ShapeDtypeStruct((B, GDN_HEADS, GDN_DK, GDN_DV), F32)],
        scratch_shapes=[pltpu.VMEM((GDN_HEADS, GDN_DK, GDN_DV), F32)],
        compiler_params=_cparams("parallel", "arbitrary"),
        name="gdn_chunks",
    )(qn, kn, vv, u_gate3, small3, a_log, dt_bias, gn, s0)


def _merge_kernel(x_ref, oa_ref, ob_ref, ga_ref, gb_ref, wo_ref, g_ref, b_ref, o_ref):
    merged = jax.nn.sigmoid(ga_ref[...]) * oa_ref[...] + jax.nn.sigmoid(gb_ref[...]) * ob_ref[...]
    y = DN_ALPHA * x_ref[...] + _dot(merged.astype(BF16), wo_ref[...])
    o_ref[...] = _layer_norm(y, g_ref[...], b_ref[...])


def _merge_proj_ln(x, o_a, o_b, u_gate, w_o, g, b, tm):
    M, D = x.shape
    row = pl.BlockSpec((tm, D), lambda i: (i, 0))
    col = lambda name: pl.BlockSpec((tm, D), lambda i, o=GATE_OFF[name] // D: (i, o))
    vec = pl.BlockSpec((1, D), lambda i: (0, 0))
    return pl.pallas_call(
        _merge_kernel,
        grid=(M // tm,),
        in_specs=[row, row, row, col('gate_a'), col('gate_b'),
                  pl.BlockSpec((D, D), lambda i: (0, 0)), vec, vec],
        out_specs=row,
        out_shape=jax.ShapeDtypeStruct((M, D), F32),
        compiler_params=_cparams("parallel"),
        name="merge_proj_ln",
    )(x, o_a, o_b, u_gate, u_gate, w_o, g, b)


def _tiles(M):
    return (512, 256) if M % 512 == 0 else (M, M)


def _layer(x, B, T, pos, wts, conv_buf, ssm0, dsa_fn):
    M = B * T
    tm, te = _tiles(M)
    row = lambda a: a.reshape(1, -1)
    x1 = _ffn_ln(x, wts['ffn1_g'], wts['ffn1_u'], wts['ffn1_d'], row(wts['ln1_g']), row(wts['ln1_b']),
                 tm, 512)
    tp = 1024 if M % 1024 == 0 else tm
    u_attn = _proj(x1, wts['w_attn'], tp, 1024)
    u_gdn = _proj(x1, wts['w_gdn'], tp, 1024)
    u_gate = _proj(x1, wts['w_gate'], tp, 1024)
    u_small = _proj(x1, wts['w_small'], tp, LANES)

    pos_rows = pos if T > 1 else jnp.broadcast_to(pos, (M,))
    tabs = (_rope_tables(pos_rows, ROPE_DIM, HEAD_DIM) + _rope_tables(pos_rows, IDX_ROPE_DIM, IDX_DIM)
            + _rope_tables(pos_rows, IDX_ROPE_DIM, IDX_DIM, live_lanes=IDX_DIM))
    q_bf, k_rot, k_bf, v_bf, qi_bf, small_rot, small_bf = _rope_prep(u_attn, u_small, tabs, te)
    v_rows = u_attn[:, ATTN_OFF['v_a']:ATTN_OFF['v_a'] + SIZES['v_a']]
    o_a = dsa_fn(v_rows, q_bf, k_rot, k_bf, v_bf, qi_bf, small_rot, small_bf)

    qn, kn, vv, conv_new = _gdn_prep(u_gdn.reshape(B, T, -1), wts['conv_w'], conv_buf, min(T, 256))
    pad = (-T) % CHUNK
    pad3 = lambda a: jnp.pad(a, ((0, 0), (0, pad), (0, 0))) if pad else a
    o_b, ssm_new = _gdn_chunks(pad3(qn), pad3(kn), pad3(vv), pad3(u_gate.reshape(B, T, -1)),
                               pad3(u_small.reshape(B, T, LANES)),
                               row(wts['a_log']), row(wts['dt_bias']), row(wts['gdn_norm_g']), ssm0, T)
    o_b = o_b[:, :T].reshape(M, D_MODEL)

    x2 = _merge_proj_ln(x1, o_a, o_b, u_gate, wts['w_o'], row(wts['ln2_g']), row(wts['ln2_b']), te)
    y = _ffn_ln(x2, wts['ffn2_g'], wts['ffn2_u'], wts['ffn2_d'], row(wts['ln3_g']), row(wts['ln3_b']),
                tm, 512)
    ki_rows = small_rot[:, SMALL_OFF['k_idx']:SMALL_OFF['k_idx'] + IDX_DIM]
    return y, (k_rot, v_rows, ki_rows, ssm_new, conv_new)


def _split_w_in(w_in):
    offs = dict(zip([nm for nm, _ in IN_SPLITS], np.cumsum([0] + [n for _, n in IN_SPLITS])))

    def span(order):
        lo = offs[order[0]]
        hi = offs[order[-1]] + SIZES[order[-1]]
        assert hi - lo == sum(SIZES[nm] for nm in order)
        return w_in[:, lo:hi].astype(BF16)

    w_small = jnp.concatenate([w_in[:, offs[nm]:offs[nm] + SIZES[nm]] for nm in SMALL_ORDER], axis=1)
    w_small = jnp.pad(w_small, ((0, 0), (0, LANES - SMALL_USED))).astype(BF16)
    return span(ATTN_ORDER), span(GDN_ORDER), span(GATE_ORDER), w_small


def kernel(x_prompt, x_sample, cache_k, cache_v, cache_idx_k, state_ssm, state_conv, page_table, ffn1_w_gate, ffn1_w_up, ffn1_w_down, ln1_g, ln1_b, w_in, conv_w, a_log, dt_bias, gdn_norm_g, w_o, ln2_g, ln2_b, ffn2_w_gate, ffn2_w_up, ffn2_w_down, ln3_g, ln3_b):
    B, S, _ = x_prompt.shape
    DB, T, _ = x_sample.shape
    assert T == 1, "the sample path handles one new token per sequence"
    n_pages = page_table.shape[1]
    yp = x_prompt.reshape(B * S, D_MODEL)
    ys = x_sample.reshape(DB * T, D_MODEL)
    outs_p, outs_s = [], []
    for l in range(ffn1_w_gate.shape[0]):
        w_attn, w_gdn, w_gate, w_small = _split_w_in(w_in[l])
        wts = dict(
            ffn1_g=ffn1_w_gate[l].astype(BF16), ffn1_u=ffn1_w_up[l].astype(BF16), ffn1_d=ffn1_w_down[l].astype(BF16),
            ffn2_g=ffn2_w_gate[l].astype(BF16), ffn2_u=ffn2_w_up[l].astype(BF16), ffn2_d=ffn2_w_down[l].astype(BF16),
            ln1_g=ln1_g[l], ln1_b=ln1_b[l], ln2_g=ln2_g[l], ln2_b=ln2_b[l], ln3_g=ln3_g[l], ln3_b=ln3_b[l],
            w_attn=w_attn, w_gdn=w_gdn, w_gate=w_gate, w_small=w_small, w_o=w_o[l].astype(BF16), conv_w=conv_w[l],
            a_log=a_log[l], dt_bias=dt_bias[l], gdn_norm_g=gdn_norm_g[l],
        )

        def dsa_p(v_rows, q_bf, k_rot, k_bf, v_bf, qi_bf, small_rot, small_bf):
            return _dsa_prompt(q_bf, qi_bf, small_rot, k_bf, v_bf, small_bf, B, S, 128)

        def dsa_s(v_rows, q_bf, k_rot, k_bf, v_bf, qi_bf, small_rot, small_bf, l=l):
            w0 = SMALL_OFF['w_idx']
            w3 = small_rot[:, w0:w0 + IDX_HEADS].reshape(DB, IDX_HEADS, 1)
            qi3 = qi_bf.reshape(DB, IDX_HEADS, IDX_DIM)
            knew_i = small_bf[:, :IDX_DIM].reshape(DB, IDX_DIM, 1)
            width = N_KV_HEADS * HEAD_DIM
            past, new = _sample_scores(page_table, qi3, w3, knew_i, jnp.swapaxes(cache_idx_k, 2, 3), l)
            scores3 = jnp.concatenate([past, new], axis=1)
            n_keys = n_pages * PAGE_SIZE + T
            ktop = min(TOPK_MAX, n_keys // 4)
            thr, jmax, take_all = _sample_threshold(scores3.reshape(DB, -1), ktop, n_keys)
            o = _sample_attn(page_table, q_bf.reshape(DB, N_HEADS, HEAD_DIM), scores3, thr, jmax, take_all,
                             k_rot.reshape(DB, 1, width), v_rows.reshape(DB, 1, width), cache_k, cache_v, l)
            return o.reshape(DB, D_MODEL)

        conv0 = jnp.zeros((B, CONV_W - 1, CONV_DIM), F32)
        ssm_zero = jnp.zeros((B, GDN_HEADS, GDN_DK, GDN_DV), F32)
        yp, st_p = _layer(yp, B, S, jnp.arange(S, dtype=jnp.int32), wts, conv0, ssm_zero, dsa_p)
        past_len = n_pages * PAGE_SIZE
        ys, st_s = _layer(ys, DB, T, past_len + jnp.arange(T, dtype=jnp.int32), wts, state_conv[l], state_ssm[l], dsa_s)
        outs_p.append(st_p)
        outs_s.append(st_s)

    def stack(outs, nb, nt):
        d = len(outs)
        k, v, ki, ssm, conv = [a[0][None] if d == 1 else jnp.stack(a) for a in zip(*outs)]
        return (k.reshape(d, nb, nt, N_KV_HEADS, HEAD_DIM), v.reshape(d, nb, nt, N_KV_HEADS, HEAD_DIM),
                ki.reshape(d, nb, nt, IDX_DIM), ssm, conv)

    return (yp.reshape(B, S, D_MODEL), ys.reshape(DB, T, D_MODEL)) + stack(outs_p, B, S) + stack(outs_s, DB, T)
```

```python
import functools

import jax
import jax.numpy as jnp
import numpy as np
from jax import lax
from jax.experimental import pallas as pl
from jax.experimental.pallas import tpu as pltpu

D_MODEL = 2048
PAST_LEN = 16384
PAGE_SIZE = 128
HEAD_DIM = 128
N_HEADS = D_MODEL // HEAD_DIM
N_KV_HEADS = 4
GROUP = N_HEADS // N_KV_HEADS
ROPE_DIM = HEAD_DIM // 4
IDX_HEADS = 16
IDX_DIM = 64
IDX_ROPE_DIM = IDX_DIM // 4
TOPK_MAX = 256
ROPE_THETA = 500000.0
GDN_DK = 128
GDN_DV = 128
GDN_HEADS = D_MODEL // GDN_DV
CONV_W = 4
CONV_DIM = 2 * GDN_HEADS * GDN_DK + GDN_HEADS * GDN_DV
CHUNK = 64
D_FF = 5632
LN_EPS = 1e-5
NORM_EPS = 1e-6
DEPTH = 1
DN_ALPHA = (2 * DEPTH) ** 0.25

IN_SPLITS = (
    ('q_a', N_HEADS * HEAD_DIM), ('k_a', N_KV_HEADS * HEAD_DIM), ('v_a', N_KV_HEADS * HEAD_DIM),
    ('q_idx', IDX_HEADS * IDX_DIM), ('k_idx', IDX_DIM), ('w_idx', IDX_HEADS),
    ('qkv_b', CONV_DIM), ('a_b', GDN_HEADS), ('beta_b', GDN_HEADS), ('z_b', GDN_HEADS * GDN_DV),
    ('gate_a', D_MODEL), ('gate_b', D_MODEL),
)
ATTN_ORDER = ('q_a', 'k_a', 'v_a', 'q_idx')
GDN_ORDER = ('qkv_b',)
GATE_ORDER = ('z_b', 'gate_a', 'gate_b')
SMALL_ORDER = ('k_idx', 'w_idx', 'a_b', 'beta_b')
LANES = 128
VMEM_LIMIT = 56 * 1024 * 1024
NEG_BIG = -1e30
INT_MIN = -2 ** 31

F32 = jnp.float32
BF16 = jnp.bfloat16


def _offsets(order):
    sizes = dict(IN_SPLITS)
    offs, o = {}, 0
    for nm in order:
        offs[nm] = o
        o += sizes[nm]
    return offs, o


ATTN_OFF, ATTN_COLS = _offsets(ATTN_ORDER)
GATE_OFF, GATE_COLS = _offsets(GATE_ORDER)
SMALL_OFF, SMALL_USED = _offsets(SMALL_ORDER)
SIZES = dict(IN_SPLITS)


def _cparams(*sem):
    return pltpu.CompilerParams(dimension_semantics=sem, vmem_limit_bytes=VMEM_LIMIT)


def _dot(a, b):
    return jnp.dot(a, b, preferred_element_type=F32)


def _dot_nt(a, b):
    return lax.dot_general(a, b, (((1,), (1,)), ((), ())), preferred_element_type=F32)


def _dot_hi(a, b):
    return jnp.dot(a, b, preferred_element_type=F32, precision=lax.Precision.HIGHEST)


def _dot_nt_hi(a, b):
    return lax.dot_general(a, b, (((1,), (1,)), ((), ())), preferred_element_type=F32,
                           precision=lax.Precision.HIGHEST)


def _dot_tn_hi(a, b):
    return lax.dot_general(a, b, (((0,), (0,)), ((), ())), preferred_element_type=F32,
                           precision=lax.Precision.HIGHEST)


def _dot_bf16(a, b):
    return _dot(a.astype(BF16), b.astype(BF16))


def _silu(x):
    return x * jax.nn.sigmoid(x)


def _layer_norm(y, g, b):
    mu = jnp.mean(y, axis=-1, keepdims=True)
    d = y - mu
    var = jnp.mean(d * d, axis=-1, keepdims=True)
    return d * lax.rsqrt(var + LN_EPS) * g + b


def _ffn_ln_kernel(x_ref, wg_ref, wu_ref, wd_ref, g_ref, b_ref, o_ref, acc_ref, xb_ref):
    j = pl.program_id(1)

    @pl.when(j == 0)
    def _():
        acc_ref[...] = jnp.zeros_like(acc_ref)
        xb_ref[...] = x_ref[...].astype(BF16)

    xb = xb_ref[...]
    hg = _dot(xb, wg_ref[...])
    hu = _dot(xb, wu_ref[...])
    h = _silu(hg) * hu
    acc_ref[...] += _dot(h.astype(BF16), wd_ref[...])

    @pl.when(j == pl.num_programs(1) - 1)
    def _():
        y = DN_ALPHA * x_ref[...] + 0.5 * acc_ref[...]
        o_ref[...] = _layer_norm(y, g_ref[...], b_ref[...])


def _ffn_ln(x, wg, wu, wd, g, b, tm, tf):
    M, D = x.shape
    F = wg.shape[1]
    return pl.pallas_call(
        _ffn_ln_kernel,
        grid=(M // tm, F // tf),
        in_specs=[
            pl.BlockSpec((tm, D), lambda i, j: (i, 0)),
            pl.BlockSpec((D, tf), lambda i, j: (0, j)),
            pl.BlockSpec((D, tf), lambda i, j: (0, j)),
            pl.BlockSpec((tf, D), lambda i, j: (j, 0)),
            pl.BlockSpec((1, D), lambda i, j: (0, 0)),
            pl.BlockSpec((1, D), lambda i, j: (0, 0)),
        ],
        out_specs=pl.BlockSpec((tm, D), lambda i, j: (i, 0)),
        out_shape=jax.ShapeDtypeStruct((M, D), F32),
        scratch_shapes=[pltpu.VMEM((tm, D), F32), pltpu.VMEM((tm, D), BF16)],
        compiler_params=_cparams("parallel", "arbitrary"),
        name="ffn_ln",
    )(x, wg, wu, wd, g, b)


def _proj_kernel(x_ref, w_ref, o_ref, xb_ref):
    @pl.when(pl.program_id(1) == 0)
    def _():
        xb_ref[...] = x_ref[...].astype(BF16)

    o_ref[...] = _dot(xb_ref[...], w_ref[...])


def _proj(x, w, tm, tn):
    M, K = x.shape
    N = w.shape[1]
    return pl.pallas_call(
        _proj_kernel,
        grid=(M // tm, N // tn),
        in_specs=[
            pl.BlockSpec((tm, K), lambda i, j: (i, 0)),
            pl.BlockSpec((K, tn), lambda i, j: (0, j)),
        ],
        out_specs=pl.BlockSpec((tm, tn), lambda i, j: (i, j)),
        out_shape=jax.ShapeDtypeStruct((M, N), F32),
        scratch_shapes=[pltpu.VMEM((tm, K), BF16)],
        compiler_params=_cparams("parallel", "arbitrary"),
        name="in_proj",
    )(x, w)


def _rope_tables(pos, rot_dim, period, live_lanes=LANES):
    half = rot_dim // 2
    inv = ROPE_THETA ** (-jnp.arange(half, dtype=F32) / half)
    ang = pos.astype(F32)[:, None] * inv[None, :]
    cos, sin = jnp.cos(ang), jnp.sin(ang)
    lane = np.arange(LANES)
    lp = lane % period
    idx = lp % half
    live = (lp < rot_dim) & (lane < live_lanes)
    c = jnp.where(live[None, :], cos[:, idx], 1.0)
    s = jnp.where(live[None, :], jnp.where((lp < half)[None, :], -sin[:, idx], sin[:, idx]), 0.0)
    return c.astype(F32), s.astype(F32)


def _rope_tile(x, c, s, half, period):
    lane = lax.broadcasted_iota(jnp.int32, x.shape, 1)
    first = (lane & (period - 1)) < half
    partner = jnp.where(first, pltpu.roll(x, LANES - half, 1), pltpu.roll(x, half, 1))
    return x * c + partner * s


def _rope_kernel(q_ref, k_ref, v_ref, qi_ref, sm_ref, ca_ref, sa_ref, ci_ref, si_ref, cs_ref, ss_ref,
                 qo_ref, ko_ref, kb_ref, vb_ref, qio_ref, smo_ref, smb_ref):
    ca, sa = ca_ref[...], sa_ref[...]
    ci, si = ci_ref[...], si_ref[...]
    for h in range(N_HEADS):
        sl = slice(h * LANES, (h + 1) * LANES)
        qo_ref[:, sl] = (_rope_tile(q_ref[:, sl], ca, sa, ROPE_DIM // 2, HEAD_DIM) * (HEAD_DIM ** -0.5)).astype(BF16)
    for h in range(N_KV_HEADS):
        sl = slice(h * LANES, (h + 1) * LANES)
        kr = _rope_tile(k_ref[:, sl], ca, sa, ROPE_DIM // 2, HEAD_DIM)
        ko_ref[:, sl] = kr
        kb_ref[:, sl] = kr.astype(BF16)
    vb_ref[...] = v_ref[...].astype(BF16)
    for h in range(IDX_HEADS * IDX_DIM // LANES):
        sl = slice(h * LANES, (h + 1) * LANES)
        qio_ref[:, sl] = _rope_tile(qi_ref[:, sl], ci, si, IDX_ROPE_DIM // 2, IDX_DIM).astype(BF16)
    sm = _rope_tile(sm_ref[...], cs_ref[...], ss_ref[...], IDX_ROPE_DIM // 2, IDX_DIM)
    smo_ref[...] = sm
    smb_ref[...] = sm.astype(BF16)


def _rope_prep(u_attn, u_small, tabs, tm):
    M = u_attn.shape[0]
    tpos = tabs[0].shape[0]
    nt = tpos // tm

    def col(name):
        w = SIZES[name]
        return pl.BlockSpec((tm, w), lambda i, o=ATTN_OFF[name] // w: (i, o))

    tab_spec = pl.BlockSpec((tm, LANES), lambda i: (i % nt, 0))
    row = lambda w: pl.BlockSpec((tm, w), lambda i: (i, 0))
    return pl.pallas_call(
        _rope_kernel,
        grid=(M // tm,),
        in_specs=[col('q_a'), col('k_a'), col('v_a'), col('q_idx'), row(LANES)] + [tab_spec] * 6,
        out_specs=[row(SIZES['q_a']), row(SIZES['k_a']), row(SIZES['k_a']), row(SIZES['v_a']),
                   row(SIZES['q_idx']), row(LANES), row(LANES)],
        out_shape=[
            jax.ShapeDtypeStruct((M, SIZES['q_a']), BF16),
            jax.ShapeDtypeStruct((M, SIZES['k_a']), F32),
            jax.ShapeDtypeStruct((M, SIZES['k_a']), BF16),
            jax.ShapeDtypeStruct((M, SIZES['v_a']), BF16),
            jax.ShapeDtypeStruct((M, SIZES['q_idx']), BF16),
            jax.ShapeDtypeStruct((M, LANES), F32),
            jax.ShapeDtypeStruct((M, LANES), BF16),
        ],
        compiler_params=_cparams("parallel"),
        name="rope_prep",
    )(u_attn, u_attn, u_attn, u_attn, u_small, *tabs)


BISECT_UNROLL = 4


def _count(pred):
    return jnp.sum(jnp.where(pred, 1.0, 0.0), axis=-1, keepdims=True)


def _tie_index(score, kidx, thr):
    return jnp.where(score == thr, kidx, jnp.int32(2 ** 31 - 1))


def _topk_threshold(score, kidx, n_allowed, k, idx_bits):
    rows = score.shape[0]
    take_all = n_allowed <= k
    lo0 = jnp.min(jnp.where(score == -jnp.inf, jnp.inf, score), axis=-1, keepdims=True)
    hi0 = jnp.max(score, axis=-1, keepdims=True)
    lo0 = jnp.where(take_all, 0.0, lo0)
    hi0 = jnp.where(take_all, 0.0, hi0)

    def step(lo, hi, n_lo):
        mid = 0.5 * lo + 0.5 * hi
        n_mid = _count(score >= mid)
        ge = n_mid >= k
        return jnp.where(ge, mid, lo), jnp.where(ge, hi, mid), jnp.where(ge, n_mid, n_lo)

    def body(state):
        lo, hi, n_lo, _ = state
        for _ in range(BISECT_UNROLL):
            lo, hi, n_lo = step(lo, hi, n_lo)
        mid = 0.5 * lo + 0.5 * hi
        still_open = jnp.max(jnp.where((mid > lo) & (mid < hi) & (n_lo > k), 1.0, 0.0))
        return lo, hi, n_lo, still_open

    n_lo0 = jnp.where(take_all, float(k), n_allowed.astype(F32))
    lo, hi, _, _ = lax.while_loop(lambda state: state[3] > 0.5, body, (lo0, hi0, n_lo0, jnp.float32(1.0)))
    thr = jnp.where(_count(score >= hi) >= k, hi, lo)
    tie = _tie_index(score, kidx, thr)

    def tie_search():
        need = k - _count(score > thr)

        def ibody(t, j):
            cand = j + jnp.left_shift(jnp.int32(1), idx_bits - 1 - t)
            return jnp.where(_count(tie < cand) < need, cand, j)

        return lax.fori_loop(0, idx_bits, ibody, jnp.zeros((rows, 1), jnp.int32))

    repeated = jnp.max(jnp.where(take_all, 0.0, _count(score == thr))) > 1.5
    jmax = lax.cond(repeated, tie_search, lambda: jnp.full((rows, 1), 2 ** 31 - 2, jnp.int32))
    return thr, jmax, take_all


def _selected(score, kidx, thr, jmax, take_all):
    return take_all | (score > thr) | (_tie_index(score, kidx, thr) <= jmax)


def _dsa_prompt_kernel(q_ref, qi_ref, sm_ref, k_ref, v_ref, kis_ref, o_ref, *, ktop, key_step):
    tq = q_ref.shape[0]
    S = k_ref.shape[0]
    i = pl.program_id(1)
    w0 = SMALL_OFF['w_idx']

    def attend(L):
        ki = kis_ref[:L, :IDX_DIM]
        w = sm_ref[:, w0:w0 + IDX_HEADS] * (IDX_HEADS ** -0.5 * IDX_DIM ** -0.5)
        score = jnp.zeros((tq, L), F32)
        for h in range(IDX_HEADS):
            s = _dot_nt(qi_ref[:, h * IDX_DIM:(h + 1) * IDX_DIM], ki)
            score = score + w[:, h:h + 1] * jnp.maximum(s, 0.0)
        qpos = i * tq + lax.broadcasted_iota(jnp.int32, (tq, 1), 0)
        kidx = lax.broadcasted_iota(jnp.int32, (tq, L), 1)
        allowed = kidx <= qpos
        score = jnp.where(allowed, score, -jnp.inf)
        thr, jmax, take_all = _topk_threshold(score, kidx, qpos + 1, ktop, int(L - 1).bit_length())
        bias = jnp.where(_selected(score, kidx, thr, jmax, take_all) & allowed, 0.0, -jnp.inf)
        for h in range(N_HEADS):
            n = h // GROUP
            sl = slice(h * HEAD_DIM, (h + 1) * HEAD_DIM)
            kv = slice(n * HEAD_DIM, (n + 1) * HEAD_DIM)
            s = _dot_nt(q_ref[:, sl], k_ref[:L, kv]) + bias
            m = jnp.max(s, axis=-1, keepdims=True)
            p = jnp.exp(s - m)
            l = jnp.sum(p, axis=-1, keepdims=True)
            o_ref[:, sl] = _dot(p.astype(BF16), v_ref[:L, kv]) / l

    level = ((i + 1) * tq - 1) // key_step
    for lv in range(S // key_step):
        pl.when(level == lv)(functools.partial(attend, (lv + 1) * key_step))


def _dsa_prompt(q_bf, qi_bf, small_rot, k_bf, v_bf, small_bf, B, S, tq):
    ktop = min(TOPK_MAX, S // 4)
    nq = S // tq
    key_step = min(S, 512)
    row = lambda w: pl.BlockSpec((tq, w), lambda b, i: (b * nq + i, 0))
    full = lambda w: pl.BlockSpec((S, w), lambda b, i: (b, 0))
    return pl.pallas_call(
        functools.partial(_dsa_prompt_kernel, ktop=ktop, key_step=key_step),
        grid=(B, nq),
        in_specs=[row(q_bf.shape[1]), row(qi_bf.shape[1]), row(LANES),
                  full(k_bf.shape[1]), full(v_bf.shape[1]), full(LANES)],
        out_specs=row(q_bf.shape[1]),
        out_shape=jax.ShapeDtypeStruct((B * S, q_bf.shape[1]), F32),
        compiler_params=_cparams("parallel", "arbitrary"),
        name="dsa_prompt",
    )(q_bf, qi_bf, small_rot, k_bf, v_bf, small_bf)


def _idx_score_rows(qi, w, kpage_t):
    s = _dot(qi, kpage_t)
    return jnp.sum(w * jnp.maximum(s, 0.0), axis=0, keepdims=True)


def _sample_scores_kernel(pt_ref, qi_ref, w_ref, knew_ref, *refs):
    page_refs, (o_ref, onew_ref) = refs[:-2], refs[-2:]
    npp = len(page_refs)
    p = pl.program_id(1)
    qi = qi_ref[0]
    w = w_ref[0] * (IDX_HEADS ** -0.5 * IDX_DIM ** -0.5)
    for j, page_ref in enumerate(page_refs):
        o_ref[0, pl.ds(p * npp + j, 1), :] = _idx_score_rows(qi, w, page_ref[0, 0].astype(BF16))

    @pl.when(p == 0)
    def _():
        kn = jnp.broadcast_to(knew_ref[0], (IDX_DIM, PAGE_SIZE))
        sc = _idx_score_rows(qi, w, kn)
        lane = lax.broadcasted_iota(jnp.int32, (1, PAGE_SIZE), 1)
        onew_ref[0] = jnp.where(lane == 0, sc, -jnp.inf)


def _page_specs(page_shape, npp, layer):
    zeros = (0,) * len(page_shape)
    return [pl.BlockSpec((1, 1) + page_shape, lambda b, p, pt, j=j: (layer, pt[b, p * npp + j]) + zeros)
            for j in range(npp)]


def _pages_per_step(n_pages, cap):
    npp = min(cap, n_pages)
    while n_pages % npp:
        npp -= 1
    return npp


def _sample_scores(page_table, qi3, w3, knew3, cache_ki, layer):
    DB, n_pages = page_table.shape
    npp = _pages_per_step(n_pages, 16)
    return pl.pallas_call(
        _sample_scores_kernel,
        grid_spec=pltpu.PrefetchScalarGridSpec(
            num_scalar_prefetch=1,
            grid=(DB, n_pages // npp),
            in_specs=[
                pl.BlockSpec((1, IDX_HEADS, IDX_DIM), lambda b, p, pt: (b, 0, 0)),
                pl.BlockSpec((1, IDX_HEADS, 1), lambda b, p, pt: (b, 0, 0)),
                pl.BlockSpec((1, IDX_DIM, 1), lambda b, p, pt: (b, 0, 0)),
            ] + _page_specs((IDX_DIM, PAGE_SIZE), npp, layer),
            out_specs=[
                pl.BlockSpec((1, n_pages, PAGE_SIZE), lambda b, p, pt: (b, 0, 0)),
                pl.BlockSpec((1, 1, PAGE_SIZE), lambda b, p, pt: (b, 0, 0)),
            ],
        ),
        out_shape=[jax.ShapeDtypeStruct((DB, n_pages, PAGE_SIZE), F32),
                   jax.ShapeDtypeStruct((DB, 1, PAGE_SIZE), F32)],
        compiler_params=_cparams("parallel", "arbitrary"),
        name="sample_scores",
    )(page_table, qi3, w3, knew3, *([cache_ki] * npp))


def _sample_thr_kernel(s_ref, thr_ref, jmax_ref, all_ref, *, ktop, n_valid, idx_bits):
    kidx = lax.broadcasted_iota(jnp.int32, s_ref.shape, 1)
    score = jnp.where(kidx < n_valid, s_ref[...], -jnp.inf)
    n_allowed = jnp.full((s_ref.shape[0], 1), n_valid, jnp.int32)
    thr, jmax, take_all = _topk_threshold(score, kidx, n_allowed, ktop, idx_bits)
    thr_ref[...] = jnp.broadcast_to(thr, thr_ref.shape)
    jmax_ref[...] = jnp.broadcast_to(jmax, jmax_ref.shape)
    all_ref[...] = jnp.broadcast_to(take_all.astype(jnp.int32), all_ref.shape)


def _sample_threshold(scores, ktop, n_valid):
    DB, L = scores.shape
    out = lambda dt: jax.ShapeDtypeStruct((DB, LANES), dt)
    return pl.pallas_call(
        functools.partial(_sample_thr_kernel, ktop=ktop, n_valid=n_valid, idx_bits=int(L - 1).bit_length()),
        out_shape=[out(F32), out(jnp.int32), out(jnp.int32)],
        compiler_params=pltpu.CompilerParams(vmem_limit_bytes=VMEM_LIMIT),
        name="sample_threshold",
    )(scores)


def _sample_attn_kernel(pt_ref, q_ref, s_ref, thr_ref, jmax_ref, all_ref, knew_ref, vnew_ref, *refs, n_pages):
    npp = (len(refs) - 4) // 2
    kp_refs, vp_refs = refs[:npp], refs[npp:2 * npp]
    o_ref, m_ref, l_ref, acc_ref = refs[2 * npp:]
    b = pl.program_id(0)
    p = pl.program_id(1)

    @pl.when(p == 0)
    def _():
        m_ref[...] = jnp.full_like(m_ref, NEG_BIG)
        l_ref[...] = jnp.zeros_like(l_ref)
        acc_ref[...] = jnp.zeros_like(acc_ref)

    thr = thr_ref[pl.ds(b, 1), 0:1]
    jmax = jmax_ref[pl.ds(b, 1), 0:1]
    take_all = all_ref[pl.ds(b, 1), 0:1] > 0

    q = q_ref[0]
    group = lax.broadcasted_iota(jnp.int32, (N_HEADS, 1), 0) // GROUP
    X = PAGE_SIZE * N_KV_HEADS
    own_head = (lax.broadcasted_iota(jnp.int32, (N_HEADS, X), 1) % N_KV_HEADS) == group
    repeat = (lax.broadcasted_iota(jnp.int32, (PAGE_SIZE, X), 1) // N_KV_HEADS
              == lax.broadcasted_iota(jnp.int32, (PAGE_SIZE, X), 0)).astype(BF16)

    def fold(logits, pv_fns):
        m_old = m_ref[...]
        m_new = functools.reduce(jnp.maximum, [jnp.max(s, axis=-1, keepdims=True) for s in logits], m_old)
        corr = jnp.exp(m_old - m_new)
        probs = [jnp.exp(s - m_new) for s in logits]
        l_ref[...] = l_ref[...] * corr + functools.reduce(
            lambda a, c: a + c, [jnp.sum(pr, axis=-1, keepdims=True) for pr in probs])
        acc_ref[...] = acc_ref[...] * corr + functools.reduce(
            lambda a, c: a + c, [fn(pr) for fn, pr in zip(pv_fns, probs)])
        m_ref[...] = m_new

    logits, pv_fns = [], []
    for j in range(npp):
        page = p * npp + j
        scores_row = s_ref[0, pl.ds(page, 1), :]
        kidx = page * PAGE_SIZE + lax.broadcasted_iota(jnp.int32, (1, PAGE_SIZE), 1)
        sel = _selected(scores_row, kidx, thr, jmax, take_all)
        sel_rows = _dot(jnp.broadcast_to(jnp.where(sel, 1.0, 0.0), (N_HEADS, PAGE_SIZE)).astype(BF16), repeat)
        s = _dot_nt(q, kp_refs[j][0].astype(BF16))
        logits.append(jnp.where(own_head & (sel_rows > 0.5), s, -jnp.inf))
        pv_fns.append(lambda pr, ref=vp_refs[j]: _dot(pr.astype(BF16), ref[0].astype(BF16)))
    fold(logits, pv_fns)

    @pl.when(p == pl.num_programs(1) - 1)
    def _():
        def per_head(ref):
            rows = [jnp.where(group == n, jnp.broadcast_to(ref[0, :, n * HEAD_DIM:(n + 1) * HEAD_DIM],
                                                           (N_HEADS, HEAD_DIM)), 0.0) for n in range(N_KV_HEADS)]
            return functools.reduce(lambda a, c: a + c, rows).astype(BF16).astype(F32)

        score_new = s_ref[0, pl.ds(n_pages, 1), 0:1]
        sel = _selected(score_new, jnp.full((1, 1), n_pages * PAGE_SIZE, jnp.int32), thr, jmax, take_all)
        s = jnp.sum(q.astype(F32) * per_head(knew_ref), axis=-1, keepdims=True)
        s = jnp.where(sel, s, -jnp.inf)
        vexp = per_head(vnew_ref)
        fold([s], [lambda pr: pr.astype(BF16).astype(F32) * vexp])
        o_ref[0] = acc_ref[...] / l_ref[...]


def _sample_attn(page_table, q3, scores3, thr, jmax, take_all, knew3, vnew3, cache_k, cache_v, layer):
    DB, n_pages = page_table.shape
    width = N_KV_HEADS * HEAD_DIM
    npp = _pages_per_step(n_pages, 32)
    bsel = lambda *shape: pl.BlockSpec((1,) + shape, lambda b, p, pt: (b,) + (0,) * len(shape))
    whole = pl.BlockSpec((DB, LANES), lambda b, p, pt: (0, 0))
    n_phys = cache_k.shape[1]
    rows = PAGE_SIZE * N_KV_HEADS
    cache_k = cache_k.reshape(-1, rows, HEAD_DIM)
    cache_v = cache_v.reshape(-1, rows, HEAD_DIM)
    pages = [pl.BlockSpec((1, rows, HEAD_DIM),
                          lambda b, p, pt, j=j: (layer * n_phys + pt[b, p * npp + j], 0, 0)) for j in range(npp)]
    return pl.pallas_call(
        functools.partial(_sample_attn_kernel, n_pages=n_pages),
        grid_spec=pltpu.PrefetchScalarGridSpec(
            num_scalar_prefetch=1,
            grid=(DB, n_pages // npp),
            in_specs=[bsel(N_HEADS, HEAD_DIM), bsel(n_pages + 1, PAGE_SIZE), whole, whole, whole,
                      bsel(1, width), bsel(1, width)] + pages + pages,
            out_specs=bsel(N_HEADS, HEAD_DIM),
            scratch_shapes=[pltpu.VMEM((N_HEADS, 1), F32), pltpu.VMEM((N_HEADS, 1), F32),
                            pltpu.VMEM((N_HEADS, HEAD_DIM), F32)],
        ),
        out_shape=jax.ShapeDtypeStruct((DB, N_HEADS, HEAD_DIM), F32),
        compiler_params=_cparams("parallel", "arbitrary"),
        name="sample_attn",
    )(page_table, q3, scores3, thr, jmax, take_all, knew3, vnew3, *([cache_k] * npp), *([cache_v] * npp))


CARRY = 8
MXU_DIM = 256
GDN_GROUP = MXU_DIM // CHUNK


def _spread(a, row_head, hg):
    return jnp.concatenate([jnp.where(row_head == i, a, 0.0) for i in range(hg)], axis=1)


def _gdn_prep_kernel(x_ref, w_ref, buf_ref, q_ref, k_ref, v_ref, conv_ref, xpad_ref):
    tt = x_ref.shape[1]
    t = pl.program_id(1)
    lo = CARRY - (CONV_W - 1)

    @pl.when(t == 0)
    def _():
        xpad_ref[lo:CARRY, :] = buf_ref[0]

    xpad_ref[CARRY:CARRY + tt, :] = x_ref[0]
    nh = GDN_HEADS
    for c in range(CONV_DIM // LANES):
        sl = slice(c * LANES, (c + 1) * LANES)
        y = w_ref[0:1, sl] * xpad_ref[lo:lo + tt, sl]
        for j in range(1, CONV_W):
            y = y + w_ref[j:j + 1, sl] * xpad_ref[lo + j:lo + j + tt, sl]
        y = _silu(y)
        if c < 2 * nh:
            y = y * lax.rsqrt(jnp.sum(y * y, axis=-1, keepdims=True) + NORM_EPS)
        if c < nh:
            q_ref[0, :, sl] = y * (GDN_DK ** -0.5)
        elif c < 2 * nh:
            k_ref[0, :, slice((c - nh) * LANES, (c - nh + 1) * LANES)] = y
        else:
            v_ref[0, :, slice((c - 2 * nh) * LANES, (c - 2 * nh + 1) * LANES)] = y
    last = xpad_ref[lo + tt:CARRY + tt, :]
    xpad_ref[lo:CARRY, :] = last

    @pl.when(t == pl.num_programs(1) - 1)
    def _():
        conv_ref[0] = last


def _gdn_prep(u_gdn3, conv_w, buf, tt):
    B, T, _ = u_gdn3.shape
    w = SIZES['qkv_b']
    hd = GDN_HEADS * GDN_DK
    out = jax.ShapeDtypeStruct((B, T, hd), F32)
    ospec = pl.BlockSpec((1, tt, hd), lambda b, t: (b, t, 0))
    return pl.pallas_call(
        _gdn_prep_kernel,
        grid=(B, T // tt),
        in_specs=[
            pl.BlockSpec((1, tt, w), lambda b, t: (b, t, 0)),
            pl.BlockSpec((CONV_W, w), lambda b, t: (0, 0)),
            pl.BlockSpec((1, CONV_W - 1, w), lambda b, t: (b, 0, 0)),
        ],
        out_specs=[ospec, ospec, ospec, pl.BlockSpec((1, CONV_W - 1, w), lambda b, t: (b, 0, 0))],
        out_shape=[out, out, out, jax.ShapeDtypeStruct((B, CONV_W - 1, w), F32)],
        scratch_shapes=[pltpu.VMEM((CARRY + tt, w), F32)],
        compiler_params=_cparams("parallel", "arbitrary"),
        name="gdn_prep",
    )(u_gdn3, conv_w, buf)


def _gdn_chunk_kernel(q_ref, k_ref, v_ref, z_ref, sm_ref, alog_ref, dtb_ref, gn_ref, s0_ref,
                      o_ref, sout_ref, state_ref, *, t_valid):
    C = q_ref.shape[1]
    c = pl.program_id(1)

    @pl.when(c == 0)
    def _():
        state_ref[...] = s0_ref[0]

    H = GDN_HEADS
    a0, b0 = SMALL_OFF['a_b'], SMALL_OFF['beta_b']
    live = (c * C + lax.broadcasted_iota(jnp.int32, (C, H), 0)) < t_valid
    xs = sm_ref[0, :, a0:a0 + H] + dtb_ref[...]
    softplus = jnp.maximum(xs, 0.0) + jnp.log1p(jnp.exp(-jnp.abs(xs)))
    g_all = jnp.where(live, -jnp.exp(alog_ref[...]) * softplus, 0.0)
    beta_all = jnp.where(live, jax.nn.sigmoid(sm_ref[0, :, b0:b0 + H]), 0.0)
    tri_f = (lax.broadcasted_iota(jnp.int32, (C, C), 0) >= lax.broadcasted_iota(jnp.int32, (C, C), 1)).astype(F32)
    eye_h = (lax.broadcasted_iota(jnp.int32, (H, H), 0) == lax.broadcasted_iota(jnp.int32, (H, H), 1)).astype(F32)
    gcum = _dot_hi(tri_f, g_all)
    gcum_t = _dot_nt_hi(eye_h, gcum)

    HG = GDN_GROUP
    R = HG * C
    ri = lax.broadcasted_iota(jnp.int32, (R, R), 0)
    ci = lax.broadcasted_iota(jnp.int32, (R, R), 1)
    same_head = (ri // C) == (ci // C)
    mask_incl = same_head & (ri >= ci)
    mask_strict = same_head & (ri > ci)
    eye = (ri == ci).astype(F32)
    row_head = lax.broadcasted_iota(jnp.int32, (R, 1), 0) // C
    row_head2 = jnp.concatenate([row_head, row_head], axis=0)

    groups = []
    for grp in range(H // HG):
        heads = range(grp * HG, (grp + 1) * HG)
        sls = [slice(h * GDN_DK, (h + 1) * GDN_DK) for h in heads]
        rows = lambda ref: jnp.concatenate([ref[0, :, sl] for sl in sls], axis=0)
        cols = lambda a: jnp.concatenate([a[:, h:h + 1] for h in heads], axis=0)
        q, k, v = rows(q_ref), rows(k_ref), rows(v_ref)
        beta = cols(beta_all)
        gcol = cols(gcum)
        grow = jnp.concatenate([gcum_t[h:h + 1, :] for h in heads], axis=1)
        glast = jnp.concatenate([jnp.broadcast_to(gcum[C - 1:C, h:h + 1], (C, 1)) for h in heads], axis=0)
        decay = jnp.where(mask_incl, jnp.exp(jnp.where(mask_incl, gcol - grow, 0.0)), 0.0)
        kb = k * beta
        kk_qk = _dot_nt(jnp.concatenate([kb, q], axis=0).astype(BF16), k.astype(BF16))
        nmat = jnp.where(mask_strict, kk_qk[:R] * decay, 0.0)
        groups.append(dict(
            heads=heads, sls=sls, nmat=nmat.astype(BF16), xinv=eye - nmat, qk=(kk_qk[R:] * decay).astype(BF16),
            rhs=jnp.concatenate([v * beta, kb * jnp.exp(gcol)], axis=1).astype(BF16),
            q_dec=q * jnp.exp(gcol), k_dec=k * jnp.exp(glast - gcol)))

    for _ in range(max(C - 1, 1).bit_length() - 1):
        for g in groups:
            g['resid'] = (eye - g['xinv']) - _dot(g['nmat'], g['xinv'].astype(BF16))
        for g in groups:
            g['xinv'] = g['xinv'] + _dot_bf16(g['xinv'], g['resid'])
    for g in groups:
        g['x'] = _dot(g['xinv'].astype(BF16), g['rhs'])
    for grp, g in enumerate(groups):
        u, w = g['x'][:, :GDN_DV], g['x'][:, GDN_DV:]
        s_stack = state_ref[grp * HG:(grp + 1) * HG].reshape(HG * GDN_DK, GDN_DV)
        w_q = jnp.concatenate([w, g['q_dec']], axis=0)
        ws_qs = _dot(_spread(w_q, row_head2, HG).astype(BF16), s_stack.astype(BF16))
        g['v_new'] = u - ws_qs[:R]
        g['o'] = ws_qs[R:]
    for g in groups:
        g['o'] = g['o'] + _dot(g['qk'], g['v_new'].astype(BF16))
        g['s_add'] = _dot(g['k_dec'].T.astype(BF16), _spread(g['v_new'], row_head, HG).astype(BF16))
    for g in groups:
        o = g['o']
        on = o * lax.rsqrt(jnp.mean(o * o, axis=-1, keepdims=True) + NORM_EPS) * gn_ref[...]
        for a, h in enumerate(g['heads']):
            state_ref[h] = (state_ref[h] * jnp.exp(gcum[C - 1:C, h:h + 1])
                            + g['s_add'][:, a * GDN_DV:(a + 1) * GDN_DV])
            o_ref[0, :, g['sls'][a]] = on[a * C:(a + 1) * C] * _silu(z_ref[0, :, g['sls'][a]])

    @pl.when(c == pl.num_programs(1) - 1)
    def _():
        sout_ref[0] = state_ref[...]


def _gdn_chunks(qn, kn, vv, u_gate3, small3, a_log, dt_bias, gn, s0, t_valid):
    B, Tp, hd = qn.shape
    nc = Tp // CHUNK
    blk = pl.BlockSpec((1, CHUNK, hd), lambda b, c: (b, c, 0))
    vec = lambda w: pl.BlockSpec((1, w), lambda b, c: (0, 0))
    st = pl.BlockSpec((1, GDN_HEADS, GDN_DK, GDN_DV), lambda b, c: (b, 0, 0, 0))
    return pl.pallas_call(
        functools.partial(_gdn_chunk_kernel, t_valid=t_valid),
        grid=(B, nc),
        in_specs=[blk, blk, blk,
                  pl.BlockSpec((1, CHUNK, hd), lambda b, c, o=GATE_OFF['z_b'] // hd: (b, c, o)),
                  pl.BlockSpec((1, CHUNK, LANES), lambda b, c: (b, c, 0)),
                  vec(GDN_HEADS), vec(GDN_HEADS), vec(GDN_DV), st],
        out_specs=[blk, st],
        out_shape=[jax.ShapeDtypeStruct((B, Tp, hd), F32),
                   jax.ShapeDtypeStruct((B, GDN_HEADS, GDN_DK, GDN_DV), F32)],
        scratch_shapes=[pltpu.VMEM((GDN_HEADS, GDN_DK, GDN_DV), F32)],
        compiler_params=_cparams("parallel", "arbitrary"),
        name="gdn_chunks",
    )(qn, kn, vv, u_gate3, small3, a_log, dt_bias, gn, s0)


def _merge_kernel(x_ref, oa_ref, ob_ref, ga_ref, gb_ref, wo_ref, g_ref, b_ref, o_ref):
    merged = jax.nn.sigmoid(ga_ref[...]) * oa_ref[...] + jax.nn.sigmoid(gb_ref[...]) * ob_ref[...]
    y = DN_ALPHA * x_ref[...] + _dot(merged.astype(BF16), wo_ref[...])
    o_ref[...] = _layer_norm(y, g_ref[...], b_ref[...])


def _merge_proj_ln(x, o_a, o_b, u_gate, w_o, g, b, tm):
    M, D = x.shape
    row = pl.BlockSpec((tm, D), lambda i: (i, 0))
    col = lambda name: pl.BlockSpec((tm, D), lambda i, o=GATE_OFF[name] // D: (i, o))
    vec = pl.BlockSpec((1, D), lambda i: (0, 0))
    return pl.pallas_call(
        _merge_kernel,
        grid=(M // tm,),
        in_specs=[row, row, row, col('gate_a'), col('gate_b'),
                  pl.BlockSpec((D, D), lambda i: (0, 0)), vec, vec],
        out_specs=row,
        out_shape=jax.ShapeDtypeStruct((M, D), F32),
        compiler_params=_cparams("parallel"),
        name="merge_proj_ln",
    )(x, o_a, o_b, u_gate, u_gate, w_o, g, b)


def _tiles(M):
    return (512, 256) if M % 512 == 0 else (M, M)


def _layer(x, B, T, pos, wts, conv_buf, ssm0, dsa_fn):
    M = B * T
    tm, te = _tiles(M)
    row = lambda a: a.reshape(1, -1)
    x1 = _ffn_ln(x, wts['ffn1_g'], wts['ffn1_u'], wts['ffn1_d'], row(wts['ln1_g']), row(wts['ln1_b']),
                 tm, 512)
    tp = 1024 if M % 1024 == 0 else tm
    u_attn = _proj(x1, wts['w_attn'], tp, 1024)
    u_gdn = _proj(x1, wts['w_gdn'], tp, 1024)
    u_gate = _proj(x1, wts['w_gate'], tp, 1024)
    u_small = _proj(x1, wts['w_small'], tp, LANES)

    pos_rows = pos if T > 1 else jnp.broadcast_to(pos, (M,))
    tabs = (_rope_tables(pos_rows, ROPE_DIM, HEAD_DIM) + _rope_tables(pos_rows, IDX_ROPE_DIM, IDX_DIM)
            + _rope_tables(pos_rows, IDX_ROPE_DIM, IDX_DIM, live_lanes=IDX_DIM))
    q_bf, k_rot, k_bf, v_bf, qi_bf, small_rot, small_bf = _rope_prep(u_attn, u_small, tabs, te)
    v_rows = u_attn[:, ATTN_OFF['v_a']:ATTN_OFF['v_a'] + SIZES['v_a']]
    o_a = dsa_fn(v_rows, q_bf, k_rot, k_bf, v_bf, qi_bf, small_rot, small_bf)

    qn, kn, vv, conv_new = _gdn_prep(u_gdn.reshape(B, T, -1), wts['conv_w'], conv_buf, min(T, 256))
    pad = (-T) % CHUNK
    pad3 = lambda a: jnp.pad(a, ((0, 0), (0, pad), (0, 0))) if pad else a
    o_b, ssm_new = _gdn_chunks(pad3(qn), pad3(kn), pad3(vv), pad3(u_gate.reshape(B, T, -1)),
                               pad3(u_small.reshape(B, T, LANES)),
                               row(wts['a_log']), row(wts['dt_bias']), row(wts['gdn_norm_g']), ssm0, T)
    o_b = o_b[:, :T].reshape(M, D_MODEL)

    x2 = _merge_proj_ln(x1, o_a, o_b, u_gate, wts['w_o'], row(wts['ln2_g']), row(wts['ln2_b']), te)
    y = _ffn_ln(x2, wts['ffn2_g'], wts['ffn2_u'], wts['ffn2_d'], row(wts['ln3_g']), row(wts['ln3_b']),
                tm, 512)
    ki_rows = small_rot[:, SMALL_OFF['k_idx']:SMALL_OFF['k_idx'] + IDX_DIM]
    return y, (k_rot, v_rows, ki_rows, ssm_new, conv_new)


def _split_w_in(w_in):
    offs = dict(zip([nm for nm, _ in IN_SPLITS], np.cumsum([0] + [n for _, n in IN_SPLITS])))

    def span(order):
        lo = offs[order[0]]
        hi = offs[order[-1]] + SIZES[order[-1]]
        assert hi - lo == sum(SIZES[nm] for nm in order)
        return w_in[:, lo:hi].astype(BF16)

    w_small = jnp.concatenate([w_in[:, offs[nm]:offs[nm] + SIZES[nm]] for nm in SMALL_ORDER], axis=1)
    w_small = jnp.pad(w_small, ((0, 0), (0, LANES - SMALL_USED))).astype(BF16)
    return span(ATTN_ORDER), span(GDN_ORDER), span(GATE_ORDER), w_small


def kernel(x_prompt, x_sample, cache_k, cache_v, cache_idx_k, state_ssm, state_conv, page_table, ffn1_w_gate, ffn1_w_up, ffn1_w_down, ln1_g, ln1_b, w_in, conv_w, a_log, dt_bias, gdn_norm_g, w_o, ln2_g, ln2_b, ffn2_w_gate, ffn2_w_up, ffn2_w_down, ln3_g, ln3_b):
    B, S, _ = x_prompt.shape
    DB, T, _ = x_sample.shape
    assert T == 1, "the sample path handles one new token per sequence"
    n_pages = page_table.shape[1]
    yp = x_prompt.reshape(B * S, D_MODEL)
    ys = x_sample.reshape(DB * T, D_MODEL)
    outs_p, outs_s = [], []
    for l in range(ffn1_w_gate.shape[0]):
        w_attn, w_gdn, w_gate, w_small = _split_w_in(w_in[l])
        wts = dict(
            ffn1_g=ffn1_w_gate[l].astype(BF16), ffn1_u=ffn1_w_up[l].astype(BF16), ffn1_d=ffn1_w_down[l].astype(BF16),
            ffn2_g=ffn2_w_gate[l].astype(BF16), ffn2_u=ffn2_w_up[l].astype(BF16), ffn2_d=ffn2_w_down[l].astype(BF16),
            ln1_g=ln1_g[l], ln1_b=ln1_b[l], ln2_g=ln2_g[l], ln2_b=ln2_b[l], ln3_g=ln3_g[l], ln3_b=ln3_b[l],
            w_attn=w_attn, w_gdn=w_gdn, w_gate=w_gate, w_small=w_small, w_o=w_o[l].astype(BF16), conv_w=conv_w[l],
            a_log=a_log[l], dt_bias=dt_bias[l], gdn_norm_g=gdn_norm_g[l],
        )

        def dsa_p(v_rows, q_bf, k_rot, k_bf, v_bf, qi_bf, small_rot, small_bf):
            return _dsa_prompt(q_bf, qi_bf, small_rot, k_bf, v_bf, small_bf, B, S, 128)

        def dsa_s(v_rows, q_bf, k_rot, k_bf, v_bf, qi_bf, small_rot, small_bf, l=l):
            w0 = SMALL_OFF['w_idx']
            w3 = small_rot[:, w0:w0 + IDX_HEADS].reshape(DB, IDX_HEADS, 1)
            qi3 = qi_bf.reshape(DB, IDX_HEADS, IDX_DIM)
            knew_i = small_bf[:, :IDX_DIM].reshape(DB, IDX_DIM, 1)
            width = N_KV_HEADS * HEAD_DIM
            past, new = _sample_scores(page_table, qi3, w3, knew_i, jnp.swapaxes(cache_idx_k, 2, 3), l)
            scores3 = jnp.concatenate([past, new], axis=1)
            n_keys = n_pages * PAGE_SIZE + T
            ktop = min(TOPK_MAX, n_keys // 4)
            thr, jmax, take_all = _sample_threshold(scores3.reshape(DB, -1), ktop, n_keys)
            o = _sample_attn(page_table, q_bf.reshape(DB, N_HEADS, HEAD_DIM), scores3, thr, jmax, take_all,
                             k_rot.reshape(DB, 1, width), v_rows.reshape(DB, 1, width), cache_k, cache_v, l)
            return o.reshape(DB, D_MODEL)

        conv0 = jnp.zeros((B, CONV_W - 1, CONV_DIM), F32)
        ssm_zero = jnp.zeros((B, GDN_HEADS, GDN_DK, GDN_DV), F32)
        yp, st_p = _layer(yp, B, S, jnp.arange(S, dtype=jnp.int32), wts, conv0, ssm_zero, dsa_p)
        past_len = n_pages * PAGE_SIZE
        ys, st_s = _layer(ys, DB, T, past_len + jnp.arange(T, dtype=jnp.int32), wts, state_conv[l], state_ssm[l], dsa_s)
        outs_p.append(st_p)
        outs_s.append(st_s)

    def stack(outs, nb, nt):
        d = len(outs)
        k, v, ki, ssm, conv = [a[0][None] if d == 1 else jnp.stack(a) for a in zip(*outs)]
        return (k.reshape(d, nb, nt, N_KV_HEADS, HEAD_DIM), v.reshape(d, nb, nt, N_KV_HEADS, HEAD_DIM),
                ki.reshape(d, nb, nt, IDX_DIM), ssm, conv)

    return (yp.reshape(B, S, D_MODEL), ys.reshape(DB, T, D_MODEL)) + stack(outs_p, B, S) + stack(outs_s, DB, T)
```

```python
import functools

import jax
import jax.numpy as jnp
import numpy as np
from jax import lax
from jax.experimental import pallas as pl
from jax.experimental.pallas import tpu as pltpu

D_MODEL = 2048
PAST_LEN = 16384
PAGE_SIZE = 128
HEAD_DIM = 128
N_HEADS = D_MODEL // HEAD_DIM
N_KV_HEADS = 4
GROUP = N_HEADS // N_KV_HEADS
ROPE_DIM = HEAD_DIM // 4
IDX_HEADS = 16
IDX_DIM = 64
IDX_ROPE_DIM = IDX_DIM // 4
TOPK_MAX = 256
ROPE_THETA = 500000.0
GDN_DK = 128
GDN_DV = 128
GDN_HEADS = D_MODEL // GDN_DV
CONV_W = 4
CONV_DIM = 2 * GDN_HEADS * GDN_DK + GDN_HEADS * GDN_DV
CHUNK = 64
D_FF = 5632
LN_EPS = 1e-5
NORM_EPS = 1e-6
DEPTH = 1
DN_ALPHA = (2 * DEPTH) ** 0.25

IN_SPLITS = (
    ('q_a', N_HEADS * HEAD_DIM), ('k_a', N_KV_HEADS * HEAD_DIM), ('v_a', N_KV_HEADS * HEAD_DIM),
    ('q_idx', IDX_HEADS * IDX_DIM), ('k_idx', IDX_DIM), ('w_idx', IDX_HEADS),
    ('qkv_b', CONV_DIM), ('a_b', GDN_HEADS), ('beta_b', GDN_HEADS), ('z_b', GDN_HEADS * GDN_DV),
    ('gate_a', D_MODEL), ('gate_b', D_MODEL),
)
ATTN_ORDER = ('q_a', 'k_a', 'v_a', 'q_idx')
GDN_ORDER = ('qkv_b',)
GATE_ORDER = ('z_b', 'gate_a', 'gate_b')
SMALL_ORDER = ('k_idx', 'w_idx', 'a_b', 'beta_b')
LANES = 128
VMEM_LIMIT = 56 * 1024 * 1024
NEG_BIG = -1e30
INT_MIN = -2 ** 31

F32 = jnp.float32
BF16 = jnp.bfloat16


def _offsets(order):
    sizes = dict(IN_SPLITS)
    offs, o = {}, 0
    for nm in order:
        offs[nm] = o
        o += sizes[nm]
    return offs, o


ATTN_OFF, ATTN_COLS = _offsets(ATTN_ORDER)
GATE_OFF, GATE_COLS = _offsets(GATE_ORDER)
SMALL_OFF, SMALL_USED = _offsets(SMALL_ORDER)
SIZES = dict(IN_SPLITS)


def _cparams(*sem):
    return pltpu.CompilerParams(dimension_semantics=sem, vmem_limit_bytes=VMEM_LIMIT)


def _dot(a, b):
    return jnp.dot(a, b, preferred_element_type=F32)


def _dot_nt(a, b):
    return lax.dot_general(a, b, (((1,), (1,)), ((), ())), preferred_element_type=F32)


def _dot_hi(a, b):
    return jnp.dot(a, b, preferred_element_type=F32, precision=lax.Precision.HIGHEST)


def _dot_nt_hi(a, b):
    return lax.dot_general(a, b, (((1,), (1,)), ((), ())), preferred_element_type=F32,
                           precision=lax.Precision.HIGHEST)


def _dot_tn_hi(a, b):
    return lax.dot_general(a, b, (((0,), (0,)), ((), ())), preferred_element_type=F32,
                           precision=lax.Precision.HIGHEST)


def _dot_bf16(a, b):
    return _dot(a.astype(BF16), b.astype(BF16))


def _silu(x):
    return x * jax.nn.sigmoid(x)


def _layer_norm(y, g, b):
    mu = jnp.mean(y, axis=-1, keepdims=True)
    d = y - mu
    var = jnp.mean(d * d, axis=-1, keepdims=True)
    return d * lax.rsqrt(var + LN_EPS) * g + b


def _ffn_ln_kernel(x_ref, wg_ref, wu_ref, wd_ref, g_ref, b_ref, o_ref, acc_ref, xb_ref):
    j = pl.program_id(1)

    @pl.when(j == 0)
    def _():
        acc_ref[...] = jnp.zeros_like(acc_ref)
        xb_ref[...] = x_ref[...].astype(BF16)

    xb = xb_ref[...]
    hg = _dot(xb, wg_ref[...])
    hu = _dot(xb, wu_ref[...])
    h = _silu(hg) * hu
    acc_ref[...] += _dot(h.astype(BF16), wd_ref[...])

    @pl.when(j == pl.num_programs(1) - 1)
    def _():
        y = DN_ALPHA * x_ref[...] + 0.5 * acc_ref[...]
        o_ref[...] = _layer_norm(y, g_ref[...], b_ref[...])


def _ffn_ln(x, wg, wu, wd, g, b, tm, tf):
    M, D = x.shape
    F = wg.shape[1]
    return pl.pallas_call(
        _ffn_ln_kernel,
        grid=(M // tm, F // tf),
        in_specs=[
            pl.BlockSpec((tm, D), lambda i, j: (i, 0)),
            pl.BlockSpec((D, tf), lambda i, j: (0, j)),
            pl.BlockSpec((D, tf), lambda i, j: (0, j)),
            pl.BlockSpec((tf, D), lambda i, j: (j, 0)),
            pl.BlockSpec((1, D), lambda i, j: (0, 0)),
            pl.BlockSpec((1, D), lambda i, j: (0, 0)),
        ],
        out_specs=pl.BlockSpec((tm, D), lambda i, j: (i, 0)),
        out_shape=jax.ShapeDtypeStruct((M, D), F32),
        scratch_shapes=[pltpu.VMEM((tm, D), F32), pltpu.VMEM((tm, D), BF16)],
        compiler_params=_cparams("parallel", "arbitrary"),
        name="ffn_ln",
    )(x, wg, wu, wd, g, b)


def _proj_kernel(x_ref, w_ref, o_ref, xb_ref):
    @pl.when(pl.program_id(1) == 0)
    def _():
        xb_ref[...] = x_ref[...].astype(BF16)

    o_ref[...] = _dot(xb_ref[...], w_ref[...])


def _proj(x, w, tm, tn):
    M, K = x.shape
    N = w.shape[1]
    return pl.pallas_call(
        _proj_kernel,
        grid=(M // tm, N // tn),
        in_specs=[
            pl.BlockSpec((tm, K), lambda i, j: (i, 0)),
            pl.BlockSpec((K, tn), lambda i, j: (0, j)),
        ],
        out_specs=pl.BlockSpec((tm, tn), lambda i, j: (i, j)),
        out_shape=jax.ShapeDtypeStruct((M, N), F32),
        scratch_shapes=[pltpu.VMEM((tm, K), BF16)],
        compiler_params=_cparams("parallel", "arbitrary"),
        name="in_proj",
    )(x, w)


def _rope_tables(pos, rot_dim, period, live_lanes=LANES):
    half = rot_dim // 2
    inv = ROPE_THETA ** (-jnp.arange(half, dtype=F32) / half)
    ang = pos.astype(F32)[:, None] * inv[None, :]
    cos, sin = jnp.cos(ang), jnp.sin(ang)
    lane = np.arange(LANES)
    lp = lane % period
    idx = lp % half
    live = (lp < rot_dim) & (lane < live_lanes)
    c = jnp.where(live[None, :], cos[:, idx], 1.0)
    s = jnp.where(live[None, :], jnp.where((lp < half)[None, :], -sin[:, idx], sin[:, idx]), 0.0)
    return c.astype(F32), s.astype(F32)


def _rope_tile(x, c, s, half, period):
    lane = lax.broadcasted_iota(jnp.int32, x.shape, 1)
    first = (lane & (period - 1)) < half
    partner = jnp.where(first, pltpu.roll(x, LANES - half, 1), pltpu.roll(x, half, 1))
    return x * c + partner * s


def _rope_kernel(q_ref, k_ref, v_ref, qi_ref, sm_ref, ca_ref, sa_ref, ci_ref, si_ref, cs_ref, ss_ref,
                 qo_ref, ko_ref, kb_ref, vb_ref, qio_ref, smo_ref, smb_ref):
    ca, sa = ca_ref[...], sa_ref[...]
    ci, si = ci_ref[...], si_ref[...]
    for h in range(N_HEADS):
        sl = slice(h * LANES, (h + 1) * LANES)
        qo_ref[:, sl] = (_rope_tile(q_ref[:, sl], ca, sa, ROPE_DIM // 2, HEAD_DIM) * (HEAD_DIM ** -0.5)).astype(BF16)
    for h in range(N_KV_HEADS):
        sl = slice(h * LANES, (h + 1) * LANES)
        kr = _rope_tile(k_ref[:, sl], ca, sa, ROPE_DIM // 2, HEAD_DIM)
        ko_ref[:, sl] = kr
        kb_ref[:, sl] = kr.astype(BF16)
    vb_ref[...] = v_ref[...].astype(BF16)
    for h in range(IDX_HEADS * IDX_DIM // LANES):
        sl = slice(h * LANES, (h + 1) * LANES)
        qio_ref[:, sl] = _rope_tile(qi_ref[:, sl], ci, si, IDX_ROPE_DIM // 2, IDX_DIM).astype(BF16)
    sm = _rope_tile(sm_ref[...], cs_ref[...], ss_ref[...], IDX_ROPE_DIM // 2, IDX_DIM)
    smo_ref[...] = sm
    smb_ref[...] = sm.astype(BF16)


def _rope_prep(u_attn, u_small, tabs, tm):
    M = u_attn.shape[0]
    tpos = tabs[0].shape[0]
    nt = tpos // tm

    def col(name):
        w = SIZES[name]
        return pl.BlockSpec((tm, w), lambda i, o=ATTN_OFF[name] // w: (i, o))

    tab_spec = pl.BlockSpec((tm, LANES), lambda i: (i % nt, 0))
    row = lambda w: pl.BlockSpec((tm, w), lambda i: (i, 0))
    return pl.pallas_call(
        _rope_kernel,
        grid=(M // tm,),
        in_specs=[col('q_a'), col('k_a'), col('v_a'), col('q_idx'), row(LANES)] + [tab_spec] * 6,
        out_specs=[row(SIZES['q_a']), row(SIZES['k_a']), row(SIZES['k_a']), row(SIZES['v_a']),
                   row(SIZES['q_idx']), row(LANES), row(LANES)],
        out_shape=[
            jax.ShapeDtypeStruct((M, SIZES['q_a']), BF16),
            jax.ShapeDtypeStruct((M, SIZES['k_a']), F32),
            jax.ShapeDtypeStruct((M, SIZES['k_a']), BF16),
            jax.ShapeDtypeStruct((M, SIZES['v_a']), BF16),
            jax.ShapeDtypeStruct((M, SIZES['q_idx']), BF16),
            jax.ShapeDtypeStruct((M, LANES), F32),
            jax.ShapeDtypeStruct((M, LANES), BF16),
        ],
        compiler_params=_cparams("parallel"),
        name="rope_prep",
    )(u_attn, u_attn, u_attn, u_attn, u_small, *tabs)


BISECT_UNROLL = 4


def _count(pred):
    return jnp.sum(jnp.where(pred, 1.0, 0.0), axis=-1, keepdims=True)


def _tie_index(score, kidx, thr):
    return jnp.where(score == thr, kidx, jnp.int32(2 ** 31 - 1))


def _topk_threshold(score, kidx, n_allowed, k, idx_bits):
    rows = score.shape[0]
    take_all = n_allowed <= k
    lo0 = jnp.min(jnp.where(score == -jnp.inf, jnp.inf, score), axis=-1, keepdims=True)
    hi0 = jnp.max(score, axis=-1, keepdims=True)
    lo0 = jnp.where(take_all, 0.0, lo0)
    hi0 = jnp.where(take_all, 0.0, hi0)

    def step(lo, hi, n_lo):
        mid = 0.5 * lo + 0.5 * hi
        n_mid = _count(score >= mid)
        ge = n_mid >= k
        return jnp.where(ge, mid, lo), jnp.where(ge, hi, mid), jnp.where(ge, n_mid, n_lo)

    def body(state):
        lo, hi, n_lo, _ = state
        for _ in range(BISECT_UNROLL):
            lo, hi, n_lo = step(lo, hi, n_lo)
        mid = 0.5 * lo + 0.5 * hi
        still_open = jnp.max(jnp.where((mid > lo) & (mid < hi) & (n_lo > k), 1.0, 0.0))
        return lo, hi, n_lo, still_open

    n_lo0 = jnp.where(take_all, float(k), n_allowed.astype(F32))
    lo, hi, _, _ = lax.while_loop(lambda state: state[3] > 0.5, body, (lo0, hi0, n_lo0, jnp.float32(1.0)))
    thr = jnp.where(_count(score >= hi) >= k, hi, lo)
    tie = _tie_index(score, kidx, thr)

    def tie_search():
        need = k - _count(score > thr)

        def ibody(t, j):
            cand = j + jnp.left_shift(jnp.int32(1), idx_bits - 1 - t)
            return jnp.where(_count(tie < cand) < need, cand, j)

        return lax.fori_loop(0, idx_bits, ibody, jnp.zeros((rows, 1), jnp.int32))

    repeated = jnp.max(jnp.where(take_all, 0.0, _count(score == thr))) > 1.5
    jmax = lax.cond(repeated, tie_search, lambda: jnp.full((rows, 1), 2 ** 31 - 2, jnp.int32))
    return thr, jmax, take_all


def _selected(score, kidx, thr, jmax, take_all):
    return take_all | (score > thr) | (_tie_index(score, kidx, thr) <= jmax)


def _dsa_prompt_kernel(q_ref, qi_ref, sm_ref, k_ref, v_ref, kis_ref, o_ref, *, ktop, key_step):
    tq = q_ref.shape[0]
    S = k_ref.shape[0]
    i = pl.program_id(1)
    w0 = SMALL_OFF['w_idx']

    def attend(L):
        ki = kis_ref[:L, :IDX_DIM]
        w = sm_ref[:, w0:w0 + IDX_HEADS] * (IDX_HEADS ** -0.5 * IDX_DIM ** -0.5)
        score = jnp.zeros((tq, L), F32)
        for h in range(IDX_HEADS):
            s = _dot_nt(qi_ref[:, h * IDX_DIM:(h + 1) * IDX_DIM], ki)
            score = score + w[:, h:h + 1] * jnp.maximum(s, 0.0)
        qpos = i * tq + lax.broadcasted_iota(jnp.int32, (tq, 1), 0)
        kidx = lax.broadcasted_iota(jnp.int32, (tq, L), 1)
        allowed = kidx <= qpos
        score = jnp.where(allowed, score, -jnp.inf)
        thr, jmax, take_all = _topk_threshold(score, kidx, qpos + 1, ktop, int(L - 1).bit_length())
        bias = jnp.where(_selected(score, kidx, thr, jmax, take_all) & allowed, 0.0, -jnp.inf)
        for h in range(N_HEADS):
            n = h // GROUP
            sl = slice(h * HEAD_DIM, (h + 1) * HEAD_DIM)
            kv = slice(n * HEAD_DIM, (n + 1) * HEAD_DIM)
            s = _dot_nt(q_ref[:, sl], k_ref[:L, kv]) + bias
            m = jnp.max(s, axis=-1, keepdims=True)
            p = jnp.exp(s - m)
            l = jnp.sum(p, axis=-1, keepdims=True)
            o_ref[:, sl] = _dot(p.astype(BF16), v_ref[:L, kv]) / l

    level = ((i + 1) * tq - 1) // key_step
    for lv in range(S // key_step):
        pl.when(level == lv)(functools.partial(attend, (lv + 1) * key_step))


def _dsa_prompt(q_bf, qi_bf, small_rot, k_bf, v_bf, small_bf, B, S, tq):
    ktop = min(TOPK_MAX, S // 4)
    nq = S // tq
    key_step = min(S, 512)
    row = lambda w: pl.BlockSpec((tq, w), lambda b, i: (b * nq + i, 0))
    full = lambda w: pl.BlockSpec((S, w), lambda b, i: (b, 0))
    return pl.pallas_call(
        functools.partial(_dsa_prompt_kernel, ktop=ktop, key_step=key_step),
        grid=(B, nq),
        in_specs=[row(q_bf.shape[1]), row(qi_bf.shape[1]), row(LANES),
                  full(k_bf.shape[1]), full(v_bf.shape[1]), full(LANES)],
        out_specs=row(q_bf.shape[1]),
        out_shape=jax.ShapeDtypeStruct((B * S, q_bf.shape[1]), F32),
        compiler_params=_cparams("parallel", "arbitrary"),
        name="dsa_prompt",
    )(q_bf, qi_bf, small_rot, k_bf, v_bf, small_bf)


def _idx_score_rows(qi, w, kpage_t):
    s = _dot(qi, kpage_t)
    return jnp.sum(w * jnp.maximum(s, 0.0), axis=0, keepdims=True)


def _sample_scores_kernel(pt_ref, qi_ref, w_ref, knew_ref, *refs):
    page_refs, (o_ref, onew_ref) = refs[:-2], refs[-2:]
    npp = len(page_refs)
    p = pl.program_id(1)
    qi = qi_ref[0]
    w = w_ref[0] * (IDX_HEADS ** -0.5 * IDX_DIM ** -0.5)
    for j, page_ref in enumerate(page_refs):
        o_ref[0, pl.ds(p * npp + j, 1), :] = _idx_score_rows(qi, w, page_ref[0, 0].astype(BF16))

    @pl.when(p == 0)
    def _():
        kn = jnp.broadcast_to(knew_ref[0], (IDX_DIM, PAGE_SIZE))
        sc = _idx_score_rows(qi, w, kn)
        lane = lax.broadcasted_iota(jnp.int32, (1, PAGE_SIZE), 1)
        onew_ref[0] = jnp.where(lane == 0, sc, -jnp.inf)


def _page_specs(page_shape, npp, layer):
    zeros = (0,) * len(page_shape)
    return [pl.BlockSpec((1, 1) + page_shape, lambda b, p, pt, j=j: (layer, pt[b, p * npp + j]) + zeros)
            for j in range(npp)]


def _pages_per_step(n_pages, cap):
    npp = min(cap, n_pages)
    while n_pages % npp:
        npp -= 1
    return npp


def _sample_scores(page_table, qi3, w3, knew3, cache_ki, layer):
    DB, n_pages = page_table.shape
    npp = _pages_per_step(n_pages, 16)
    return pl.pallas_call(
        _sample_scores_kernel,
        grid_spec=pltpu.PrefetchScalarGridSpec(
            num_scalar_prefetch=1,
            grid=(DB, n_pages // npp),
            in_specs=[
                pl.BlockSpec((1, IDX_HEADS, IDX_DIM), lambda b, p, pt: (b, 0, 0)),
                pl.BlockSpec((1, IDX_HEADS, 1), lambda b, p, pt: (b, 0, 0)),
                pl.BlockSpec((1, IDX_DIM, 1), lambda b, p, pt: (b, 0, 0)),
            ] + _page_specs((IDX_DIM, PAGE_SIZE), npp, layer),
            out_specs=[
                pl.BlockSpec((1, n_pages, PAGE_SIZE), lambda b, p, pt: (b, 0, 0)),
                pl.BlockSpec((1, 1, PAGE_SIZE), lambda b, p, pt: (b, 0, 0)),
            ],
        ),
        out_shape=[jax.ShapeDtypeStruct((DB, n_pages, PAGE_SIZE), F32),
                   jax.ShapeDtypeStruct((DB, 1, PAGE_SIZE), F32)],
        compiler_params=_cparams("parallel", "arbitrary"),
        name="sample_scores",
    )(page_table, qi3, w3, knew3, *([cache_ki] * npp))


def _sample_thr_kernel(s_ref, thr_ref, jmax_ref, all_ref, *, ktop, n_valid, idx_bits):
    kidx = lax.broadcasted_iota(jnp.int32, s_ref.shape, 1)
    score = jnp.where(kidx < n_valid, s_ref[...], -jnp.inf)
    n_allowed = jnp.full((s_ref.shape[0], 1), n_valid, jnp.int32)
    thr, jmax, take_all = _topk_threshold(score, kidx, n_allowed, ktop, idx_bits)
    thr_ref[...] = jnp.broadcast_to(thr, thr_ref.shape)
    jmax_ref[...] = jnp.broadcast_to(jmax, jmax_ref.shape)
    all_ref[...] = jnp.broadcast_to(take_all.astype(jnp.int32), all_ref.shape)


def _sample_threshold(scores, ktop, n_valid):
    DB, L = scores.shape
    out = lambda dt: jax.ShapeDtypeStruct((DB, LANES), dt)
    return pl.pallas_call(
        functools.partial(_sample_thr_kernel, ktop=ktop, n_valid=n_valid, idx_bits=int(L - 1).bit_length()),
        out_shape=[out(F32), out(jnp.int32), out(jnp.int32)],
        compiler_params=pltpu.CompilerParams(vmem_limit_bytes=VMEM_LIMIT),
        name="sample_threshold",
    )(scores)


def _sample_attn_kernel(pt_ref, q_ref, s_ref, thr_ref, jmax_ref, all_ref, knew_ref, vnew_ref, *refs, n_pages):
    npp = (len(refs) - 4) // 2
    kp_refs, vp_refs = refs[:npp], refs[npp:2 * npp]
    o_ref, m_ref, l_ref, acc_ref = refs[2 * npp:]
    b = pl.program_id(0)
    p = pl.program_id(1)

    @pl.when(p == 0)
    def _():
        m_ref[...] = jnp.full_like(m_ref, NEG_BIG)
        l_ref[...] = jnp.zeros_like(l_ref)
        acc_ref[...] = jnp.zeros_like(acc_ref)

    thr = thr_ref[pl.ds(b, 1), 0:1]
    jmax = jmax_ref[pl.ds(b, 1), 0:1]
    take_all = all_ref[pl.ds(b, 1), 0:1] > 0

    q = q_ref[0]
    group = lax.broadcasted_iota(jnp.int32, (N_HEADS, 1), 0) // GROUP
    X = PAGE_SIZE * N_KV_HEADS
    own_head = (lax.broadcasted_iota(jnp.int32, (N_HEADS, X), 1) % N_KV_HEADS) == group
    repeat = (lax.broadcasted_iota(jnp.int32, (PAGE_SIZE, X), 1) // N_KV_HEADS
              == lax.broadcasted_iota(jnp.int32, (PAGE_SIZE, X), 0)).astype(BF16)

    def fold(logits, pv_fns):
        m_old = m_ref[...]
        m_new = functools.reduce(jnp.maximum, [jnp.max(s, axis=-1, keepdims=True) for s in logits], m_old)
        corr = jnp.exp(m_old - m_new)
        probs = [jnp.exp(s - m_new) for s in logits]
        l_ref[...] = l_ref[...] * corr + functools.reduce(
            lambda a, c: a + c, [jnp.sum(pr, axis=-1, keepdims=True) for pr in probs])
        acc_ref[...] = acc_ref[...] * corr + functools.reduce(
            lambda a, c: a + c, [fn(pr) for fn, pr in zip(pv_fns, probs)])
        m_ref[...] = m_new

    logits, pv_fns = [], []
    for j in range(npp):
        page = p * npp + j
        scores_row = s_ref[0, pl.ds(page, 1), :]
        kidx = page * PAGE_SIZE + lax.broadcasted_iota(jnp.int32, (1, PAGE_SIZE), 1)
        sel = _selected(scores_row, kidx, thr, jmax, take_all)
        sel_rows = _dot(jnp.broadcast_to(jnp.where(sel, 1.0, 0.0), (N_HEADS, PAGE_SIZE)).astype(BF16), repeat)
        s = _dot_nt(q, kp_refs[j][0].astype(BF16))
        logits.append(jnp.where(own_head & (sel_rows > 0.5), s, -jnp.inf))
        pv_fns.append(lambda pr, ref=vp_refs[j]: _dot(pr.astype(BF16), ref[0].astype(BF16)))
    fold(logits, pv_fns)

    @pl.when(p == pl.num_programs(1) - 1)
    def _():
        def per_head(ref):
            rows = [jnp.where(group == n, jnp.broadcast_to(ref[0, :, n * HEAD_DIM:(n + 1) * HEAD_DIM],
                                                           (N_HEADS, HEAD_DIM)), 0.0) for n in range(N_KV_HEADS)]
            return functools.reduce(lambda a, c: a + c, rows).astype(BF16).astype(F32)

        score_new = s_ref[0, pl.ds(n_pages, 1), 0:1]
        sel = _selected(score_new, jnp.full((1, 1), n_pages * PAGE_SIZE, jnp.int32), thr, jmax, take_all)
        s = jnp.sum(q.astype(F32) * per_head(knew_ref), axis=-1, keepdims=True)
        s = jnp.where(sel, s, -jnp.inf)
        vexp = per_head(vnew_ref)
        fold([s], [lambda pr: pr.astype(BF16).astype(F32) * vexp])
        o_ref[0] = acc_ref[...] / l_ref[...]


def _sample_attn(page_table, q3, scores3, thr, jmax, take_all, knew3, vnew3, cache_k, cache_v, layer):
    DB, n_pages = page_table.shape
    width = N_KV_HEADS * HEAD_DIM
    npp = _pages_per_step(n_pages, 32)
    bsel = lambda *shape: pl.BlockSpec((1,) + shape, lambda b, p, pt: (b,) + (0,) * len(shape))
    whole = pl.BlockSpec((DB, LANES), lambda b, p, pt: (0, 0))
    n_phys = cache_k.shape[1]
    rows = PAGE_SIZE * N_KV_HEADS
    cache_k = cache_k.reshape(-1, rows, HEAD_DIM)
    cache_v = cache_v.reshape(-1, rows, HEAD_DIM)
    pages = [pl.BlockSpec((1, rows, HEAD_DIM),
                          lambda b, p, pt, j=j: (layer * n_phys + pt[b, p * npp + j], 0, 0)) for j in range(npp)]
    return pl.pallas_call(
        functools.partial(_sample_attn_kernel, n_pages=n_pages),
        grid_spec=pltpu.PrefetchScalarGridSpec(
            num_scalar_prefetch=1,
            grid=(DB, n_pages // npp),
            in_specs=[bsel(N_HEADS, HEAD_DIM), bsel(n_pages + 1, PAGE_SIZE), whole, whole, whole,
                      bsel(1, width), bsel(1, width)] + pages + pages,
            out_specs=bsel(N_HEADS, HEAD_DIM),
            scratch_shapes=[pltpu.VMEM((N_HEADS, 1), F32), pltpu.VMEM((N_HEADS, 1), F32),
                            pltpu.VMEM((N_HEADS, HEAD_DIM), F32)],
        ),
        out_shape=jax.ShapeDtypeStruct((DB, N_HEADS, HEAD_DIM), F32),
        compiler_params=_cparams("parallel", "arbitrary"),
        name="sample_attn",
    )(page_table, q3, scores3, thr, jmax, take_all, knew3, vnew3, *([cache_k] * npp), *([cache_v] * npp))


CARRY = 8
MXU_DIM = 256
GDN_GROUP = MXU_DIM // CHUNK


def _spread(a, row_head, hg):
    return jnp.concatenate([jnp.where(row_head == i, a, 0.0) for i in range(hg)], axis=1)


def _gdn_prep_kernel(x_ref, w_ref, buf_ref, q_ref, k_ref, v_ref, conv_ref, xpad_ref):
    tt = x_ref.shape[1]
    t = pl.program_id(1)
    lo = CARRY - (CONV_W - 1)

    @pl.when(t == 0)
    def _():
        xpad_ref[lo:CARRY, :] = buf_ref[0]

    xpad_ref[CARRY:CARRY + tt, :] = x_ref[0]
    nh = GDN_HEADS
    for c in range(CONV_DIM // LANES):
        sl = slice(c * LANES, (c + 1) * LANES)
        y = w_ref[0:1, sl] * xpad_ref[lo:lo + tt, sl]
        for j in range(1, CONV_W):
            y = y + w_ref[j:j + 1, sl] * xpad_ref[lo + j:lo + j + tt, sl]
        y = _silu(y)
        if c < 2 * nh:
            y = y * lax.rsqrt(jnp.sum(y * y, axis=-1, keepdims=True) + NORM_EPS)
        if c < nh:
            q_ref[0, :, sl] = y * (GDN_DK ** -0.5)
        elif c < 2 * nh:
            k_ref[0, :, slice((c - nh) * LANES, (c - nh + 1) * LANES)] = y
        else:
            v_ref[0, :, slice((c - 2 * nh) * LANES, (c - 2 * nh + 1) * LANES)] = y
    last = xpad_ref[lo + tt:CARRY + tt, :]
    xpad_ref[lo:CARRY, :] = last

    @pl.when(t == pl.num_programs(1) - 1)
    def _():
        conv_ref[0] = last


def _gdn_prep(u_gdn3, conv_w, buf, tt):
    B, T, _ = u_gdn3.shape
    w = SIZES['qkv_b']
    hd = GDN_HEADS * GDN_DK
    out = jax.ShapeDtypeStruct((B, T, hd), F32)
    ospec = pl.BlockSpec((1, tt, hd), lambda b, t: (b, t, 0))
    return pl.pallas_call(
        _gdn_prep_kernel,
        grid=(B, T // tt),
        in_specs=[
            pl.BlockSpec((1, tt, w), lambda b, t: (b, t, 0)),
            pl.BlockSpec((CONV_W, w), lambda b, t: (0, 0)),
            pl.BlockSpec((1, CONV_W - 1, w), lambda b, t: (b, 0, 0)),
        ],
        out_specs=[ospec, ospec, ospec, pl.BlockSpec((1, CONV_W - 1, w), lambda b, t: (b, 0, 0))],
        out_shape=[out, out, out, jax.ShapeDtypeStruct((B, CONV_W - 1, w), F32)],
        scratch_shapes=[pltpu.VMEM((CARRY + tt, w), F32)],
        compiler_params=_cparams("parallel", "arbitrary"),
        name="gdn_prep",
    )(u_gdn3, conv_w, buf)


def _gdn_chunk_kernel(q_ref, k_ref, v_ref, z_ref, sm_ref, alog_ref, dtb_ref, gn_ref, s0_ref,
                      o_ref, sout_ref, state_ref, *, t_valid):
    C = q_ref.shape[1]
    c = pl.program_id(1)

    @pl.when(c == 0)
    def _():
        state_ref[...] = s0_ref[0]

    H = GDN_HEADS
    a0, b0 = SMALL_OFF['a_b'], SMALL_OFF['beta_b']
    live = (c * C + lax.broadcasted_iota(jnp.int32, (C, H), 0)) < t_valid
    xs = sm_ref[0, :, a0:a0 + H] + dtb_ref[...]
    softplus = jnp.maximum(xs, 0.0) + jnp.log1p(jnp.exp(-jnp.abs(xs)))
    g_all = jnp.where(live, -jnp.exp(alog_ref[...]) * softplus, 0.0)
    beta_all = jnp.where(live, jax.nn.sigmoid(sm_ref[0, :, b0:b0 + H]), 0.0)
    tri_f = (lax.broadcasted_iota(jnp.int32, (C, C), 0) >= lax.broadcasted_iota(jnp.int32, (C, C), 1)).astype(F32)
    eye_h = (lax.broadcasted_iota(jnp.int32, (H, H), 0) == lax.broadcasted_iota(jnp.int32, (H, H), 1)).astype(F32)
    gcum = _dot_hi(tri_f, g_all)
    gcum_t = _dot_nt_hi(eye_h, gcum)

    HG = GDN_GROUP
    R = HG * C
    ri = lax.broadcasted_iota(jnp.int32, (R, R), 0)
    ci = lax.broadcasted_iota(jnp.int32, (R, R), 1)
    same_head = (ri // C) == (ci // C)
    mask_incl = same_head & (ri >= ci)
    mask_strict = same_head & (ri > ci)
    eye = (ri == ci).astype(F32)
    row_head = lax.broadcasted_iota(jnp.int32, (R, 1), 0) // C
    row_head2 = jnp.concatenate([row_head, row_head], axis=0)

    groups = []
    for grp in range(H // HG):
        heads = range(grp * HG, (grp + 1) * HG)
        sls = [slice(h * GDN_DK, (h + 1) * GDN_DK) for h in heads]
        rows = lambda ref: jnp.concatenate([ref[0, :, sl] for sl in sls], axis=0)
        cols = lambda a: jnp.concatenate([a[:, h:h + 1] for h in heads], axis=0)
        q, k, v = rows(q_ref), rows(k_ref), rows(v_ref)
        beta = cols(beta_all)
        gcol = cols(gcum)
        grow = jnp.concatenate([gcum_t[h:h + 1, :] for h in heads], axis=1)
        glast = jnp.concatenate([jnp.broadcast_to(gcum[C - 1:C, h:h + 1], (C, 1)) for h in heads], axis=0)
        decay = jnp.where(mask_incl, jnp.exp(jnp.where(mask_incl, gcol - grow, 0.0)), 0.0)
        kb = k * beta
        kk_qk = _dot_nt(jnp.concatenate([kb, q], axis=0).astype(BF16), k.astype(BF16))
        nmat = jnp.where(mask_strict, kk_qk[:R] * decay, 0.0)
        groups.append(dict(
            heads=heads, sls=sls, nmat=nmat.astype(BF16), xinv=eye - nmat, qk=(kk_qk[R:] * decay).astype(BF16),
            rhs=jnp.concatenate([v * beta, kb * jnp.exp(gcol)], axis=1).astype(BF16),
            q_dec=q * jnp.exp(gcol), k_dec=k * jnp.exp(glast - gcol)))

    for _ in range(max(C - 1, 1).bit_length() - 1):
        for g in groups:
            g['resid'] = (eye - g['xinv']) - _dot(g['nmat'], g['xinv'].astype(BF16))
        for g in groups:
            g['xinv'] = g['xinv'] + _dot_bf16(g['xinv'], g['resid'])
    for g in groups:
        g['x'] = _dot(g['xinv'].astype(BF16), g['rhs'])
    for grp, g in enumerate(groups):
        u, w = g['x'][:, :GDN_DV], g['x'][:, GDN_DV:]
        s_stack = state_ref[grp * HG:(grp + 1) * HG].reshape(HG * GDN_DK, GDN_DV)
        w_q = jnp.concatenate([w, g['q_dec']], axis=0)
        ws_qs = _dot(_spread(w_q, row_head2, HG).astype(BF16), s_stack.astype(BF16))
        g['v_new'] = u - ws_qs[:R]
        g['o'] = ws_qs[R:]
    for g in groups:
        g['o'] = g['o'] + _dot(g['qk'], g['v_new'].astype(BF16))
        g['s_add'] = _dot(g['k_dec'].T.astype(BF16), _spread(g['v_new'], row_head, HG).astype(BF16))
    for g in groups:
        o = g['o']
        on = o * lax.rsqrt(jnp.mean(o * o, axis=-1, keepdims=True) + NORM_EPS) * gn_ref[...]
        for a, h in enumerate(g['heads']):
            state_ref[h] = (state_ref[h] * jnp.exp(gcum[C - 1:C, h:h + 1])
                            + g['s_add'][:, a * GDN_DV:(a + 1) * GDN_DV])
            o_ref[0, :, g['sls'][a]] = on[a * C:(a + 1) * C] * _silu(z_ref[0, :, g['sls'][a]])

    @pl.when(c == pl.num_programs(1) - 1)
    def _():
        sout_ref[0] = state_ref[...]


def _gdn_chunks(qn, kn, vv, u_gate3, small3, a_log, dt_bias, gn, s0, t_valid):
    B, Tp, hd = qn.shape
    nc = Tp // CHUNK
    blk = pl.BlockSpec((1, CHUNK, hd), lambda b, c: (b, c, 0))
    vec = lambda w: pl.BlockSpec((1, w), lambda b, c: (0, 0))
    st = pl.BlockSpec((1, GDN_HEADS, GDN_DK, GDN_DV), lambda b, c: (b, 0, 0, 0))
    return pl.pallas_call(
        functools.partial(_gdn_chunk_kernel, t_valid=t_valid),
        grid=(B, nc),
        in_specs=[blk, blk, blk,
                  pl.BlockSpec((1, CHUNK, hd), lambda b, c, o=GATE_OFF['z_b'] // hd: (b, c, o)),
                  pl.BlockSpec((1, CHUNK, LANES), lambda b, c: (b, c, 0)),
                  vec(GDN_HEADS), vec(GDN_HEADS), vec(GDN_DV), st],
        out_specs=[blk, st],
        out_shape=[jax.ShapeDtypeStruct((B, Tp, hd), F32),
                   jax.ShapeDtypeStruct((B, GDN_HEADS, GDN_DK, GDN_DV), F32)],
        scratch_shapes=[pltpu.VMEM((GDN_HEADS, GDN_DK, GDN_DV), F32)],
        compiler_params=_cparams("parallel", "arbitrary"),
        name="gdn_chunks",
    )(qn, kn, vv, u_gate3, small3, a_log, dt_bias, gn, s0)


def _merge_kernel(x_ref, oa_ref, ob_ref, ga_ref, gb_ref, wo_ref, g_ref, b_ref, o_ref):
    merged = jax.nn.sigmoid(ga_ref[...]) * oa_ref[...] + jax.nn.sigmoid(gb_ref[...]) * ob_ref[...]
    y = DN_ALPHA * x_ref[...] + _dot(merged.astype(BF16), wo_ref[...])
    o_ref[...] = _layer_norm(y, g_ref[...], b_ref[...])


def _merge_proj_ln(x, o_a, o_b, u_gate, w_o, g, b, tm):
    M, D = x.shape
    row = pl.BlockSpec((tm, D), lambda i: (i, 0))
    col = lambda name: pl.BlockSpec((tm, D), lambda i, o=GATE_OFF[name] // D: (i, o))
    vec = pl.BlockSpec((1, D), lambda i: (0, 0))
    return pl.pallas_call(
        _merge_kernel,
        grid=(M // tm,),
        in_specs=[row, row, row, col('gate_a'), col('gate_b'),
                  pl.BlockSpec((D, D), lambda i: (0, 0)), vec, vec],
        out_specs=row,
        out_shape=jax.ShapeDtypeStruct((M, D), F32),
        compiler_params=_cparams("parallel"),
        name="merge_proj_ln",
    )(x, o_a, o_b, u_gate, u_gate, w_o, g, b)


def _tiles(M):
    return (512, 256) if M % 512 == 0 else (M, M)


def _layer(x, B, T, pos, wts, conv_buf, ssm0, dsa_fn):
    M = B * T
    tm, te = _tiles(M)
    row = lambda a: a.reshape(1, -1)
    x1 = _ffn_ln(x, wts['ffn1_g'], wts['ffn1_u'], wts['ffn1_d'], row(wts['ln1_g']), row(wts['ln1_b']),
                 tm, 512)
    tp = 1024 if M % 1024 == 0 else tm
    u_attn = _proj(x1, wts['w_attn'], tp, 1024)
    u_gdn = _proj(x1, wts['w_gdn'], tp, 1024)
    u_gate = _proj(x1, wts['w_gate'], tp, 1024)
    u_small = _proj(x1, wts['w_small'], tp, LANES)

    pos_rows = pos if T > 1 else jnp.broadcast_to(pos, (M,))
    tabs = (_rope_tables(pos_rows, ROPE_DIM, HEAD_DIM) + _rope_tables(pos_rows, IDX_ROPE_DIM, IDX_DIM)
            + _rope_tables(pos_rows, IDX_ROPE_DIM, IDX_DIM, live_lanes=IDX_DIM))
    q_bf, k_rot, k_bf, v_bf, qi_bf, small_rot, small_bf = _rope_prep(u_attn, u_small, tabs, te)
    v_rows = u_attn[:, ATTN_OFF['v_a']:ATTN_OFF['v_a'] + SIZES['v_a']]
    o_a = dsa_fn(v_rows, q_bf, k_rot, k_bf, v_bf, qi_bf, small_rot, small_bf)

    qn, kn, vv, conv_new = _gdn_prep(u_gdn.reshape(B, T, -1), wts['conv_w'], conv_buf, min(T, 256))
    pad = (-T) % CHUNK
    pad3 = lambda a: jnp.pad(a, ((0, 0), (0, pad), (0, 0))) if pad else a
    o_b, ssm_new = _gdn_chunks(pad3(qn), pad3(kn), pad3(vv), pad3(u_gate.reshape(B, T, -1)),
                               pad3(u_small.reshape(B, T, LANES)),
                               row(wts['a_log']), row(wts['dt_bias']), row(wts['gdn_norm_g']), ssm0, T)
    o_b = o_b[:, :T].reshape(M, D_MODEL)

    x2 = _merge_proj_ln(x1, o_a, o_b, u_gate, wts['w_o'], row(wts['ln2_g']), row(wts['ln2_b']), te)
    y = _ffn_ln(x2, wts['ffn2_g'], wts['ffn2_u'], wts['ffn2_d'], row(wts['ln3_g']), row(wts['ln3_b']),
                tm, 512)
    ki_rows = small_rot[:, SMALL_OFF['k_idx']:SMALL_OFF['k_idx'] + IDX_DIM]
    return y, (k_rot, v_rows, ki_rows, ssm_new, conv_new)


def _split_w_in(w_in):
    offs = dict(zip([nm for nm, _ in IN_SPLITS], np.cumsum([0] + [n for _, n in IN_SPLITS])))

    def span(order):
        lo = offs[order[0]]
        hi = offs[order[-1]] + SIZES[order[-1]]
        assert hi - lo == sum(SIZES[nm] for nm in order)
        return w_in[:, lo:hi].astype(BF16)

    w_small = jnp.concatenate([w_in[:, offs[nm]:offs[nm] + SIZES[nm]] for nm in SMALL_ORDER], axis=1)
    w_small = jnp.pad(w_small, ((0, 0), (0, LANES - SMALL_USED))).astype(BF16)
    return span(ATTN_ORDER), span(GDN_ORDER), span(GATE_ORDER), w_small


def kernel(x_prompt, x_sample, cache_k, cache_v, cache_idx_k, state_ssm, state_conv, page_table, ffn1_w_gate, ffn1_w_up, ffn1_w_down, ln1_g, ln1_b, w_in, conv_w, a_log, dt_bias, gdn_norm_g, w_o, ln2_g, ln2_b, ffn2_w_gate, ffn2_w_up, ffn2_w_down, ln3_g, ln3_b):
    B, S, _ = x_prompt.shape
    DB, T, _ = x_sample.shape
    assert T == 1, "the sample path handles one new token per sequence"
    n_pages = page_table.shape[1]
    yp = x_prompt.reshape(B * S, D_MODEL)
    ys = x_sample.reshape(DB * T, D_MODEL)
    outs_p, outs_s = [], []
    for l in range(ffn1_w_gate.shape[0]):
        w_attn, w_gdn, w_gate, w_small = _split_w_in(w_in[l])
        wts = dict(
            ffn1_g=ffn1_w_gate[l].astype(BF16), ffn1_u=ffn1_w_up[l].astype(BF16), ffn1_d=ffn1_w_down[l].astype(BF16),
            ffn2_g=ffn2_w_gate[l].astype(BF16), ffn2_u=ffn2_w_up[l].astype(BF16), ffn2_d=ffn2_w_down[l].astype(BF16),
            ln1_g=ln1_g[l], ln1_b=ln1_b[l], ln2_g=ln2_g[l], ln2_b=ln2_b[l], ln3_g=ln3_g[l], ln3_b=ln3_b[l],
            w_attn=w_attn, w_gdn=w_gdn, w_gate=w_gate, w_small=w_small, w_o=w_o[l].astype(BF16), conv_w=conv_w[l],
            a_log=a_log[l], dt_bias=dt_bias[l], gdn_norm_g=gdn_norm_g[l],
        )

        def dsa_p(v_rows, q_bf, k_rot, k_bf, v_bf, qi_bf, small_rot, small_bf):
            return _dsa_prompt(q_bf, qi_bf, small_rot, k_bf, v_bf, small_bf, B, S, min(S, 256))

        def dsa_s(v_rows, q_bf, k_rot, k_bf, v_bf, qi_bf, small_rot, small_bf, l=l):
            w0 = SMALL_OFF['w_idx']
            w3 = small_rot[:, w0:w0 + IDX_HEADS].reshape(DB, IDX_HEADS, 1)
            qi3 = qi_bf.reshape(DB, IDX_HEADS, IDX_DIM)
            knew_i = small_bf[:, :IDX_DIM].reshape(DB, IDX_DIM, 1)
            width = N_KV_HEADS * HEAD_DIM
            past, new = _sample_scores(page_table, qi3, w3, knew_i, jnp.swapaxes(cache_idx_k, 2, 3), l)
            scores3 = jnp.concatenate([past, new], axis=1)
            n_keys = n_pages * PAGE_SIZE + T
            ktop = min(TOPK_MAX, n_keys // 4)
            thr, jmax, take_all = _sample_threshold(scores3.reshape(DB, -1), ktop, n_keys)
            o = _sample_attn(page_table, q_bf.reshape(DB, N_HEADS, HEAD_DIM), scores3, thr, jmax, take_all,
                             k_rot.reshape(DB, 1, width), v_rows.reshape(DB, 1, width), cache_k, cache_v, l)
            return o.reshape(DB, D_MODEL)

        conv0 = jnp.zeros((B, CONV_W - 1, CONV_DIM), F32)
        ssm_zero = jnp.zeros((B, GDN_HEADS, GDN_DK, GDN_DV), F32)
        yp, st_p = _layer(yp, B, S, jnp.arange(S, dtype=jnp.int32), wts, conv0, ssm_zero, dsa_p)
        past_len = n_pages * PAGE_SIZE
        ys, st_s = _layer(ys, DB, T, past_len + jnp.arange(T, dtype=jnp.int32), wts, state_conv[l], state_ssm[l], dsa_s)
        outs_p.append(st_p)
        outs_s.append(st_s)

    def stack(outs, nb, nt):
        d = len(outs)
        k, v, ki, ssm, conv = [a[0][None] if d == 1 else jnp.stack(a) for a in zip(*outs)]
        return (k.reshape(d, nb, nt, N_KV_HEADS, HEAD_DIM), v.reshape(d, nb, nt, N_KV_HEADS, HEAD_DIM),
                ki.reshape(d, nb, nt, IDX_DIM), ssm, conv)

    return (yp.reshape(B, S, D_MODEL), ys.reshape(DB, T, D_MODEL)) + stack(outs_p, B, S) + stack(outs_s, DB, T)
```

```python
import functools

import jax
import jax.numpy as jnp
import numpy as np
from jax import lax
from jax.experimental import pallas as pl
from jax.experimental.pallas import tpu as pltpu

D_MODEL = 2048
PAST_LEN = 16384
PAGE_SIZE = 128
HEAD_DIM = 128
N_HEADS = D_MODEL // HEAD_DIM
N_KV_HEADS = 4
GROUP = N_HEADS // N_KV_HEADS
ROPE_DIM = HEAD_DIM // 4
IDX_HEADS = 16
IDX_DIM = 64
IDX_ROPE_DIM = IDX_DIM // 4
TOPK_MAX = 256
ROPE_THETA = 500000.0
GDN_DK = 128
GDN_DV = 128
GDN_HEADS = D_MODEL // GDN_DV
CONV_W = 4
CONV_DIM = 2 * GDN_HEADS * GDN_DK + GDN_HEADS * GDN_DV
CHUNK = 64
D_FF = 5632
LN_EPS = 1e-5
NORM_EPS = 1e-6
DEPTH = 1
DN_ALPHA = (2 * DEPTH) ** 0.25

IN_SPLITS = (
    ('q_a', N_HEADS * HEAD_DIM), ('k_a', N_KV_HEADS * HEAD_DIM), ('v_a', N_KV_HEADS * HEAD_DIM),
    ('q_idx', IDX_HEADS * IDX_DIM), ('k_idx', IDX_DIM), ('w_idx', IDX_HEADS),
    ('qkv_b', CONV_DIM), ('a_b', GDN_HEADS), ('beta_b', GDN_HEADS), ('z_b', GDN_HEADS * GDN_DV),
    ('gate_a', D_MODEL), ('gate_b', D_MODEL),
)
ATTN_ORDER = ('q_a', 'k_a', 'v_a', 'q_idx')
GDN_ORDER = ('qkv_b',)
GATE_ORDER = ('z_b', 'gate_a', 'gate_b')
SMALL_ORDER = ('k_idx', 'w_idx', 'a_b', 'beta_b')
LANES = 128
VMEM_LIMIT = 56 * 1024 * 1024
NEG_BIG = -1e30
INT_MIN = -2 ** 31

F32 = jnp.float32
BF16 = jnp.bfloat16


def _offsets(order):
    sizes = dict(IN_SPLITS)
    offs, o = {}, 0
    for nm in order:
        offs[nm] = o
        o += sizes[nm]
    return offs, o


ATTN_OFF, ATTN_COLS = _offsets(ATTN_ORDER)
GATE_OFF, GATE_COLS = _offsets(GATE_ORDER)
SMALL_OFF, SMALL_USED = _offsets(SMALL_ORDER)
SIZES = dict(IN_SPLITS)


def _cparams(*sem):
    return pltpu.CompilerParams(dimension_semantics=sem, vmem_limit_bytes=VMEM_LIMIT)


def _dot(a, b):
    return jnp.dot(a, b, preferred_element_type=F32)


def _dot_nt(a, b):
    return lax.dot_general(a, b, (((1,), (1,)), ((), ())), preferred_element_type=F32)


def _dot_hi(a, b):
    return jnp.dot(a, b, preferred_element_type=F32, precision=lax.Precision.HIGHEST)


def _dot_nt_hi(a, b):
    return lax.dot_general(a, b, (((1,), (1,)), ((), ())), preferred_element_type=F32,
                           precision=lax.Precision.HIGHEST)


def _dot_tn_hi(a, b):
    return lax.dot_general(a, b, (((0,), (0,)), ((), ())), preferred_element_type=F32,
                           precision=lax.Precision.HIGHEST)


def _dot_bf16(a, b):
    return _dot(a.astype(BF16), b.astype(BF16))


def _silu(x):
    return x * jax.nn.sigmoid(x)


def _layer_norm(y, g, b):
    mu = jnp.mean(y, axis=-1, keepdims=True)
    d = y - mu
    var = jnp.mean(d * d, axis=-1, keepdims=True)
    return d * lax.rsqrt(var + LN_EPS) * g + b


def _ffn_ln_kernel(x_ref, wg_ref, wu_ref, wd_ref, g_ref, b_ref, o_ref, acc_ref, xb_ref):
    j = pl.program_id(1)

    @pl.when(j == 0)
    def _():
        acc_ref[...] = jnp.zeros_like(acc_ref)
        xb_ref[...] = x_ref[...].astype(BF16)

    xb = xb_ref[...]
    hg = _dot(xb, wg_ref[...])
    hu = _dot(xb, wu_ref[...])
    h = _silu(hg) * hu
    acc_ref[...] += _dot(h.astype(BF16), wd_ref[...])

    @pl.when(j == pl.num_programs(1) - 1)
    def _():
        y = DN_ALPHA * x_ref[...] + 0.5 * acc_ref[...]
        o_ref[...] = _layer_norm(y, g_ref[...], b_ref[...])


def _ffn_ln(x, wg, wu, wd, g, b, tm, tf):
    M, D = x.shape
    F = wg.shape[1]
    return pl.pallas_call(
        _ffn_ln_kernel,
        grid=(M // tm, F // tf),
        in_specs=[
            pl.BlockSpec((tm, D), lambda i, j: (i, 0)),
            pl.BlockSpec((D, tf), lambda i, j: (0, j)),
            pl.BlockSpec((D, tf), lambda i, j: (0, j)),
            pl.BlockSpec((tf, D), lambda i, j: (j, 0)),
            pl.BlockSpec((1, D), lambda i, j: (0, 0)),
            pl.BlockSpec((1, D), lambda i, j: (0, 0)),
        ],
        out_specs=pl.BlockSpec((tm, D), lambda i, j: (i, 0)),
        out_shape=jax.ShapeDtypeStruct((M, D), F32),
        scratch_shapes=[pltpu.VMEM((tm, D), F32), pltpu.VMEM((tm, D), BF16)],
        compiler_params=_cparams("parallel", "arbitrary"),
        name="ffn_ln",
    )(x, wg, wu, wd, g, b)


def _proj_kernel(x_ref, w_ref, o_ref, xb_ref):
    @pl.when(pl.program_id(1) == 0)
    def _():
        xb_ref[...] = x_ref[...].astype(BF16)

    o_ref[...] = _dot(xb_ref[...], w_ref[...])


def _proj(x, w, tm, tn):
    M, K = x.shape
    N = w.shape[1]
    return pl.pallas_call(
        _proj_kernel,
        grid=(M // tm, N // tn),
        in_specs=[
            pl.BlockSpec((tm, K), lambda i, j: (i, 0)),
            pl.BlockSpec((K, tn), lambda i, j: (0, j)),
        ],
        out_specs=pl.BlockSpec((tm, tn), lambda i, j: (i, j)),
        out_shape=jax.ShapeDtypeStruct((M, N), F32),
        scratch_shapes=[pltpu.VMEM((tm, K), BF16)],
        compiler_params=_cparams("parallel", "arbitrary"),
        name="in_proj",
    )(x, w)


def _rope_tables(pos, rot_dim, period, live_lanes=LANES):
    half = rot_dim // 2
    inv = ROPE_THETA ** (-jnp.arange(half, dtype=F32) / half)
    ang = pos.astype(F32)[:, None] * inv[None, :]
    cos, sin = jnp.cos(ang), jnp.sin(ang)
    lane = np.arange(LANES)
    lp = lane % period
    idx = lp % half
    live = (lp < rot_dim) & (lane < live_lanes)
    c = jnp.where(live[None, :], cos[:, idx], 1.0)
    s = jnp.where(live[None, :], jnp.where((lp < half)[None, :], -sin[:, idx], sin[:, idx]), 0.0)
    return c.astype(F32), s.astype(F32)


def _rope_tile(x, c, s, half, period):
    lane = lax.broadcasted_iota(jnp.int32, x.shape, 1)
    first = (lane & (period - 1)) < half
    partner = jnp.where(first, pltpu.roll(x, LANES - half, 1), pltpu.roll(x, half, 1))
    return x * c + partner * s


def _rope_kernel(q_ref, k_ref, v_ref, qi_ref, sm_ref, ca_ref, sa_ref, ci_ref, si_ref, cs_ref, ss_ref,
                 qo_ref, ko_ref, kb_ref, vb_ref, qio_ref, smo_ref, smb_ref):
    ca, sa = ca_ref[...], sa_ref[...]
    ci, si = ci_ref[...], si_ref[...]
    for h in range(N_HEADS):
        sl = slice(h * LANES, (h + 1) * LANES)
        qo_ref[:, sl] = (_rope_tile(q_ref[:, sl], ca, sa, ROPE_DIM // 2, HEAD_DIM) * (HEAD_DIM ** -0.5)).astype(BF16)
    for h in range(N_KV_HEADS):
        sl = slice(h * LANES, (h + 1) * LANES)
        kr = _rope_tile(k_ref[:, sl], ca, sa, ROPE_DIM // 2, HEAD_DIM)
        ko_ref[:, sl] = kr
        kb_ref[:, sl] = kr.astype(BF16)
    vb_ref[...] = v_ref[...].astype(BF16)
    for h in range(IDX_HEADS * IDX_DIM // LANES):
        sl = slice(h * LANES, (h + 1) * LANES)
        qio_ref[:, sl] = _rope_tile(qi_ref[:, sl], ci, si, IDX_ROPE_DIM // 2, IDX_DIM).astype(BF16)
    sm = _rope_tile(sm_ref[...], cs_ref[...], ss_ref[...], IDX_ROPE_DIM // 2, IDX_DIM)
    smo_ref[...] = sm
    smb_ref[...] = sm.astype(BF16)


def _rope_prep(u_attn, u_small, tabs, tm):
    M = u_attn.shape[0]
    tpos = tabs[0].shape[0]
    nt = tpos // tm

    def col(name):
        w = SIZES[name]
        return pl.BlockSpec((tm, w), lambda i, o=ATTN_OFF[name] // w: (i, o))

    tab_spec = pl.BlockSpec((tm, LANES), lambda i: (i % nt, 0))
    row = lambda w: pl.BlockSpec((tm, w), lambda i: (i, 0))
    return pl.pallas_call(
        _rope_kernel,
        grid=(M // tm,),
        in_specs=[col('q_a'), col('k_a'), col('v_a'), col('q_idx'), row(LANES)] + [tab_spec] * 6,
        out_specs=[row(SIZES['q_a']), row(SIZES['k_a']), row(SIZES['k_a']), row(SIZES['v_a']),
                   row(SIZES['q_idx']), row(LANES), row(LANES)],
        out_shape=[
            jax.ShapeDtypeStruct((M, SIZES['q_a']), BF16),
            jax.ShapeDtypeStruct((M, SIZES['k_a']), F32),
            jax.ShapeDtypeStruct((M, SIZES['k_a']), BF16),
            jax.ShapeDtypeStruct((M, SIZES['v_a']), BF16),
            jax.ShapeDtypeStruct((M, SIZES['q_idx']), BF16),
            jax.ShapeDtypeStruct((M, LANES), F32),
            jax.ShapeDtypeStruct((M, LANES), BF16),
        ],
        compiler_params=_cparams("parallel"),
        name="rope_prep",
    )(u_attn, u_attn, u_attn, u_attn, u_small, *tabs)


BISECT_UNROLL = 4


def _count(pred):
    return jnp.sum(jnp.where(pred, 1.0, 0.0), axis=-1, keepdims=True)


def _tie_index(score, kidx, thr):
    return jnp.where(score == thr, kidx, jnp.int32(2 ** 31 - 1))


def _topk_threshold(score, kidx, n_allowed, k, idx_bits):
    rows = score.shape[0]
    take_all = n_allowed <= k
    lo0 = jnp.min(jnp.where(score == -jnp.inf, jnp.inf, score), axis=-1, keepdims=True)
    hi0 = jnp.max(score, axis=-1, keepdims=True)
    lo0 = jnp.where(take_all, 0.0, lo0)
    hi0 = jnp.where(take_all, 0.0, hi0)

    def step(lo, hi, n_lo):
        mid = 0.5 * lo + 0.5 * hi
        n_mid = _count(score >= mid)
        ge = n_mid >= k
        return jnp.where(ge, mid, lo), jnp.where(ge, hi, mid), jnp.where(ge, n_mid, n_lo)

    def body(state):
        lo, hi, n_lo, _ = state
        for _ in range(BISECT_UNROLL):
            lo, hi, n_lo = step(lo, hi, n_lo)
        mid = 0.5 * lo + 0.5 * hi
        still_open = jnp.max(jnp.where((mid > lo) & (mid < hi) & (n_lo > k), 1.0, 0.0))
        return lo, hi, n_lo, still_open

    n_lo0 = jnp.where(take_all, float(k), n_allowed.astype(F32))
    lo, hi, _, _ = lax.while_loop(lambda state: state[3] > 0.5, body, (lo0, hi0, n_lo0, jnp.float32(1.0)))
    thr = jnp.where(_count(score >= hi) >= k, hi, lo)
    tie = _tie_index(score, kidx, thr)

    def tie_search():
        need = k - _count(score > thr)

        def ibody(t, j):
            cand = j + jnp.left_shift(jnp.int32(1), idx_bits - 1 - t)
            return jnp.where(_count(tie < cand) < need, cand, j)

        return lax.fori_loop(0, idx_bits, ibody, jnp.zeros((rows, 1), jnp.int32))

    repeated = jnp.max(jnp.where(take_all, 0.0, _count(score == thr))) > 1.5
    jmax = lax.cond(repeated, tie_search, lambda: jnp.full((rows, 1), 2 ** 31 - 2, jnp.int32))
    return thr, jmax, take_all


def _selected(score, kidx, thr, jmax, take_all):
    return take_all | (score > thr) | (_tie_index(score, kidx, thr) <= jmax)


def _dsa_prompt_kernel(q_ref, qi_ref, sm_ref, k_ref, v_ref, kis_ref, o_ref, *, ktop, key_step):
    tq = q_ref.shape[0]
    S = k_ref.shape[0]
    i = pl.program_id(1)
    w0 = SMALL_OFF['w_idx']

    def attend(L):
        ki = kis_ref[:L, :IDX_DIM]
        w = sm_ref[:, w0:w0 + IDX_HEADS] * (IDX_HEADS ** -0.5 * IDX_DIM ** -0.5)
        head_lane = lax.broadcasted_iota(jnp.int32, (tq, IDX_HEADS), 1)
        per_trip = 2
        width = per_trip * IDX_DIM

        def idx_heads(t, score):
            qi = qi_ref[:, pl.ds(pl.multiple_of(t * width, width), width)]
            for j in range(per_trip):
                s = _dot_nt(qi[:, j * IDX_DIM:(j + 1) * IDX_DIM], ki)
                w_h = jnp.sum(jnp.where(head_lane == t * per_trip + j, w, 0.0), axis=-1, keepdims=True)
                score = score + w_h * jnp.maximum(s, 0.0)
            return score

        score = lax.fori_loop(0, IDX_HEADS // per_trip, idx_heads, jnp.zeros((tq, L), F32))
        qpos = i * tq + lax.broadcasted_iota(jnp.int32, (tq, 1), 0)
        kidx = lax.broadcasted_iota(jnp.int32, (tq, L), 1)
        allowed = kidx <= qpos
        score = jnp.where(allowed, score, -jnp.inf)
        thr, jmax, take_all = _topk_threshold(score, kidx, qpos + 1, ktop, int(L - 1).bit_length())
        bias = jnp.where(_selected(score, kidx, thr, jmax, take_all) & allowed, 0.0, -jnp.inf)
        def head(h):
            sl = pl.ds(pl.multiple_of(h * HEAD_DIM, HEAD_DIM), HEAD_DIM)
            kv = pl.ds(pl.multiple_of((h // GROUP) * HEAD_DIM, HEAD_DIM), HEAD_DIM)
            s = _dot_nt(q_ref[:, sl], k_ref[:L, kv]) + bias
            m = jnp.max(s, axis=-1, keepdims=True)
            p = jnp.exp(s - m)
            l = jnp.sum(p, axis=-1, keepdims=True)
            o_ref[:, sl] = _dot(p.astype(BF16), v_ref[:L, kv]) / l

        def pair(t, carry):
            head(2 * t)
            head(2 * t + 1)
            return carry

        lax.fori_loop(0, N_HEADS // 2, pair, 0)

    level = ((i + 1) * tq - 1) // key_step
    for lv in range(S // key_step):
        pl.when(level == lv)(functools.partial(attend, (lv + 1) * key_step))


def _dsa_prompt(q_bf, qi_bf, small_rot, k_bf, v_bf, small_bf, B, S, tq):
    ktop = min(TOPK_MAX, S // 4)
    nq = S // tq
    key_step = min(S, 256)
    row = lambda w: pl.BlockSpec((tq, w), lambda b, i: (b * nq + i, 0))
    full = lambda w: pl.BlockSpec((S, w), lambda b, i: (b, 0))
    return pl.pallas_call(
        functools.partial(_dsa_prompt_kernel, ktop=ktop, key_step=key_step),
        grid=(B, nq),
        in_specs=[row(q_bf.shape[1]), row(qi_bf.shape[1]), row(LANES),
                  full(k_bf.shape[1]), full(v_bf.shape[1]), full(LANES)],
        out_specs=row(q_bf.shape[1]),
        out_shape=jax.ShapeDtypeStruct((B * S, q_bf.shape[1]), F32),
        compiler_params=_cparams("parallel", "arbitrary"),
        name="dsa_prompt",
    )(q_bf, qi_bf, small_rot, k_bf, v_bf, small_bf)


def _idx_score_rows(qi, w, kpage_t):
    s = _dot(qi, kpage_t)
    return jnp.sum(w * jnp.maximum(s, 0.0), axis=0, keepdims=True)


def _sample_scores_kernel(pt_ref, qi_ref, w_ref, knew_ref, *refs):
    page_refs, (o_ref, onew_ref) = refs[:-2], refs[-2:]
    npp = len(page_refs)
    p = pl.program_id(1)
    qi = qi_ref[0]
    w = w_ref[0] * (IDX_HEADS ** -0.5 * IDX_DIM ** -0.5)
    for j, page_ref in enumerate(page_refs):
        o_ref[0, pl.ds(p * npp + j, 1), :] = _idx_score_rows(qi, w, page_ref[0, 0].astype(BF16))

    @pl.when(p == 0)
    def _():
        kn = jnp.broadcast_to(knew_ref[0], (IDX_DIM, PAGE_SIZE))
        sc = _idx_score_rows(qi, w, kn)
        lane = lax.broadcasted_iota(jnp.int32, (1, PAGE_SIZE), 1)
        onew_ref[0] = jnp.where(lane == 0, sc, -jnp.inf)


def _page_specs(page_shape, npp, layer):
    zeros = (0,) * len(page_shape)
    return [pl.BlockSpec((1, 1) + page_shape, lambda b, p, pt, j=j: (layer, pt[b, p * npp + j]) + zeros)
            for j in range(npp)]


def _pages_per_step(n_pages, cap):
    npp = min(cap, n_pages)
    while n_pages % npp:
        npp -= 1
    return npp


def _sample_scores(page_table, qi3, w3, knew3, cache_ki, layer):
    DB, n_pages = page_table.shape
    npp = _pages_per_step(n_pages, 16)
    return pl.pallas_call(
        _sample_scores_kernel,
        grid_spec=pltpu.PrefetchScalarGridSpec(
            num_scalar_prefetch=1,
            grid=(DB, n_pages // npp),
            in_specs=[
                pl.BlockSpec((1, IDX_HEADS, IDX_DIM), lambda b, p, pt: (b, 0, 0)),
                pl.BlockSpec((1, IDX_HEADS, 1), lambda b, p, pt: (b, 0, 0)),
                pl.BlockSpec((1, IDX_DIM, 1), lambda b, p, pt: (b, 0, 0)),
            ] + _page_specs((IDX_DIM, PAGE_SIZE), npp, layer),
            out_specs=[
                pl.BlockSpec((1, n_pages, PAGE_SIZE), lambda b, p, pt: (b, 0, 0)),
                pl.BlockSpec((1, 1, PAGE_SIZE), lambda b, p, pt: (b, 0, 0)),
            ],
        ),
        out_shape=[jax.ShapeDtypeStruct((DB, n_pages, PAGE_SIZE), F32),
                   jax.ShapeDtypeStruct((DB, 1, PAGE_SIZE), F32)],
        compiler_params=_cparams("parallel", "arbitrary"),
        name="sample_scores",
    )(page_table, qi3, w3, knew3, *([cache_ki] * npp))


def _sample_thr_kernel(s_ref, thr_ref, jmax_ref, all_ref, *, ktop, n_valid, idx_bits):
    kidx = lax.broadcasted_iota(jnp.int32, s_ref.shape, 1)
    score = jnp.where(kidx < n_valid, s_ref[...], -jnp.inf)
    n_allowed = jnp.full((s_ref.shape[0], 1), n_valid, jnp.int32)
    thr, jmax, take_all = _topk_threshold(score, kidx, n_allowed, ktop, idx_bits)
    thr_ref[...] = jnp.broadcast_to(thr, thr_ref.shape)
    jmax_ref[...] = jnp.broadcast_to(jmax, jmax_ref.shape)
    all_ref[...] = jnp.broadcast_to(take_all.astype(jnp.int32), all_ref.shape)


def _sample_threshold(scores, ktop, n_valid):
    DB, L = scores.shape
    out = lambda dt: jax.ShapeDtypeStruct((DB, LANES), dt)
    return pl.pallas_call(
        functools.partial(_sample_thr_kernel, ktop=ktop, n_valid=n_valid, idx_bits=int(L - 1).bit_length()),
        out_shape=[out(F32), out(jnp.int32), out(jnp.int32)],
        compiler_params=pltpu.CompilerParams(vmem_limit_bytes=VMEM_LIMIT),
        name="sample_threshold",
    )(scores)


def _sample_attn_kernel(pt_ref, q_ref, s_ref, thr_ref, jmax_ref, all_ref, knew_ref, vnew_ref, *refs, n_pages):
    npp = (len(refs) - 4) // 2
    kp_refs, vp_refs = refs[:npp], refs[npp:2 * npp]
    o_ref, m_ref, l_ref, acc_ref = refs[2 * npp:]
    b = pl.program_id(0)
    p = pl.program_id(1)

    @pl.when(p == 0)
    def _():
        m_ref[...] = jnp.full_like(m_ref, NEG_BIG)
        l_ref[...] = jnp.zeros_like(l_ref)
        acc_ref[...] = jnp.zeros_like(acc_ref)

    thr = thr_ref[pl.ds(b, 1), 0:1]
    jmax = jmax_ref[pl.ds(b, 1), 0:1]
    take_all = all_ref[pl.ds(b, 1), 0:1] > 0

    q = q_ref[0]
    group = lax.broadcasted_iota(jnp.int32, (N_HEADS, 1), 0) // GROUP
    X = PAGE_SIZE * N_KV_HEADS
    own_head = (lax.broadcasted_iota(jnp.int32, (N_HEADS, X), 1) % N_KV_HEADS) == group
    repeat = (lax.broadcasted_iota(jnp.int32, (PAGE_SIZE, X), 1) // N_KV_HEADS
              == lax.broadcasted_iota(jnp.int32, (PAGE_SIZE, X), 0)).astype(BF16)

    def fold(logits, pv_fns):
        m_old = m_ref[...]
        m_new = functools.reduce(jnp.maximum, [jnp.max(s, axis=-1, keepdims=True) for s in logits], m_old)
        corr = jnp.exp(m_old - m_new)
        probs = [jnp.exp(s - m_new) for s in logits]
        l_ref[...] = l_ref[...] * corr + functools.reduce(
            lambda a, c: a + c, [jnp.sum(pr, axis=-1, keepdims=True) for pr in probs])
        acc_ref[...] = acc_ref[...] * corr + functools.reduce(
            lambda a, c: a + c, [fn(pr) for fn, pr in zip(pv_fns, probs)])
        m_ref[...] = m_new

    logits, pv_fns = [], []
    for j in range(npp):
        page = p * npp + j
        scores_row = s_ref[0, pl.ds(page, 1), :]
        kidx = page * PAGE_SIZE + lax.broadcasted_iota(jnp.int32, (1, PAGE_SIZE), 1)
        sel = _selected(scores_row, kidx, thr, jmax, take_all)
        sel_rows = _dot(jnp.broadcast_to(jnp.where(sel, 1.0, 0.0), (N_HEADS, PAGE_SIZE)).astype(BF16), repeat)
        s = _dot_nt(q, kp_refs[j][0].astype(BF16))
        logits.append(jnp.where(own_head & (sel_rows > 0.5), s, -jnp.inf))
        pv_fns.append(lambda pr, ref=vp_refs[j]: _dot(pr.astype(BF16), ref[0].astype(BF16)))
    fold(logits, pv_fns)

    @pl.when(p == pl.num_programs(1) - 1)
    def _():
        def per_head(ref):
            rows = [jnp.where(group == n, jnp.broadcast_to(ref[0, :, n * HEAD_DIM:(n + 1) * HEAD_DIM],
                                                           (N_HEADS, HEAD_DIM)), 0.0) for n in range(N_KV_HEADS)]
            return functools.reduce(lambda a, c: a + c, rows).astype(BF16).astype(F32)

        score_new = s_ref[0, pl.ds(n_pages, 1), 0:1]
        sel = _selected(score_new, jnp.full((1, 1), n_pages * PAGE_SIZE, jnp.int32), thr, jmax, take_all)
        s = jnp.sum(q.astype(F32) * per_head(knew_ref), axis=-1, keepdims=True)
        s = jnp.where(sel, s, -jnp.inf)
        vexp = per_head(vnew_ref)
        fold([s], [lambda pr: pr.astype(BF16).astype(F32) * vexp])
        o_ref[0] = acc_ref[...] / l_ref[...]


def _sample_attn(page_table, q3, scores3, thr, jmax, take_all, knew3, vnew3, cache_k, cache_v, layer):
    DB, n_pages = page_table.shape
    width = N_KV_HEADS * HEAD_DIM
    npp = _pages_per_step(n_pages, 32)
    bsel = lambda *shape: pl.BlockSpec((1,) + shape, lambda b, p, pt: (b,) + (0,) * len(shape))
    whole = pl.BlockSpec((DB, LANES), lambda b, p, pt: (0, 0))
    n_phys = cache_k.shape[1]
    rows = PAGE_SIZE * N_KV_HEADS
    cache_k = cache_k.reshape(-1, rows, HEAD_DIM)
    cache_v = cache_v.reshape(-1, rows, HEAD_DIM)
    pages = [pl.BlockSpec((1, rows, HEAD_DIM),
                          lambda b, p, pt, j=j: (layer * n_phys + pt[b, p * npp + j], 0, 0)) for j in range(npp)]
    return pl.pallas_call(
        functools.partial(_sample_attn_kernel, n_pages=n_pages),
        grid_spec=pltpu.PrefetchScalarGridSpec(
            num_scalar_prefetch=1,
            grid=(DB, n_pages // npp),
            in_specs=[bsel(N_HEADS, HEAD_DIM), bsel(n_pages + 1, PAGE_SIZE), whole, whole, whole,
                      bsel(1, width), bsel(1, width)] + pages + pages,
            out_specs=bsel(N_HEADS, HEAD_DIM),
            scratch_shapes=[pltpu.VMEM((N_HEADS, 1), F32), pltpu.VMEM((N_HEADS, 1), F32),
                            pltpu.VMEM((N_HEADS, HEAD_DIM), F32)],
        ),
        out_shape=jax.ShapeDtypeStruct((DB, N_HEADS, HEAD_DIM), F32),
        compiler_params=_cparams("parallel", "arbitrary"),
        name="sample_attn",
    )(page_table, q3, scores3, thr, jmax, take_all, knew3, vnew3, *([cache_k] * npp), *([cache_v] * npp))


CARRY = 8
MXU_DIM = 256
GDN_GROUP = MXU_DIM // CHUNK


def _spread(a, row_head, hg):
    return jnp.concatenate([jnp.where(row_head == i, a, 0.0) for i in range(hg)], axis=1)


def _gdn_prep_kernel(x_ref, w_ref, buf_ref, q_ref, k_ref, v_ref, conv_ref, xpad_ref):
    tt = x_ref.shape[1]
    t = pl.program_id(1)
    lo = CARRY - (CONV_W - 1)

    @pl.when(t == 0)
    def _():
        xpad_ref[lo:CARRY, :] = buf_ref[0]

    xpad_ref[CARRY:CARRY + tt, :] = x_ref[0]
    nh = GDN_HEADS
    for c in range(CONV_DIM // LANES):
        sl = slice(c * LANES, (c + 1) * LANES)
        y = w_ref[0:1, sl] * xpad_ref[lo:lo + tt, sl]
        for j in range(1, CONV_W):
            y = y + w_ref[j:j + 1, sl] * xpad_ref[lo + j:lo + j + tt, sl]
        y = _silu(y)
        if c < 2 * nh:
            y = y * lax.rsqrt(jnp.sum(y * y, axis=-1, keepdims=True) + NORM_EPS)
        if c < nh:
            q_ref[0, :, sl] = y * (GDN_DK ** -0.5)
        elif c < 2 * nh:
            k_ref[0, :, slice((c - nh) * LANES, (c - nh + 1) * LANES)] = y
        else:
            v_ref[0, :, slice((c - 2 * nh) * LANES, (c - 2 * nh + 1) * LANES)] = y
    last = xpad_ref[lo + tt:CARRY + tt, :]
    xpad_ref[lo:CARRY, :] = last

    @pl.when(t == pl.num_programs(1) - 1)
    def _():
        conv_ref[0] = last


def _gdn_prep(u_gdn3, conv_w, buf, tt):
    B, T, _ = u_gdn3.shape
    w = SIZES['qkv_b']
    hd = GDN_HEADS * GDN_DK
    out = jax.ShapeDtypeStruct((B, T, hd), F32)
    ospec = pl.BlockSpec((1, tt, hd), lambda b, t: (b, t, 0))
    return pl.pallas_call(
        _gdn_prep_kernel,
        grid=(B, T // tt),
        in_specs=[
            pl.BlockSpec((1, tt, w), lambda b, t: (b, t, 0)),
            pl.BlockSpec((CONV_W, w), lambda b, t: (0, 0)),
            pl.BlockSpec((1, CONV_W - 1, w), lambda b, t: (b, 0, 0)),
        ],
        out_specs=[ospec, ospec, ospec, pl.BlockSpec((1, CONV_W - 1, w), lambda b, t: (b, 0, 0))],
        out_shape=[out, out, out, jax.ShapeDtypeStruct((B, CONV_W - 1, w), F32)],
        scratch_shapes=[pltpu.VMEM((CARRY + tt, w), F32)],
        compiler_params=_cparams("parallel", "arbitrary"),
        name="gdn_prep",
    )(u_gdn3, conv_w, buf)


def _gdn_chunk_kernel(q_ref, k_ref, v_ref, z_ref, sm_ref, alog_ref, dtb_ref, gn_ref, s0_ref,
                      o_ref, sout_ref, state_ref, *, t_valid):
    C = q_ref.shape[1]
    c = pl.program_id(1)

    @pl.when(c == 0)
    def _():
        state_ref[...] = s0_ref[0]

    H = GDN_HEADS
    a0, b0 = SMALL_OFF['a_b'], SMALL_OFF['beta_b']
    live = (c * C + lax.broadcasted_iota(jnp.int32, (C, H), 0)) < t_valid
    xs = sm_ref[0, :, a0:a0 + H] + dtb_ref[...]
    softplus = jnp.maximum(xs, 0.0) + jnp.log1p(jnp.exp(-jnp.abs(xs)))
    g_all = jnp.where(live, -jnp.exp(alog_ref[...]) * softplus, 0.0)
    beta_all = jnp.where(live, jax.nn.sigmoid(sm_ref[0, :, b0:b0 + H]), 0.0)
    tri_f = (lax.broadcasted_iota(jnp.int32, (C, C), 0) >= lax.broadcasted_iota(jnp.int32, (C, C), 1)).astype(F32)
    eye_h = (lax.broadcasted_iota(jnp.int32, (H, H), 0) == lax.broadcasted_iota(jnp.int32, (H, H), 1)).astype(F32)
    gcum = _dot_hi(tri_f, g_all)
    gcum_t = _dot_nt_hi(eye_h, gcum)

    HG = GDN_GROUP
    R = HG * C
    ri = lax.broadcasted_iota(jnp.int32, (R, R), 0)
    ci = lax.broadcasted_iota(jnp.int32, (R, R), 1)
    same_head = (ri // C) == (ci // C)
    mask_incl = same_head & (ri >= ci)
    mask_strict = same_head & (ri > ci)
    eye = (ri == ci).astype(F32)
    row_head = lax.broadcasted_iota(jnp.int32, (R, 1), 0) // C
    row_head2 = jnp.concatenate([row_head, row_head], axis=0)

    groups = []
    for grp in range(H // HG):
        heads = range(grp * HG, (grp + 1) * HG)
        sls = [slice(h * GDN_DK, (h + 1) * GDN_DK) for h in heads]
        rows = lambda ref: jnp.concatenate([ref[0, :, sl] for sl in sls], axis=0)
        cols = lambda a: jnp.concatenate([a[:, h:h + 1] for h in heads], axis=0)
        q, k, v = rows(q_ref), rows(k_ref), rows(v_ref)
        beta = cols(beta_all)
        gcol = cols(gcum)
        grow = jnp.concatenate([gcum_t[h:h + 1, :] for h in heads], axis=1)
        glast = jnp.concatenate([jnp.broadcast_to(gcum[C - 1:C, h:h + 1], (C, 1)) for h in heads], axis=0)
        decay = jnp.where(mask_incl, jnp.exp(jnp.where(mask_incl, gcol - grow, 0.0)), 0.0)
        kb = k * beta
        kk_qk = _dot_nt(jnp.concatenate([kb, q], axis=0).astype(BF16), k.astype(BF16))
        nmat = jnp.where(mask_strict, kk_qk[:R] * decay, 0.0)
        groups.append(dict(
            heads=heads, sls=sls, nmat=nmat.astype(BF16), xinv=eye - nmat, qk=(kk_qk[R:] * decay).astype(BF16),
            rhs=jnp.concatenate([v * beta, kb * jnp.exp(gcol)], axis=1).astype(BF16),
            q_dec=q * jnp.exp(gcol), k_dec=k * jnp.exp(glast - gcol)))

    for _ in range(max(C - 1, 1).bit_length() - 1):
        for g in groups:
            g['resid'] = (eye - g['xinv']) - _dot(g['nmat'], g['xinv'].astype(BF16))
        for g in groups:
            g['xinv'] = g['xinv'] + _dot_bf16(g['xinv'], g['resid'])
    for g in groups:
        g['x'] = _dot(g['xinv'].astype(BF16), g['rhs'])
    for grp, g in enumerate(groups):
        u, w = g['x'][:, :GDN_DV], g['x'][:, GDN_DV:]
        s_stack = state_ref[grp * HG:(grp + 1) * HG].reshape(HG * GDN_DK, GDN_DV)
        w_q = jnp.concatenate([w, g['q_dec']], axis=0)
        ws_qs = _dot(_spread(w_q, row_head2, HG).astype(BF16), s_stack.astype(BF16))
        g['v_new'] = u - ws_qs[:R]
        g['o'] = ws_qs[R:]
    for g in groups:
        g['o'] = g['o'] + _dot(g['qk'], g['v_new'].astype(BF16))
        g['s_add'] = _dot(g['k_dec'].T.astype(BF16), _spread(g['v_new'], row_head, HG).astype(BF16))
    for g in groups:
        o = g['o']
        on = o * lax.rsqrt(jnp.mean(o * o, axis=-1, keepdims=True) + NORM_EPS) * gn_ref[...]
        for a, h in enumerate(g['heads']):
            state_ref[h] = (state_ref[h] * jnp.exp(gcum[C - 1:C, h:h + 1])
                            + g['s_add'][:, a * GDN_DV:(a + 1) * GDN_DV])
            o_ref[0, :, g['sls'][a]] = on[a * C:(a + 1) * C] * _silu(z_ref[0, :, g['sls'][a]])

    @pl.when(c == pl.num_programs(1) - 1)
    def _():
        sout_ref[0] = state_ref[...]


def _gdn_chunks(qn, kn, vv, u_gate3, small3, a_log, dt_bias, gn, s0, t_valid):
    B, Tp, hd = qn.shape
    nc = Tp // CHUNK
    blk = pl.BlockSpec((1, CHUNK, hd), lambda b, c: (b, c, 0))
    vec = lambda w: pl.BlockSpec((1, w), lambda b, c: (0, 0))
    st = pl.BlockSpec((1, GDN_HEADS, GDN_DK, GDN_DV), lambda b, c: (b, 0, 0, 0))
    return pl.pallas_call(
        functools.partial(_gdn_chunk_kernel, t_valid=t_valid),
        grid=(B, nc),
        in_specs=[blk, blk, blk,
                  pl.BlockSpec((1, CHUNK, hd), lambda b, c, o=GATE_OFF['z_b'] // hd: (b, c, o)),
                  pl.BlockSpec((1, CHUNK, LANES), lambda b, c: (b, c, 0)),
                  vec(GDN_HEADS), vec(GDN_HEADS), vec(GDN_DV), st],
        out_specs=[blk, st],
        out_shape=[jax.ShapeDtypeStruct((B, Tp, hd), F32),
                   jax.ShapeDtypeStruct((B, GDN_HEADS, GDN_DK, GDN_DV), F32)],
        scratch_shapes=[pltpu.VMEM((GDN_HEADS, GDN_DK, GDN_DV), F32)],
        compiler_params=_cparams("parallel", "arbitrary"),
        name="gdn_chunks",
    )(qn, kn, vv, u_gate3, small3, a_log, dt_bias, gn, s0)


def _merge_kernel(x_ref, oa_ref, ob_ref, ga_ref, gb_ref, wo_ref, g_ref, b_ref, o_ref):
    merged = jax.nn.sigmoid(ga_ref[...]) * oa_ref[...] + jax.nn.sigmoid(gb_ref[...]) * ob_ref[...]
    y = DN_ALPHA * x_ref[...] + _dot(merged.astype(BF16), wo_ref[...])
    o_ref[...] = _layer_norm(y, g_ref[...], b_ref[...])


def _merge_proj_ln(x, o_a, o_b, u_gate, w_o, g, b, tm):
    M, D = x.shape
    row = pl.BlockSpec((tm, D), lambda i: (i, 0))
    col = lambda name: pl.BlockSpec((tm, D), lambda i, o=GATE_OFF[name] // D: (i, o))
    vec = pl.BlockSpec((1, D), lambda i: (0, 0))
    return pl.pallas_call(
        _merge_kernel,
        grid=(M // tm,),
        in_specs=[row, row, row, col('gate_a'), col('gate_b'),
                  pl.BlockSpec((D, D), lambda i: (0, 0)), vec, vec],
        out_specs=row,
        out_shape=jax.ShapeDtypeStruct((M, D), F32),
        compiler_params=_cparams("parallel"),
        name="merge_proj_ln",
    )(x, o_a, o_b, u_gate, u_gate, w_o, g, b)


def _tiles(M):
    return (512, 256) if M % 512 == 0 else (M, M)


def _layer(x, B, T, pos, wts, conv_buf, ssm0, dsa_fn):
    M = B * T
    tm, te = _tiles(M)
    row = lambda a: a.reshape(1, -1)
    x1 = _ffn_ln(x, wts['ffn1_g'], wts['ffn1_u'], wts['ffn1_d'], row(wts['ln1_g']), row(wts['ln1_b']),
                 tm, 512)
    tp = 1024 if M % 1024 == 0 else tm
    u_attn = _proj(x1, wts['w_attn'], tp, 1024)
    u_gdn = _proj(x1, wts['w_gdn'], tp, 1024)
    u_gate = _proj(x1, wts['w_gate'], tp, 1024)
    u_small = _proj(x1, wts['w_small'], tp, LANES)

    pos_rows = pos if T > 1 else jnp.broadcast_to(pos, (M,))
    tabs = (_rope_tables(pos_rows, ROPE_DIM, HEAD_DIM) + _rope_tables(pos_rows, IDX_ROPE_DIM, IDX_DIM)
            + _rope_tables(pos_rows, IDX_ROPE_DIM, IDX_DIM, live_lanes=IDX_DIM))
    q_bf, k_rot, k_bf, v_bf, qi_bf, small_rot, small_bf = _rope_prep(u_attn, u_small, tabs, te)
    v_rows = u_attn[:, ATTN_OFF['v_a']:ATTN_OFF['v_a'] + SIZES['v_a']]
    o_a = dsa_fn(v_rows, q_bf, k_rot, k_bf, v_bf, qi_bf, small_rot, small_bf)

    qn, kn, vv, conv_new = _gdn_prep(u_gdn.reshape(B, T, -1), wts['conv_w'], conv_buf, min(T, 256))
    pad = (-T) % CHUNK
    pad3 = lambda a: jnp.pad(a, ((0, 0), (0, pad), (0, 0))) if pad else a
    o_b, ssm_new = _gdn_chunks(pad3(qn), pad3(kn), pad3(vv), pad3(u_gate.reshape(B, T, -1)),
                               pad3(u_small.reshape(B, T, LANES)),
                               row(wts['a_log']), row(wts['dt_bias']), row(wts['gdn_norm_g']), ssm0, T)
    o_b = o_b[:, :T].reshape(M, D_MODEL)

    x2 = _merge_proj_ln(x1, o_a, o_b, u_gate, wts['w_o'], row(wts['ln2_g']), row(wts['ln2_b']), te)
    y = _ffn_ln(x2, wts['ffn2_g'], wts['ffn2_u'], wts['ffn2_d'], row(wts['ln3_g']), row(wts['ln3_b']),
                tm, 512)
    ki_rows = small_rot[:, SMALL_OFF['k_idx']:SMALL_OFF['k_idx'] + IDX_DIM]
    return y, (k_rot, v_rows, ki_rows, ssm_new, conv_new)


def _split_w_in(w_in):
    offs = dict(zip([nm for nm, _ in IN_SPLITS], np.cumsum([0] + [n for _, n in IN_SPLITS])))

    def span(order):
        lo = offs[order[0]]
        hi = offs[order[-1]] + SIZES[order[-1]]
        assert hi - lo == sum(SIZES[nm] for nm in order)
        return w_in[:, lo:hi].astype(BF16)

    w_small = jnp.concatenate([w_in[:, offs[nm]:offs[nm] + SIZES[nm]] for nm in SMALL_ORDER], axis=1)
    w_small = jnp.pad(w_small, ((0, 0), (0, LANES - SMALL_USED))).astype(BF16)
    return span(ATTN_ORDER), span(GDN_ORDER), span(GATE_ORDER), w_small


def kernel(x_prompt, x_sample, cache_k, cache_v, cache_idx_k, state_ssm, state_conv, page_table, ffn1_w_gate, ffn1_w_up, ffn1_w_down, ln1_g, ln1_b, w_in, conv_w, a_log, dt_bias, gdn_norm_g, w_o, ln2_g, ln2_b, ffn2_w_gate, ffn2_w_up, ffn2_w_down, ln3_g, ln3_b):
    B, S, _ = x_prompt.shape
    DB, T, _ = x_sample.shape
    assert T == 1, "the sample path handles one new token per sequence"
    n_pages = page_table.shape[1]
    yp = x_prompt.reshape(B * S, D_MODEL)
    ys = x_sample.reshape(DB * T, D_MODEL)
    outs_p, outs_s = [], []
    for l in range(ffn1_w_gate.shape[0]):
        w_attn, w_gdn, w_gate, w_small = _split_w_in(w_in[l])
        wts = dict(
            ffn1_g=ffn1_w_gate[l].astype(BF16), ffn1_u=ffn1_w_up[l].astype(BF16), ffn1_d=ffn1_w_down[l].astype(BF16),
            ffn2_g=ffn2_w_gate[l].astype(BF16), ffn2_u=ffn2_w_up[l].astype(BF16), ffn2_d=ffn2_w_down[l].astype(BF16),
            ln1_g=ln1_g[l], ln1_b=ln1_b[l], ln2_g=ln2_g[l], ln2_b=ln2_b[l], ln3_g=ln3_g[l], ln3_b=ln3_b[l],
            w_attn=w_attn, w_gdn=w_gdn, w_gate=w_gate, w_small=w_small, w_o=w_o[l].astype(BF16), conv_w=conv_w[l],
            a_log=a_log[l], dt_bias=dt_bias[l], gdn_norm_g=gdn_norm_g[l],
        )

        def dsa_p(v_rows, q_bf, k_rot, k_bf, v_bf, qi_bf, small_rot, small_bf):
            return _dsa_prompt(q_bf, qi_bf, small_rot, k_bf, v_bf, small_bf, B, S, min(S, 256))

        def dsa_s(v_rows, q_bf, k_rot, k_bf, v_bf, qi_bf, small_rot, small_bf, l=l):
            w0 = SMALL_OFF['w_idx']
            w3 = small_rot[:, w0:w0 + IDX_HEADS].reshape(DB, IDX_HEADS, 1)
            qi3 = qi_bf.reshape(DB, IDX_HEADS, IDX_DIM)
            knew_i = small_bf[:, :IDX_DIM].reshape(DB, IDX_DIM, 1)
            width = N_KV_HEADS * HEAD_DIM
            past, new = _sample_scores(page_table, qi3, w3, knew_i, jnp.swapaxes(cache_idx_k, 2, 3), l)
            scores3 = jnp.concatenate([past, new], axis=1)
            n_keys = n_pages * PAGE_SIZE + T
            ktop = min(TOPK_MAX, n_keys // 4)
            thr, jmax, take_all = _sample_threshold(scores3.reshape(DB, -1), ktop, n_keys)
            o = _sample_attn(page_table, q_bf.reshape(DB, N_HEADS, HEAD_DIM), scores3, thr, jmax, take_all,
                             k_rot.reshape(DB, 1, width), v_rows.reshape(DB, 1, width), cache_k, cache_v, l)
            return o.reshape(DB, D_MODEL)

        conv0 = jnp.zeros((B, CONV_W - 1, CONV_DIM), F32)
        ssm_zero = jnp.zeros((B, GDN_HEADS, GDN_DK, GDN_DV), F32)
        yp, st_p = _layer(yp, B, S, jnp.arange(S, dtype=jnp.int32), wts, conv0, ssm_zero, dsa_p)
        past_len = n_pages * PAGE_SIZE
        ys, st_s = _layer(ys, DB, T, past_len + jnp.arange(T, dtype=jnp.int32), wts, state_conv[l], state_ssm[l], dsa_s)
        outs_p.append(st_p)
        outs_s.append(st_s)

    def stack(outs, nb, nt):
        d = len(outs)
        k, v, ki, ssm, conv = [a[0][None] if d == 1 else jnp.stack(a) for a in zip(*outs)]
        return (k.reshape(d, nb, nt, N_KV_HEADS, HEAD_DIM), v.reshape(d, nb, nt, N_KV_HEADS, HEAD_DIM),
                ki.reshape(d, nb, nt, IDX_DIM), ssm, conv)

    return (yp.reshape(B, S, D_MODEL), ys.reshape(DB, T, D_MODEL)) + stack(outs_p, B, S) + stack(outs_s, DB, T)
```

```python
import functools

import jax
import jax.numpy as jnp
import numpy as np
from jax import lax
from jax.experimental import pallas as pl
from jax.experimental.pallas import tpu as pltpu

D_MODEL = 2048
PAGE_SIZE = 128
HEAD_DIM = 128
N_HEADS = D_MODEL // HEAD_DIM
N_KV_HEADS = 4
GROUP = N_HEADS // N_KV_HEADS
ROPE_DIM = HEAD_DIM // 4
IDX_HEADS = 16
IDX_DIM = 64
IDX_ROPE_DIM = IDX_DIM // 4
TOPK_MAX = 256
ROPE_THETA = 500000.0
GDN_DK = 128
GDN_DV = 128
GDN_HEADS = D_MODEL // GDN_DV
CONV_W = 4
CONV_DIM = 2 * GDN_HEADS * GDN_DK + GDN_HEADS * GDN_DV
CHUNK = 64
LN_EPS = 1e-5
NORM_EPS = 1e-6
DEPTH = 1
DN_ALPHA = (2 * DEPTH) ** 0.25

IN_SPLITS = (
    ('q_a', N_HEADS * HEAD_DIM), ('k_a', N_KV_HEADS * HEAD_DIM), ('v_a', N_KV_HEADS * HEAD_DIM),
    ('q_idx', IDX_HEADS * IDX_DIM), ('k_idx', IDX_DIM), ('w_idx', IDX_HEADS),
    ('qkv_b', CONV_DIM), ('a_b', GDN_HEADS), ('beta_b', GDN_HEADS), ('z_b', GDN_HEADS * GDN_DV),
    ('gate_a', D_MODEL), ('gate_b', D_MODEL),
)
ATTN_ORDER = ('q_a', 'k_a', 'v_a', 'q_idx')
GDN_ORDER = ('qkv_b',)
GATE_ORDER = ('z_b', 'gate_a', 'gate_b')
SMALL_ORDER = ('k_idx', 'w_idx', 'a_b', 'beta_b')
LANES = 128
VMEM_LIMIT = 56 * 1024 * 1024
NEG_BIG = -1e30

F32 = jnp.float32
BF16 = jnp.bfloat16


def _offsets(order):
    sizes = dict(IN_SPLITS)
    offs, o = {}, 0
    for nm in order:
        offs[nm] = o
        o += sizes[nm]
    return offs, o


ATTN_OFF, ATTN_COLS = _offsets(ATTN_ORDER)
GATE_OFF, _ = _offsets(GATE_ORDER)
SMALL_OFF, SMALL_USED = _offsets(SMALL_ORDER)
SIZES = dict(IN_SPLITS)


def _cparams(*sem):
    return pltpu.CompilerParams(dimension_semantics=sem, vmem_limit_bytes=VMEM_LIMIT)


def _dot(a, b):
    return jnp.dot(a, b, preferred_element_type=F32)


def _dot_nt(a, b):
    return lax.dot_general(a, b, (((1,), (1,)), ((), ())), preferred_element_type=F32)


def _dot_hi(a, b):
    return jnp.dot(a, b, preferred_element_type=F32, precision=lax.Precision.HIGHEST)


def _dot_nt_hi(a, b):
    return lax.dot_general(a, b, (((1,), (1,)), ((), ())), preferred_element_type=F32,
                           precision=lax.Precision.HIGHEST)


def _dot_bf16(a, b):
    return _dot(a.astype(BF16), b.astype(BF16))


def _silu(x):
    return x * jax.nn.sigmoid(x)


def _layer_norm(y, g, b):
    mu = jnp.mean(y, axis=-1, keepdims=True)
    d = y - mu
    var = jnp.mean(d * d, axis=-1, keepdims=True)
    return d * lax.rsqrt(var + LN_EPS) * g + b


def _ffn_ln_kernel(x_ref, wg_ref, wu_ref, wd_ref, g_ref, b_ref, o_ref, acc_ref, xb_ref):
    j = pl.program_id(1)

    @pl.when(j == 0)
    def _():
        acc_ref[...] = jnp.zeros_like(acc_ref)
        xb_ref[...] = x_ref[...].astype(BF16)

    xb = xb_ref[...]
    hg = _dot(xb, wg_ref[...])
    hu = _dot(xb, wu_ref[...])
    h = _silu(hg) * hu
    acc_ref[...] += _dot(h.astype(BF16), wd_ref[...])

    @pl.when(j == pl.num_programs(1) - 1)
    def _():
        y = DN_ALPHA * x_ref[...] + 0.5 * acc_ref[...]
        o_ref[...] = _layer_norm(y, g_ref[...], b_ref[...])


def _ffn_ln(x, wg, wu, wd, g, b, tm, tf):
    M, D = x.shape
    F = wg.shape[1]
    return pl.pallas_call(
        _ffn_ln_kernel,
        grid=(M // tm, F // tf),
        in_specs=[
            pl.BlockSpec((tm, D), lambda i, j: (i, 0)),
            pl.BlockSpec((D, tf), lambda i, j: (0, j)),
            pl.BlockSpec((D, tf), lambda i, j: (0, j)),
            pl.BlockSpec((tf, D), lambda i, j: (j, 0)),
            pl.BlockSpec((1, D), lambda i, j: (0, 0)),
            pl.BlockSpec((1, D), lambda i, j: (0, 0)),
        ],
        out_specs=pl.BlockSpec((tm, D), lambda i, j: (i, 0)),
        out_shape=jax.ShapeDtypeStruct((M, D), F32),
        scratch_shapes=[pltpu.VMEM((tm, D), F32), pltpu.VMEM((tm, D), BF16)],
        compiler_params=_cparams("parallel", "arbitrary"),
        name="ffn_ln",
    )(x, wg, wu, wd, g, b)


def _proj_kernel(x_ref, w_ref, o_ref, xb_ref):
    @pl.when(pl.program_id(1) == 0)
    def _():
        xb_ref[...] = x_ref[...].astype(BF16)

    o_ref[...] = _dot(xb_ref[...], w_ref[...])


def _proj(x, w, tm, tn):
    M, K = x.shape
    N = w.shape[1]
    return pl.pallas_call(
        _proj_kernel,
        grid=(M // tm, N // tn),
        in_specs=[
            pl.BlockSpec((tm, K), lambda i, j: (i, 0)),
            pl.BlockSpec((K, tn), lambda i, j: (0, j)),
        ],
        out_specs=pl.BlockSpec((tm, tn), lambda i, j: (i, j)),
        out_shape=jax.ShapeDtypeStruct((M, N), F32),
        scratch_shapes=[pltpu.VMEM((tm, K), BF16)],
        compiler_params=_cparams("parallel", "arbitrary"),
        name="in_proj",
    )(x, w)


def _rope_tables(pos, rot_dim, period, live_lanes=LANES):
    half = rot_dim // 2
    inv = ROPE_THETA ** (-jnp.arange(half, dtype=F32) / half)
    ang = pos.astype(F32)[:, None] * inv[None, :]
    cos, sin = jnp.cos(ang), jnp.sin(ang)
    lane = np.arange(LANES)
    lp = lane % period
    idx = lp % half
    live = (lp < rot_dim) & (lane < live_lanes)
    c = jnp.where(live[None, :], cos[:, idx], 1.0)
    s = jnp.where(live[None, :], jnp.where((lp < half)[None, :], -sin[:, idx], sin[:, idx]), 0.0)
    return c.astype(F32), s.astype(F32)


def _rope_tile(x, c, s, half, period):
    lane = lax.broadcasted_iota(jnp.int32, x.shape, 1)
    first = (lane & (period - 1)) < half
    partner = jnp.where(first, pltpu.roll(x, LANES - half, 1), pltpu.roll(x, half, 1))
    return x * c + partner * s


def _rope_kernel(q_ref, k_ref, v_ref, qi_ref, sm_ref, ca_ref, sa_ref, ci_ref, si_ref, cs_ref, ss_ref,
                 qo_ref, ko_ref, kb_ref, vb_ref, qio_ref, smo_ref, smb_ref):
    ca, sa = ca_ref[...], sa_ref[...]
    ci, si = ci_ref[...], si_ref[...]
    for h in range(N_HEADS):
        sl = slice(h * LANES, (h + 1) * LANES)
        qo_ref[:, sl] = (_rope_tile(q_ref[:, sl], ca, sa, ROPE_DIM // 2, HEAD_DIM) * (HEAD_DIM ** -0.5)).astype(BF16)
    for h in range(N_KV_HEADS):
        sl = slice(h * LANES, (h + 1) * LANES)
        kr = _rope_tile(k_ref[:, sl], ca, sa, ROPE_DIM // 2, HEAD_DIM)
        ko_ref[:, sl] = kr
        kb_ref[:, sl] = kr.astype(BF16)
    vb_ref[...] = v_ref[...].astype(BF16)
    for h in range(IDX_HEADS * IDX_DIM // LANES):
        sl = slice(h * LANES, (h + 1) * LANES)
        qio_ref[:, sl] = _rope_tile(qi_ref[:, sl], ci, si, IDX_ROPE_DIM // 2, IDX_DIM).astype(BF16)
    sm = _rope_tile(sm_ref[...], cs_ref[...], ss_ref[...], IDX_ROPE_DIM // 2, IDX_DIM)
    smo_ref[...] = sm
    smb_ref[...] = sm.astype(BF16)


def _rope_prep(u_attn, u_small, tabs, tm):
    M = u_attn.shape[0]
    tpos = tabs[0].shape[0]
    nt = tpos // tm

    def col(name):
        w = SIZES[name]
        return pl.BlockSpec((tm, w), lambda i, o=ATTN_OFF[name] // w: (i, o))

    tab_spec = pl.BlockSpec((tm, LANES), lambda i: (i % nt, 0))
    row = lambda w: pl.BlockSpec((tm, w), lambda i: (i, 0))
    return pl.pallas_call(
        _rope_kernel,
        grid=(M // tm,),
        in_specs=[col('q_a'), col('k_a'), col('v_a'), col('q_idx'), row(LANES)] + [tab_spec] * 6,
        out_specs=[row(SIZES['q_a']), row(SIZES['k_a']), row(SIZES['k_a']), row(SIZES['v_a']),
                   row(SIZES['q_idx']), row(LANES), row(LANES)],
        out_shape=[
            jax.ShapeDtypeStruct((M, SIZES['q_a']), BF16),
            jax.ShapeDtypeStruct((M, SIZES['k_a']), F32),
            jax.ShapeDtypeStruct((M, SIZES['k_a']), BF16),
            jax.ShapeDtypeStruct((M, SIZES['v_a']), BF16),
            jax.ShapeDtypeStruct((M, SIZES['q_idx']), BF16),
            jax.ShapeDtypeStruct((M, LANES), F32),
            jax.ShapeDtypeStruct((M, LANES), BF16),
        ],
        compiler_params=_cparams("parallel"),
        name="rope_prep",
    )(u_attn, u_attn, u_attn, u_attn, u_small, *tabs)


BISECT_UNROLL = 4


def _count(pred):
    return jnp.sum(jnp.where(pred, 1.0, 0.0), axis=-1, keepdims=True)


def _tie_index(score, kidx, thr):
    return jnp.where(score == thr, kidx, jnp.int32(2 ** 31 - 1))


def _topk_threshold(score, kidx, n_allowed, k, idx_bits):
    rows = score.shape[0]
    take_all = n_allowed <= k
    lo0 = jnp.min(jnp.where(score == -jnp.inf, jnp.inf, score), axis=-1, keepdims=True)
    hi0 = jnp.max(score, axis=-1, keepdims=True)
    lo0 = jnp.where(take_all, 0.0, lo0)
    hi0 = jnp.where(take_all, 0.0, hi0)

    def step(lo, hi, n_lo):
        mid = 0.5 * lo + 0.5 * hi
        n_mid = _count(score >= mid)
        ge = n_mid >= k
        return jnp.where(ge, mid, lo), jnp.where(ge, hi, mid), jnp.where(ge, n_mid, n_lo)

    def body(state):
        lo, hi, n_lo, _ = state
        for _ in range(BISECT_UNROLL):
            lo, hi, n_lo = step(lo, hi, n_lo)
        mid = 0.5 * lo + 0.5 * hi
        still_open = jnp.max(jnp.where((mid > lo) & (mid < hi) & (n_lo > k), 1.0, 0.0))
        return lo, hi, n_lo, still_open

    n_lo0 = jnp.where(take_all, float(k), n_allowed.astype(F32))
    lo, hi, _, _ = lax.while_loop(lambda state: state[3] > 0.5, body, (lo0, hi0, n_lo0, jnp.float32(1.0)))
    thr = jnp.where(_count(score >= hi) >= k, hi, lo)
    tie = _tie_index(score, kidx, thr)

    def tie_search():
        need = k - _count(score > thr)

        def ibody(t, j):
            cand = j + jnp.left_shift(jnp.int32(1), idx_bits - 1 - t)
            return jnp.where(_count(tie < cand) < need, cand, j)

        return lax.fori_loop(0, idx_bits, ibody, jnp.zeros((rows, 1), jnp.int32))

    repeated = jnp.max(jnp.where(take_all, 0.0, _count(score == thr))) > 1.5
    jmax = lax.cond(repeated, tie_search, lambda: jnp.full((rows, 1), 2 ** 31 - 2, jnp.int32))
    return thr, jmax, take_all


def _selected(score, kidx, thr, jmax, take_all):
    return take_all | (score > thr) | (_tie_index(score, kidx, thr) <= jmax)


def _dsa_prompt_kernel(q_ref, qi_ref, sm_ref, k_ref, v_ref, kis_ref, o_ref, *, ktop, key_step):
    tq = q_ref.shape[0]
    S = k_ref.shape[0]
    i = pl.program_id(1)
    w0 = SMALL_OFF['w_idx']

    def attend(L):
        ki = kis_ref[:L, :IDX_DIM]
        w = sm_ref[:, w0:w0 + IDX_HEADS] * (IDX_HEADS ** -0.5 * IDX_DIM ** -0.5)
        head_lane = lax.broadcasted_iota(jnp.int32, (tq, IDX_HEADS), 1)
        per_trip = 4
        width = per_trip * IDX_DIM

        def idx_heads(t, score):
            qi = qi_ref[:, pl.ds(pl.multiple_of(t * width, width), width)]
            for j in range(per_trip):
                s = _dot_nt(qi[:, j * IDX_DIM:(j + 1) * IDX_DIM], ki)
                w_h = jnp.sum(jnp.where(head_lane == t * per_trip + j, w, 0.0), axis=-1, keepdims=True)
                score = score + w_h * jnp.maximum(s, 0.0)
            return score

        score = lax.fori_loop(0, IDX_HEADS // per_trip, idx_heads, jnp.zeros((tq, L), F32))
        qpos = i * tq + lax.broadcasted_iota(jnp.int32, (tq, 1), 0)
        kidx = lax.broadcasted_iota(jnp.int32, (tq, L), 1)
        allowed = kidx <= qpos
        score = jnp.where(allowed, score, -jnp.inf)
        thr, jmax, take_all = _topk_threshold(score, kidx, qpos + 1, ktop, int(L - 1).bit_length())
        bias = jnp.where(_selected(score, kidx, thr, jmax, take_all) & allowed, 0.0, -jnp.inf)
        def head(h):
            sl = pl.ds(pl.multiple_of(h * HEAD_DIM, HEAD_DIM), HEAD_DIM)
            kv = pl.ds(pl.multiple_of((h // GROUP) * HEAD_DIM, HEAD_DIM), HEAD_DIM)
            s = _dot_nt(q_ref[:, sl], k_ref[:L, kv]) + bias
            m = jnp.max(s, axis=-1, keepdims=True)
            p = jnp.exp(s - m)
            l = jnp.sum(p, axis=-1, keepdims=True)
            o_ref[:, sl] = _dot(p.astype(BF16), v_ref[:L, kv]) / l

        def pair(t, carry):
            head(2 * t)
            head(2 * t + 1)
            return carry

        lax.fori_loop(0, N_HEADS // 2, pair, 0)

    level = ((i + 1) * tq - 1) // key_step
    for lv in range(S // key_step):
        pl.when(level == lv)(functools.partial(attend, (lv + 1) * key_step))


def _dsa_prompt(q_bf, qi_bf, small_rot, k_bf, v_bf, small_bf, B, S, tq):
    ktop = min(TOPK_MAX, S // 4)
    nq = S // tq
    key_step = min(S, 256)
    row = lambda w: pl.BlockSpec((tq, w), lambda b, i: (b * nq + i, 0))
    full = lambda w: pl.BlockSpec((S, w), lambda b, i: (b, 0))
    return pl.pallas_call(
        functools.partial(_dsa_prompt_kernel, ktop=ktop, key_step=key_step),
        grid=(B, nq),
        in_specs=[row(q_bf.shape[1]), row(qi_bf.shape[1]), row(LANES),
                  full(k_bf.shape[1]), full(v_bf.shape[1]), full(LANES)],
        out_specs=row(q_bf.shape[1]),
        out_shape=jax.ShapeDtypeStruct((B * S, q_bf.shape[1]), F32),
        compiler_params=_cparams("parallel", "arbitrary"),
        name="dsa_prompt",
    )(q_bf, qi_bf, small_rot, k_bf, v_bf, small_bf)


def _idx_score_rows(qi, w, kpage_t):
    s = _dot(qi, kpage_t)
    return jnp.sum(w * jnp.maximum(s, 0.0), axis=0, keepdims=True)


def _sample_scores_kernel(pt_ref, qi_ref, w_ref, knew_ref, *refs):
    page_refs, (o_ref, onew_ref) = refs[:-2], refs[-2:]
    npp = len(page_refs)
    p = pl.program_id(1)
    qi = qi_ref[0]
    w = w_ref[0] * (IDX_HEADS ** -0.5 * IDX_DIM ** -0.5)
    for j, page_ref in enumerate(page_refs):
        o_ref[0, pl.ds(p * npp + j, 1), :] = _idx_score_rows(qi, w, page_ref[0, 0].astype(BF16))

    @pl.when(p == 0)
    def _():
        kn = jnp.broadcast_to(knew_ref[0], (IDX_DIM, PAGE_SIZE))
        sc = _idx_score_rows(qi, w, kn)
        lane = lax.broadcasted_iota(jnp.int32, (1, PAGE_SIZE), 1)
        onew_ref[0] = jnp.where(lane == 0, sc, -jnp.inf)


def _page_specs(page_shape, npp, layer):
    zeros = (0,) * len(page_shape)
    return [pl.BlockSpec((1, 1) + page_shape, lambda b, p, pt, j=j: (layer, pt[b, p * npp + j]) + zeros)
            for j in range(npp)]


def _pages_per_step(n_pages, cap):
    npp = min(cap, n_pages)
    while n_pages % npp:
        npp -= 1
    return npp


def _sample_scores(page_table, qi3, w3, knew3, cache_ki, layer):
    DB, n_pages = page_table.shape
    npp = _pages_per_step(n_pages, 16)
    return pl.pallas_call(
        _sample_scores_kernel,
        grid_spec=pltpu.PrefetchScalarGridSpec(
            num_scalar_prefetch=1,
            grid=(DB, n_pages // npp),
            in_specs=[
                pl.BlockSpec((1, IDX_HEADS, IDX_DIM), lambda b, p, pt: (b, 0, 0)),
                pl.BlockSpec((1, IDX_HEADS, 1), lambda b, p, pt: (b, 0, 0)),
                pl.BlockSpec((1, IDX_DIM, 1), lambda b, p, pt: (b, 0, 0)),
            ] + _page_specs((IDX_DIM, PAGE_SIZE), npp, layer),
            out_specs=[
                pl.BlockSpec((1, n_pages, PAGE_SIZE), lambda b, p, pt: (b, 0, 0)),
                pl.BlockSpec((1, 1, PAGE_SIZE), lambda b, p, pt: (b, 0, 0)),
            ],
        ),
        out_shape=[jax.ShapeDtypeStruct((DB, n_pages, PAGE_SIZE), F32),
                   jax.ShapeDtypeStruct((DB, 1, PAGE_SIZE), F32)],
        compiler_params=_cparams("parallel", "arbitrary"),
        name="sample_scores",
    )(page_table, qi3, w3, knew3, *([cache_ki] * npp))


def _sample_thr_kernel(s_ref, thr_ref, jmax_ref, all_ref, *, ktop, n_valid, idx_bits):
    kidx = lax.broadcasted_iota(jnp.int32, s_ref.shape, 1)
    score = jnp.where(kidx < n_valid, s_ref[...], -jnp.inf)
    n_allowed = jnp.full((s_ref.shape[0], 1), n_valid, jnp.int32)
    thr, jmax, take_all = _topk_threshold(score, kidx, n_allowed, ktop, idx_bits)
    thr_ref[...] = jnp.broadcast_to(thr, thr_ref.shape)
    jmax_ref[...] = jnp.broadcast_to(jmax, jmax_ref.shape)
    all_ref[...] = jnp.broadcast_to(take_all.astype(jnp.int32), all_ref.shape)


def _sample_threshold(scores, ktop, n_valid):
    DB, L = scores.shape
    out = lambda dt: jax.ShapeDtypeStruct((DB, LANES), dt)
    return pl.pallas_call(
        functools.partial(_sample_thr_kernel, ktop=ktop, n_valid=n_valid, idx_bits=int(L - 1).bit_length()),
        out_shape=[out(F32), out(jnp.int32), out(jnp.int32)],
        compiler_params=pltpu.CompilerParams(vmem_limit_bytes=VMEM_LIMIT),
        name="sample_threshold",
    )(scores)


def _sample_attn_kernel(pt_ref, q_ref, s_ref, thr_ref, jmax_ref, all_ref, knew_ref, vnew_ref, *refs, n_pages):
    npp = (len(refs) - 4) // 2
    kp_refs, vp_refs = refs[:npp], refs[npp:2 * npp]
    o_ref, m_ref, l_ref, acc_ref = refs[2 * npp:]
    b = pl.program_id(0)
    p = pl.program_id(1)

    @pl.when(p == 0)
    def _():
        m_ref[...] = jnp.full_like(m_ref, NEG_BIG)
        l_ref[...] = jnp.zeros_like(l_ref)
        acc_ref[...] = jnp.zeros_like(acc_ref)

    thr = thr_ref[pl.ds(b, 1), 0:1]
    jmax = jmax_ref[pl.ds(b, 1), 0:1]
    take_all = all_ref[pl.ds(b, 1), 0:1] > 0

    q = q_ref[0]
    group = lax.broadcasted_iota(jnp.int32, (N_HEADS, 1), 0) // GROUP
    X = PAGE_SIZE * N_KV_HEADS
    own_head = (lax.broadcasted_iota(jnp.int32, (N_HEADS, X), 1) % N_KV_HEADS) == group
    repeat = (lax.broadcasted_iota(jnp.int32, (PAGE_SIZE, X), 1) // N_KV_HEADS
              == lax.broadcasted_iota(jnp.int32, (PAGE_SIZE, X), 0)).astype(BF16)

    def fold(logits, pv_fns):
        m_old = m_ref[...]
        m_new = functools.reduce(jnp.maximum, [jnp.max(s, axis=-1, keepdims=True) for s in logits], m_old)
        corr = jnp.exp(m_old - m_new)
        probs = [jnp.exp(s - m_new) for s in logits]
        l_ref[...] = l_ref[...] * corr + functools.reduce(
            lambda a, c: a + c, [jnp.sum(pr, axis=-1, keepdims=True) for pr in probs])
        acc_ref[...] = acc_ref[...] * corr + functools.reduce(
            lambda a, c: a + c, [fn(pr) for fn, pr in zip(pv_fns, probs)])
        m_ref[...] = m_new

    logits, pv_fns = [], []
    for j in range(npp):
        page = p * npp + j
        scores_row = s_ref[0, pl.ds(page, 1), :]
        kidx = page * PAGE_SIZE + lax.broadcasted_iota(jnp.int32, (1, PAGE_SIZE), 1)
        sel = _selected(scores_row, kidx, thr, jmax, take_all)
        sel_rows = _dot(jnp.broadcast_to(jnp.where(sel, 1.0, 0.0), (N_HEADS, PAGE_SIZE)).astype(BF16), repeat)
        s = _dot_nt(q, kp_refs[j][0].astype(BF16))
        logits.append(jnp.where(own_head & (sel_rows > 0.5), s, -jnp.inf))
        pv_fns.append(lambda pr, ref=vp_refs[j]: _dot(pr.astype(BF16), ref[0].astype(BF16)))
    fold(logits, pv_fns)

    @pl.when(p == pl.num_programs(1) - 1)
    def _():
        def per_head(ref):
            rows = [jnp.where(group == n, jnp.broadcast_to(ref[0, :, n * HEAD_DIM:(n + 1) * HEAD_DIM],
                                                           (N_HEADS, HEAD_DIM)), 0.0) for n in range(N_KV_HEADS)]
            return functools.reduce(lambda a, c: a + c, rows).astype(BF16).astype(F32)

        score_new = s_ref[0, pl.ds(n_pages, 1), 0:1]
        sel = _selected(score_new, jnp.full((1, 1), n_pages * PAGE_SIZE, jnp.int32), thr, jmax, take_all)
        s = jnp.sum(q.astype(F32) * per_head(knew_ref), axis=-1, keepdims=True)
        s = jnp.where(sel, s, -jnp.inf)
        vexp = per_head(vnew_ref)
        fold([s], [lambda pr: pr.astype(BF16).astype(F32) * vexp])
        o_ref[0] = acc_ref[...] / l_ref[...]


def _sample_attn(page_table, q3, scores3, thr, jmax, take_all, knew3, vnew3, cache_k, cache_v, layer):
    DB, n_pages = page_table.shape
    width = N_KV_HEADS * HEAD_DIM
    npp = _pages_per_step(n_pages, 32)
    bsel = lambda *shape: pl.BlockSpec((1,) + shape, lambda b, p, pt: (b,) + (0,) * len(shape))
    whole = pl.BlockSpec((DB, LANES), lambda b, p, pt: (0, 0))
    n_phys = cache_k.shape[1]
    rows = PAGE_SIZE * N_KV_HEADS
    cache_k = cache_k.reshape(-1, rows, HEAD_DIM)
    cache_v = cache_v.reshape(-1, rows, HEAD_DIM)
    pages = [pl.BlockSpec((1, rows, HEAD_DIM),
                          lambda b, p, pt, j=j: (layer * n_phys + pt[b, p * npp + j], 0, 0)) for j in range(npp)]
    return pl.pallas_call(
        functools.partial(_sample_attn_kernel, n_pages=n_pages),
        grid_spec=pltpu.PrefetchScalarGridSpec(
            num_scalar_prefetch=1,
            grid=(DB, n_pages // npp),
            in_specs=[bsel(N_HEADS, HEAD_DIM), bsel(n_pages + 1, PAGE_SIZE), whole, whole, whole,
                      bsel(1, width), bsel(1, width)] + pages + pages,
            out_specs=bsel(N_HEADS, HEAD_DIM),
            scratch_shapes=[pltpu.VMEM((N_HEADS, 1), F32), pltpu.VMEM((N_HEADS, 1), F32),
                            pltpu.VMEM((N_HEADS, HEAD_DIM), F32)],
        ),
        out_shape=jax.ShapeDtypeStruct((DB, N_HEADS, HEAD_DIM), F32),
        compiler_params=_cparams("parallel", "arbitrary"),
        name="sample_attn",
    )(page_table, q3, scores3, thr, jmax, take_all, knew3, vnew3, *([cache_k] * npp), *([cache_v] * npp))


CARRY = 8
MXU_DIM = 256
GDN_GROUP = MXU_DIM // CHUNK


def _spread(a, row_head, hg):
    return jnp.concatenate([jnp.where(row_head == i, a, 0.0) for i in range(hg)], axis=1)


def _gdn_prep_kernel(x_ref, w_ref, buf_ref, q_ref, k_ref, v_ref, conv_ref, xpad_ref):
    tt = x_ref.shape[1]
    t = pl.program_id(1)
    lo = CARRY - (CONV_W - 1)

    @pl.when(t == 0)
    def _():
        xpad_ref[lo:CARRY, :] = buf_ref[0]

    xpad_ref[CARRY:CARRY + tt, :] = x_ref[0]
    nh = GDN_HEADS
    for c in range(CONV_DIM // LANES):
        sl = slice(c * LANES, (c + 1) * LANES)
        y = w_ref[0:1, sl] * xpad_ref[lo:lo + tt, sl]
        for j in range(1, CONV_W):
            y = y + w_ref[j:j + 1, sl] * xpad_ref[lo + j:lo + j + tt, sl]
        y = _silu(y)
        if c < 2 * nh:
            y = y * lax.rsqrt(jnp.sum(y * y, axis=-1, keepdims=True) + NORM_EPS)
        if c < nh:
            q_ref[0, :, sl] = y * (GDN_DK ** -0.5)
        elif c < 2 * nh:
            k_ref[0, :, slice((c - nh) * LANES, (c - nh + 1) * LANES)] = y
        else:
            v_ref[0, :, slice((c - 2 * nh) * LANES, (c - 2 * nh + 1) * LANES)] = y
    last = xpad_ref[lo + tt:CARRY + tt, :]
    xpad_ref[lo:CARRY, :] = last

    @pl.when(t == pl.num_programs(1) - 1)
    def _():
        conv_ref[0] = last


def _gdn_prep(u_gdn3, conv_w, buf, tt):
    B, T, _ = u_gdn3.shape
    w = SIZES['qkv_b']
    hd = GDN_HEADS * GDN_DK
    out = jax.ShapeDtypeStruct((B, T, hd), F32)
    ospec = pl.BlockSpec((1, tt, hd), lambda b, t: (b, t, 0))
    return pl.pallas_call(
        _gdn_prep_kernel,
        grid=(B, T // tt),
        in_specs=[
            pl.BlockSpec((1, tt, w), lambda b, t: (b, t, 0)),
            pl.BlockSpec((CONV_W, w), lambda b, t: (0, 0)),
            pl.BlockSpec((1, CONV_W - 1, w), lambda b, t: (b, 0, 0)),
        ],
        out_specs=[ospec, ospec, ospec, pl.BlockSpec((1, CONV_W - 1, w), lambda b, t: (b, 0, 0))],
        out_shape=[out, out, out, jax.ShapeDtypeStruct((B, CONV_W - 1, w), F32)],
        scratch_shapes=[pltpu.VMEM((CARRY + tt, w), F32)],
        compiler_params=_cparams("parallel", "arbitrary"),
        name="gdn_prep",
    )(u_gdn3, conv_w, buf)


def _gdn_chunk_kernel(q_ref, k_ref, v_ref, z_ref, sm_ref, alog_ref, dtb_ref, gn_ref, s0_ref,
                      o_ref, sout_ref, state_ref, *, t_valid):
    C = q_ref.shape[1]
    c = pl.program_id(1)

    @pl.when(c == 0)
    def _():
        state_ref[...] = s0_ref[0]

    H = GDN_HEADS
    a0, b0 = SMALL_OFF['a_b'], SMALL_OFF['beta_b']
    live = (c * C + lax.broadcasted_iota(jnp.int32, (C, H), 0)) < t_valid
    xs = sm_ref[0, :, a0:a0 + H] + dtb_ref[...]
    softplus = jnp.maximum(xs, 0.0) + jnp.log1p(jnp.exp(-jnp.abs(xs)))
    g_all = jnp.where(live, -jnp.exp(alog_ref[...]) * softplus, 0.0)
    beta_all = jnp.where(live, jax.nn.sigmoid(sm_ref[0, :, b0:b0 + H]), 0.0)
    tri_f = (lax.broadcasted_iota(jnp.int32, (C, C), 0) >= lax.broadcasted_iota(jnp.int32, (C, C), 1)).astype(F32)
    eye_h = (lax.broadcasted_iota(jnp.int32, (H, H), 0) == lax.broadcasted_iota(jnp.int32, (H, H), 1)).astype(F32)
    gcum = _dot_hi(tri_f, g_all)
    gcum_t = _dot_nt_hi(eye_h, gcum)

    HG = GDN_GROUP
    R = HG * C
    ri = lax.broadcasted_iota(jnp.int32, (R, R), 0)
    ci = lax.broadcasted_iota(jnp.int32, (R, R), 1)
    same_head = (ri // C) == (ci // C)
    mask_incl = same_head & (ri >= ci)
    mask_strict = same_head & (ri > ci)
    eye = (ri == ci).astype(F32)
    row_head = lax.broadcasted_iota(jnp.int32, (R, 1), 0) // C
    row_head2 = jnp.concatenate([row_head, row_head], axis=0)

    groups = []
    for grp in range(H // HG):
        heads = range(grp * HG, (grp + 1) * HG)
        sls = [slice(h * GDN_DK, (h + 1) * GDN_DK) for h in heads]
        rows = lambda ref: jnp.concatenate([ref[0, :, sl] for sl in sls], axis=0)
        cols = lambda a: jnp.concatenate([a[:, h:h + 1] for h in heads], axis=0)
        q, k, v = rows(q_ref), rows(k_ref), rows(v_ref)
        beta = cols(beta_all)
        gcol = cols(gcum)
        grow = jnp.concatenate([gcum_t[h:h + 1, :] for h in heads], axis=1)
        glast = jnp.concatenate([jnp.broadcast_to(gcum[C - 1:C, h:h + 1], (C, 1)) for h in heads], axis=0)
        decay = jnp.where(mask_incl, jnp.exp(jnp.where(mask_incl, gcol - grow, 0.0)), 0.0)
        kb = k * beta
        kk_qk = _dot_nt(jnp.concatenate([kb, q], axis=0).astype(BF16), k.astype(BF16))
        nmat = jnp.where(mask_strict, kk_qk[:R] * decay, 0.0)
        groups.append(dict(
            heads=heads, sls=sls, nmat=nmat.astype(BF16), xinv=eye - nmat, qk=(kk_qk[R:] * decay).astype(BF16),
            rhs=jnp.concatenate([v * beta, kb * jnp.exp(gcol)], axis=1).astype(BF16),
            q_dec=q * jnp.exp(gcol), k_dec=k * jnp.exp(glast - gcol)))

    for _ in range(max(C - 1, 1).bit_length() - 1):
        for g in groups:
            g['resid'] = (eye - g['xinv']) - _dot(g['nmat'], g['xinv'].astype(BF16))
        for g in groups:
            g['xinv'] = g['xinv'] + _dot_bf16(g['xinv'], g['resid'])
    for g in groups:
        g['x'] = _dot(g['xinv'].astype(BF16), g['rhs'])
    for grp, g in enumerate(groups):
        u, w = g['x'][:, :GDN_DV], g['x'][:, GDN_DV:]
        s_stack = state_ref[grp * HG:(grp + 1) * HG].reshape(HG * GDN_DK, GDN_DV)
        w_q = jnp.concatenate([w, g['q_dec']], axis=0)
        ws_qs = _dot(_spread(w_q, row_head2, HG).astype(BF16), s_stack.astype(BF16))
        g['v_new'] = u - ws_qs[:R]
        g['o'] = ws_qs[R:]
    for g in groups:
        g['o'] = g['o'] + _dot(g['qk'], g['v_new'].astype(BF16))
        g['s_add'] = _dot(g['k_dec'].T.astype(BF16), _spread(g['v_new'], row_head, HG).astype(BF16))
    for g in groups:
        o = g['o']
        on = o * lax.rsqrt(jnp.mean(o * o, axis=-1, keepdims=True) + NORM_EPS) * gn_ref[...]
        for a, h in enumerate(g['heads']):
            state_ref[h] = (state_ref[h] * jnp.exp(gcum[C - 1:C, h:h + 1])
                            + g['s_add'][:, a * GDN_DV:(a + 1) * GDN_DV])
            o_ref[0, :, g['sls'][a]] = on[a * C:(a + 1) * C] * _silu(z_ref[0, :, g['sls'][a]])

    @pl.when(c == pl.num_programs(1) - 1)
    def _():
        sout_ref[0] = state_ref[...]


def _gdn_chunks(qn, kn, vv, u_gate3, small3, a_log, dt_bias, gn, s0, t_valid):
    B, Tp, hd = qn.shape
    nc = Tp // CHUNK
    blk = pl.BlockSpec((1, CHUNK, hd), lambda b, c: (b, c, 0))
    vec = lambda w: pl.BlockSpec((1, w), lambda b, c: (0, 0))
    st = pl.BlockSpec((1, GDN_HEADS, GDN_DK, GDN_DV), lambda b, c: (b, 0, 0, 0))
    return pl.pallas_call(
        functools.partial(_gdn_chunk_kernel, t_valid=t_valid),
        grid=(B, nc),
        in_specs=[blk, blk, blk,
                  pl.BlockSpec((1, CHUNK, hd), lambda b, c, o=GATE_OFF['z_b'] // hd: (b, c, o)),
                  pl.BlockSpec((1, CHUNK, LANES), lambda b, c: (b, c, 0)),
                  vec(GDN_HEADS), vec(GDN_HEADS), vec(GDN_DV), st],
        out_specs=[blk, st],
        out_shape=[jax.ShapeDtypeStruct((B, Tp, hd), F32),
                   jax.ShapeDtypeStruct((B, GDN_HEADS, GDN_DK, GDN_DV), F32)],
        scratch_shapes=[pltpu.VMEM((GDN_HEADS, GDN_DK, GDN_DV), F32)],
        compiler_params=_cparams("parallel", "arbitrary"),
        name="gdn_chunks",
    )(qn, kn, vv, u_gate3, small3, a_log, dt_bias, gn, s0)


def _merge_kernel(x_ref, oa_ref, ob_ref, ga_ref, gb_ref, wo_ref, g_ref, b_ref, o_ref):
    merged = jax.nn.sigmoid(ga_ref[...]) * oa_ref[...] + jax.nn.sigmoid(gb_ref[...]) * ob_ref[...]
    y = DN_ALPHA * x_ref[...] + _dot(merged.astype(BF16), wo_ref[...])
    o_ref[...] = _layer_norm(y, g_ref[...], b_ref[...])


def _merge_proj_ln(x, o_a, o_b, u_gate, w_o, g, b, tm):
    M, D = x.shape
    row = pl.BlockSpec((tm, D), lambda i: (i, 0))
    col = lambda name: pl.BlockSpec((tm, D), lambda i, o=GATE_OFF[name] // D: (i, o))
    vec = pl.BlockSpec((1, D), lambda i: (0, 0))
    return pl.pallas_call(
        _merge_kernel,
        grid=(M // tm,),
        in_specs=[row, row, row, col('gate_a'), col('gate_b'),
                  pl.BlockSpec((D, D), lambda i: (0, 0)), vec, vec],
        out_specs=row,
        out_shape=jax.ShapeDtypeStruct((M, D), F32),
        compiler_params=_cparams("parallel"),
        name="merge_proj_ln",
    )(x, o_a, o_b, u_gate, u_gate, w_o, g, b)


def _tiles(M):
    if M % 1024 == 0:
        return 512, 1024, 256
    return M, M, M


def _layer(x, B, T, pos, wts, conv_buf, ssm0, dsa_fn):
    M = B * T
    tm, tp, te = _tiles(M)
    row = lambda a: a.reshape(1, -1)
    x1 = _ffn_ln(x, wts['ffn1_g'], wts['ffn1_u'], wts['ffn1_d'], row(wts['ln1_g']), row(wts['ln1_b']),
                 tm, 512)
    u_attn = _proj(x1, wts['w_attn'], tp, 1024)
    u_gdn = _proj(x1, wts['w_gdn'], tp, 1024)
    u_gate = _proj(x1, wts['w_gate'], tp, 1024)
    u_small = _proj(x1, wts['w_small'], tp, LANES)

    pos_rows = pos if T > 1 else jnp.broadcast_to(pos, (M,))
    tabs = (_rope_tables(pos_rows, ROPE_DIM, HEAD_DIM) + _rope_tables(pos_rows, IDX_ROPE_DIM, IDX_DIM)
            + _rope_tables(pos_rows, IDX_ROPE_DIM, IDX_DIM, live_lanes=IDX_DIM))
    q_bf, k_rot, k_bf, v_bf, qi_bf, small_rot, small_bf = _rope_prep(u_attn, u_small, tabs, te)
    v_rows = u_attn[:, ATTN_OFF['v_a']:ATTN_OFF['v_a'] + SIZES['v_a']]
    o_a = dsa_fn(v_rows, q_bf, k_rot, k_bf, v_bf, qi_bf, small_rot, small_bf)

    qn, kn, vv, conv_new = _gdn_prep(u_gdn.reshape(B, T, -1), wts['conv_w'], conv_buf, min(T, 256))
    pad = (-T) % CHUNK
    pad3 = lambda a: jnp.pad(a, ((0, 0), (0, pad), (0, 0))) if pad else a
    o_b, ssm_new = _gdn_chunks(pad3(qn), pad3(kn), pad3(vv), pad3(u_gate.reshape(B, T, -1)),
                               pad3(u_small.reshape(B, T, LANES)),
                               row(wts['a_log']), row(wts['dt_bias']), row(wts['gdn_norm_g']), ssm0, T)
    o_b = o_b[:, :T].reshape(M, D_MODEL)

    x2 = _merge_proj_ln(x1, o_a, o_b, u_gate, wts['w_o'], row(wts['ln2_g']), row(wts['ln2_b']), te)
    y = _ffn_ln(x2, wts['ffn2_g'], wts['ffn2_u'], wts['ffn2_d'], row(wts['ln3_g']), row(wts['ln3_b']),
                tm, 512)
    ki_rows = small_rot[:, SMALL_OFF['k_idx']:SMALL_OFF['k_idx'] + IDX_DIM]
    return y, (k_rot, v_rows, ki_rows, ssm_new, conv_new)


def _split_w_in(w_in):
    offs = dict(zip([nm for nm, _ in IN_SPLITS], np.cumsum([0] + [n for _, n in IN_SPLITS])))

    def span(order):
        lo = offs[order[0]]
        hi = offs[order[-1]] + SIZES[order[-1]]
        assert hi - lo == sum(SIZES[nm] for nm in order)
        return w_in[:, lo:hi].astype(BF16)

    w_small = jnp.concatenate([w_in[:, offs[nm]:offs[nm] + SIZES[nm]] for nm in SMALL_ORDER], axis=1)
    w_small = jnp.pad(w_small, ((0, 0), (0, LANES - SMALL_USED))).astype(BF16)
    return span(ATTN_ORDER), span(GDN_ORDER), span(GATE_ORDER), w_small


def kernel(x_prompt, x_sample, cache_k, cache_v, cache_idx_k, state_ssm, state_conv, page_table, ffn1_w_gate, ffn1_w_up, ffn1_w_down, ln1_g, ln1_b, w_in, conv_w, a_log, dt_bias, gdn_norm_g, w_o, ln2_g, ln2_b, ffn2_w_gate, ffn2_w_up, ffn2_w_down, ln3_g, ln3_b):
    B, S, _ = x_prompt.shape
    DB, T, _ = x_sample.shape
    assert T == 1, "the sample path handles one new token per sequence"
    n_pages = page_table.shape[1]
    yp = x_prompt.reshape(B * S, D_MODEL)
    ys = x_sample.reshape(DB * T, D_MODEL)
    outs_p, outs_s = [], []
    for l in range(ffn1_w_gate.shape[0]):
        w_attn, w_gdn, w_gate, w_small = _split_w_in(w_in[l])
        wts = dict(
            ffn1_g=ffn1_w_gate[l].astype(BF16), ffn1_u=ffn1_w_up[l].astype(BF16), ffn1_d=ffn1_w_down[l].astype(BF16),
            ffn2_g=ffn2_w_gate[l].astype(BF16), ffn2_u=ffn2_w_up[l].astype(BF16), ffn2_d=ffn2_w_down[l].astype(BF16),
            ln1_g=ln1_g[l], ln1_b=ln1_b[l], ln2_g=ln2_g[l], ln2_b=ln2_b[l], ln3_g=ln3_g[l], ln3_b=ln3_b[l],
            w_attn=w_attn, w_gdn=w_gdn, w_gate=w_gate, w_small=w_small, w_o=w_o[l].astype(BF16), conv_w=conv_w[l],
            a_log=a_log[l], dt_bias=dt_bias[l], gdn_norm_g=gdn_norm_g[l],
        )

        def dsa_p(v_rows, q_bf, k_rot, k_bf, v_bf, qi_bf, small_rot, small_bf):
            return _dsa_prompt(q_bf, qi_bf, small_rot, k_bf, v_bf, small_bf, B, S, min(S, 256))

        def dsa_s(v_rows, q_bf, k_rot, k_bf, v_bf, qi_bf, small_rot, small_bf, l=l):
            w0 = SMALL_OFF['w_idx']
            w3 = small_rot[:, w0:w0 + IDX_HEADS].reshape(DB, IDX_HEADS, 1)
            qi3 = qi_bf.reshape(DB, IDX_HEADS, IDX_DIM)
            knew_i = small_bf[:, :IDX_DIM].reshape(DB, IDX_DIM, 1)
            width = N_KV_HEADS * HEAD_DIM
            past, new = _sample_scores(page_table, qi3, w3, knew_i, jnp.swapaxes(cache_idx_k, 2, 3), l)
            scores3 = jnp.concatenate([past, new], axis=1)
            n_keys = n_pages * PAGE_SIZE + T
            ktop = min(TOPK_MAX, n_keys // 4)
            thr, jmax, take_all = _sample_threshold(scores3.reshape(DB, -1), ktop, n_keys)
            o = _sample_attn(page_table, q_bf.reshape(DB, N_HEADS, HEAD_DIM), scores3, thr, jmax, take_all,
                             k_rot.reshape(DB, 1, width), v_rows.reshape(DB, 1, width), cache_k, cache_v, l)
            return o.reshape(DB, D_MODEL)

        conv0 = jnp.zeros((B, CONV_W - 1, CONV_DIM), F32)
        ssm_zero = jnp.zeros((B, GDN_HEADS, GDN_DK, GDN_DV), F32)
        yp, st_p = _layer(yp, B, S, jnp.arange(S, dtype=jnp.int32), wts, conv0, ssm_zero, dsa_p)
        past_len = n_pages * PAGE_SIZE
        ys, st_s = _layer(ys, DB, T, past_len + jnp.arange(T, dtype=jnp.int32), wts, state_conv[l], state_ssm[l], dsa_s)
        outs_p.append(st_p)
        outs_s.append(st_s)

    def stack(outs, nb, nt):
        d = len(outs)
        k, v, ki, ssm, conv = [a[0][None] if d == 1 else jnp.stack(a) for a in zip(*outs)]
        return (k.reshape(d, nb, nt, N_KV_HEADS, HEAD_DIM), v.reshape(d, nb, nt, N_KV_HEADS, HEAD_DIM),
                ki.reshape(d, nb, nt, IDX_DIM), ssm, conv)

    return (yp.reshape(B, S, D_MODEL), ys.reshape(DB, T, D_MODEL)) + stack(outs_p, B, S) + stack(outs_s, DB, T)
```

```python
import functools

import jax
import jax.numpy as jnp
import numpy as np
from jax import lax
from jax.experimental import pallas as pl
from jax.experimental.pallas import tpu as pltpu

D_MODEL = 2048
PAGE_SIZE = 128
HEAD_DIM = 128
N_HEADS = D_MODEL // HEAD_DIM
N_KV_HEADS = 4
GROUP = N_HEADS // N_KV_HEADS
ROPE_DIM = HEAD_DIM // 4
IDX_HEADS = 16
IDX_DIM = 64
IDX_ROPE_DIM = IDX_DIM // 4
TOPK_MAX = 256
ROPE_THETA = 500000.0
GDN_DK = 128
GDN_DV = 128
GDN_HEADS = D_MODEL // GDN_DV
CONV_W = 4
CONV_DIM = 2 * GDN_HEADS * GDN_DK + GDN_HEADS * GDN_DV
CHUNK = 64
LN_EPS = 1e-5
NORM_EPS = 1e-6
DEPTH = 1
DN_ALPHA = (2 * DEPTH) ** 0.25

IN_SPLITS = (
    ('q_a', N_HEADS * HEAD_DIM), ('k_a', N_KV_HEADS * HEAD_DIM), ('v_a', N_KV_HEADS * HEAD_DIM),
    ('q_idx', IDX_HEADS * IDX_DIM), ('k_idx', IDX_DIM), ('w_idx', IDX_HEADS),
    ('qkv_b', CONV_DIM), ('a_b', GDN_HEADS), ('beta_b', GDN_HEADS), ('z_b', GDN_HEADS * GDN_DV),
    ('gate_a', D_MODEL), ('gate_b', D_MODEL),
)
ATTN_ORDER = ('q_a', 'k_a', 'v_a', 'q_idx')
GDN_ORDER = ('qkv_b',)
GATE_ORDER = ('z_b', 'gate_a', 'gate_b')
SMALL_ORDER = ('k_idx', 'w_idx', 'a_b', 'beta_b')
LANES = 128
VMEM_LIMIT = 56 * 1024 * 1024
NEG_BIG = -1e30

F32 = jnp.float32
BF16 = jnp.bfloat16


def _offsets(order):
    sizes = dict(IN_SPLITS)
    offs, o = {}, 0
    for nm in order:
        offs[nm] = o
        o += sizes[nm]
    return offs, o


ATTN_OFF, ATTN_COLS = _offsets(ATTN_ORDER)
GATE_OFF, _ = _offsets(GATE_ORDER)
SMALL_OFF, SMALL_USED = _offsets(SMALL_ORDER)
SIZES = dict(IN_SPLITS)


def _cparams(*sem):
    return pltpu.CompilerParams(dimension_semantics=sem, vmem_limit_bytes=VMEM_LIMIT)


def _dot(a, b):
    return jnp.dot(a, b, preferred_element_type=F32)


def _dot_nt(a, b):
    return lax.dot_general(a, b, (((1,), (1,)), ((), ())), preferred_element_type=F32)


def _dot_hi(a, b):
    return jnp.dot(a, b, preferred_element_type=F32, precision=lax.Precision.HIGHEST)


def _dot_nt_hi(a, b):
    return lax.dot_general(a, b, (((1,), (1,)), ((), ())), preferred_element_type=F32,
                           precision=lax.Precision.HIGHEST)


def _dot_bf16(a, b):
    return _dot(a.astype(BF16), b.astype(BF16))


def _silu(x):
    return x * jax.nn.sigmoid(x)


def _layer_norm(y, g, b):
    mu = jnp.mean(y, axis=-1, keepdims=True)
    d = y - mu
    var = jnp.mean(d * d, axis=-1, keepdims=True)
    return d * lax.rsqrt(var + LN_EPS) * g + b


def _ffn_ln_kernel(x_ref, wg_ref, wu_ref, wd_ref, g_ref, b_ref, o_ref, acc_ref, xb_ref):
    j = pl.program_id(1)

    @pl.when(j == 0)
    def _():
        acc_ref[...] = jnp.zeros_like(acc_ref)
        xb_ref[...] = x_ref[...].astype(BF16)

    xb = xb_ref[...]
    hg = _dot(xb, wg_ref[...])
    hu = _dot(xb, wu_ref[...])
    h = _silu(hg) * hu
    acc_ref[...] += _dot(h.astype(BF16), wd_ref[...])

    @pl.when(j == pl.num_programs(1) - 1)
    def _():
        y = DN_ALPHA * x_ref[...] + 0.5 * acc_ref[...]
        o_ref[...] = _layer_norm(y, g_ref[...], b_ref[...])


def _ffn_ln(x, wg, wu, wd, g, b, tm, tf):
    M, D = x.shape
    F = wg.shape[1]
    return pl.pallas_call(
        _ffn_ln_kernel,
        grid=(M // tm, F // tf),
        in_specs=[
            pl.BlockSpec((tm, D), lambda i, j: (i, 0)),
            pl.BlockSpec((D, tf), lambda i, j: (0, j)),
            pl.BlockSpec((D, tf), lambda i, j: (0, j)),
            pl.BlockSpec((tf, D), lambda i, j: (j, 0)),
            pl.BlockSpec((1, D), lambda i, j: (0, 0)),
            pl.BlockSpec((1, D), lambda i, j: (0, 0)),
        ],
        out_specs=pl.BlockSpec((tm, D), lambda i, j: (i, 0)),
        out_shape=jax.ShapeDtypeStruct((M, D), F32),
        scratch_shapes=[pltpu.VMEM((tm, D), F32), pltpu.VMEM((tm, D), BF16)],
        compiler_params=_cparams("parallel", "arbitrary"),
        name="ffn_ln",
    )(x, wg, wu, wd, g, b)


def _proj_kernel(x_ref, w_ref, o_ref, xb_ref):
    @pl.when(pl.program_id(1) == 0)
    def _():
        xb_ref[...] = x_ref[...].astype(BF16)

    o_ref[...] = _dot(xb_ref[...], w_ref[...])


def _proj(x, w, tm, tn):
    M, K = x.shape
    N = w.shape[1]
    return pl.pallas_call(
        _proj_kernel,
        grid=(M // tm, N // tn),
        in_specs=[
            pl.BlockSpec((tm, K), lambda i, j: (i, 0)),
            pl.BlockSpec((K, tn), lambda i, j: (0, j)),
        ],
        out_specs=pl.BlockSpec((tm, tn), lambda i, j: (i, j)),
        out_shape=jax.ShapeDtypeStruct((M, N), F32),
        scratch_shapes=[pltpu.VMEM((tm, K), BF16)],
        compiler_params=_cparams("parallel", "arbitrary"),
        name="in_proj",
    )(x, w)


def _rope_tables(pos, rot_dim, period, live_lanes=LANES):
    half = rot_dim // 2
    inv = ROPE_THETA ** (-jnp.arange(half, dtype=F32) / half)
    ang = pos.astype(F32)[:, None] * inv[None, :]
    cos, sin = jnp.cos(ang), jnp.sin(ang)
    lane = np.arange(LANES)
    lp = lane % period
    idx = lp % half
    live = (lp < rot_dim) & (lane < live_lanes)
    c = jnp.where(live[None, :], cos[:, idx], 1.0)
    s = jnp.where(live[None, :], jnp.where((lp < half)[None, :], -sin[:, idx], sin[:, idx]), 0.0)
    return c.astype(F32), s.astype(F32)


def _rope_tile(x, c, s, half, period):
    lane = lax.broadcasted_iota(jnp.int32, x.shape, 1)
    first = (lane & (period - 1)) < half
    partner = jnp.where(first, pltpu.roll(x, LANES - half, 1), pltpu.roll(x, half, 1))
    return x * c + partner * s


def _rope_kernel(q_ref, k_ref, v_ref, qi_ref, sm_ref, ca_ref, sa_ref, ci_ref, si_ref, cs_ref, ss_ref,
                 qo_ref, ko_ref, kb_ref, vb_ref, qio_ref, smo_ref, smb_ref):
    ca, sa = ca_ref[...], sa_ref[...]
    ci, si = ci_ref[...], si_ref[...]
    for h in range(N_HEADS):
        sl = slice(h * LANES, (h + 1) * LANES)
        qo_ref[:, sl] = (_rope_tile(q_ref[:, sl], ca, sa, ROPE_DIM // 2, HEAD_DIM) * (HEAD_DIM ** -0.5)).astype(BF16)
    for h in range(N_KV_HEADS):
        sl = slice(h * LANES, (h + 1) * LANES)
        kr = _rope_tile(k_ref[:, sl], ca, sa, ROPE_DIM // 2, HEAD_DIM)
        ko_ref[:, sl] = kr
        kb_ref[:, sl] = kr.astype(BF16)
    vb_ref[...] = v_ref[...].astype(BF16)
    for h in range(IDX_HEADS * IDX_DIM // LANES):
        sl = slice(h * LANES, (h + 1) * LANES)
        qio_ref[:, sl] = _rope_tile(qi_ref[:, sl], ci, si, IDX_ROPE_DIM // 2, IDX_DIM).astype(BF16)
    sm = _rope_tile(sm_ref[...], cs_ref[...], ss_ref[...], IDX_ROPE_DIM // 2, IDX_DIM)
    smo_ref[...] = sm
    smb_ref[...] = sm.astype(BF16)


def _rope_prep(u_attn, u_small, tabs, tm):
    M = u_attn.shape[0]
    tpos = tabs[0].shape[0]
    nt = tpos // tm

    def col(name):
        w = SIZES[name]
        return pl.BlockSpec((tm, w), lambda i, o=ATTN_OFF[name] // w: (i, o))

    tab_spec = pl.BlockSpec((tm, LANES), lambda i: (i % nt, 0))
    row = lambda w: pl.BlockSpec((tm, w), lambda i: (i, 0))
    return pl.pallas_call(
        _rope_kernel,
        grid=(M // tm,),
        in_specs=[col('q_a'), col('k_a'), col('v_a'), col('q_idx'), row(LANES)] + [tab_spec] * 6,
        out_specs=[row(SIZES['q_a']), row(SIZES['k_a']), row(SIZES['k_a']), row(SIZES['v_a']),
                   row(SIZES['q_idx']), row(LANES), row(LANES)],
        out_shape=[
            jax.ShapeDtypeStruct((M, SIZES['q_a']), BF16),
            jax.ShapeDtypeStruct((M, SIZES['k_a']), F32),
            jax.ShapeDtypeStruct((M, SIZES['k_a']), BF16),
            jax.ShapeDtypeStruct((M, SIZES['v_a']), BF16),
            jax.ShapeDtypeStruct((M, SIZES['q_idx']), BF16),
            jax.ShapeDtypeStruct((M, LANES), F32),
            jax.ShapeDtypeStruct((M, LANES), BF16),
        ],
        compiler_params=_cparams("parallel"),
        name="rope_prep",
    )(u_attn, u_attn, u_attn, u_attn, u_small, *tabs)


BISECT_UNROLL = 4


def _count(pred):
    return jnp.sum(jnp.where(pred, 1.0, 0.0), axis=-1, keepdims=True)


def _tie_index(score, kidx, thr):
    return jnp.where(score == thr, kidx, jnp.int32(2 ** 31 - 1))


def _topk_threshold(score, kidx, n_allowed, k, idx_bits):
    rows = score.shape[0]
    take_all = n_allowed <= k
    lo0 = jnp.min(jnp.where(score == -jnp.inf, jnp.inf, score), axis=-1, keepdims=True)
    hi0 = jnp.max(score, axis=-1, keepdims=True)
    lo0 = jnp.where(take_all, 0.0, lo0)
    hi0 = jnp.where(take_all, 0.0, hi0)

    def step(lo, hi, n_lo):
        mid = 0.5 * lo + 0.5 * hi
        n_mid = _count(score >= mid)
        ge = n_mid >= k
        return jnp.where(ge, mid, lo), jnp.where(ge, hi, mid), jnp.where(ge, n_mid, n_lo)

    def body(state):
        lo, hi, n_lo, _ = state
        for _ in range(BISECT_UNROLL):
            lo, hi, n_lo = step(lo, hi, n_lo)
        mid = 0.5 * lo + 0.5 * hi
        still_open = jnp.max(jnp.where((mid > lo) & (mid < hi) & (n_lo > k), 1.0, 0.0))
        return lo, hi, n_lo, still_open

    n_lo0 = jnp.where(take_all, float(k), n_allowed.astype(F32))
    lo, hi, _, _ = lax.while_loop(lambda state: state[3] > 0.5, body, (lo0, hi0, n_lo0, jnp.float32(1.0)))
    thr = jnp.where(_count(score >= hi) >= k, hi, lo)
    tie = _tie_index(score, kidx, thr)

    def tie_search():
        need = k - _count(score > thr)

        def ibody(t, j):
            cand = j + jnp.left_shift(jnp.int32(1), idx_bits - 1 - t)
            return jnp.where(_count(tie < cand) < need, cand, j)

        return lax.fori_loop(0, idx_bits, ibody, jnp.zeros((rows, 1), jnp.int32))

    repeated = jnp.max(jnp.where(take_all, 0.0, _count(score == thr))) > 1.5
    jmax = lax.cond(repeated, tie_search, lambda: jnp.full((rows, 1), 2 ** 31 - 2, jnp.int32))
    return thr, jmax, take_all


def _selected(score, kidx, thr, jmax, take_all):
    return take_all | (score > thr) | (_tie_index(score, kidx, thr) <= jmax)


def _dsa_prompt_kernel(q_ref, qi_ref, sm_ref, k_ref, v_ref, kis_ref, o_ref, *, ktop, key_step):
    tq = q_ref.shape[0]
    S = k_ref.shape[0]
    i = pl.program_id(1)
    w0 = SMALL_OFF['w_idx']

    def attend(L):
        ki = kis_ref[:L, :IDX_DIM]
        w = sm_ref[:, w0:w0 + IDX_HEADS] * (IDX_HEADS ** -0.5 * IDX_DIM ** -0.5)
        head_lane = lax.broadcasted_iota(jnp.int32, (tq, IDX_HEADS), 1)
        per_trip = 2
        width = per_trip * IDX_DIM

        def idx_heads(t, score):
            qi = qi_ref[:, pl.ds(pl.multiple_of(t * width, width), width)]
            for j in range(per_trip):
                s = _dot_nt(qi[:, j * IDX_DIM:(j + 1) * IDX_DIM], ki)
                w_h = jnp.sum(jnp.where(head_lane == t * per_trip + j, w, 0.0), axis=-1, keepdims=True)
                score = score + w_h * jnp.maximum(s, 0.0)
            return score

        score = lax.fori_loop(0, IDX_HEADS // per_trip, idx_heads, jnp.zeros((tq, L), F32))
        qpos = i * tq + lax.broadcasted_iota(jnp.int32, (tq, 1), 0)
        kidx = lax.broadcasted_iota(jnp.int32, (tq, L), 1)
        allowed = kidx <= qpos
        score = jnp.where(allowed, score, -jnp.inf)
        thr, jmax, take_all = _topk_threshold(score, kidx, qpos + 1, ktop, int(L - 1).bit_length())
        bias = jnp.where(_selected(score, kidx, thr, jmax, take_all) & allowed, 0.0, -jnp.inf)
        def head(h):
            sl = pl.ds(pl.multiple_of(h * HEAD_DIM, HEAD_DIM), HEAD_DIM)
            kv = pl.ds(pl.multiple_of((h // GROUP) * HEAD_DIM, HEAD_DIM), HEAD_DIM)
            s = _dot_nt(q_ref[:, sl], k_ref[:L, kv]) + bias
            m = jnp.max(s, axis=-1, keepdims=True)
            p = jnp.exp(s - m)
            l = jnp.sum(p, axis=-1, keepdims=True)
            o_ref[:, sl] = _dot(p.astype(BF16), v_ref[:L, kv]) / l

        def pair(t, carry):
            head(2 * t)
            head(2 * t + 1)
            return carry

        lax.fori_loop(0, N_HEADS // 2, pair, 0)

    level = ((i + 1) * tq - 1) // key_step
    for lv in range(S // key_step):
        pl.when(level == lv)(functools.partial(attend, (lv + 1) * key_step))


def _dsa_prompt(q_bf, qi_bf, small_rot, k_bf, v_bf, small_bf, B, S, tq):
    ktop = min(TOPK_MAX, S // 4)
    nq = S // tq
    key_step = min(S, 256)
    row = lambda w: pl.BlockSpec((tq, w), lambda b, i: (b * nq + i, 0))
    full = lambda w: pl.BlockSpec((S, w), lambda b, i: (b, 0))
    return pl.pallas_call(
        functools.partial(_dsa_prompt_kernel, ktop=ktop, key_step=key_step),
        grid=(B, nq),
        in_specs=[row(q_bf.shape[1]), row(qi_bf.shape[1]), row(LANES),
                  full(k_bf.shape[1]), full(v_bf.shape[1]), full(LANES)],
        out_specs=row(q_bf.shape[1]),
        out_shape=jax.ShapeDtypeStruct((B * S, q_bf.shape[1]), F32),
        compiler_params=_cparams("parallel", "arbitrary"),
        name="dsa_prompt",
    )(q_bf, qi_bf, small_rot, k_bf, v_bf, small_bf)


def _idx_score_rows(qi, w, kpage_t):
    s = _dot(qi, kpage_t)
    return jnp.sum(w * jnp.maximum(s, 0.0), axis=0, keepdims=True)


def _sample_scores_kernel(pt_ref, qi_ref, w_ref, knew_ref, *refs):
    page_refs, (o_ref, onew_ref) = refs[:-2], refs[-2:]
    npp = len(page_refs)
    p = pl.program_id(1)
    qi = qi_ref[0]
    w = w_ref[0] * (IDX_HEADS ** -0.5 * IDX_DIM ** -0.5)
    for j, page_ref in enumerate(page_refs):
        o_ref[0, pl.ds(p * npp + j, 1), :] = _idx_score_rows(qi, w, page_ref[0, 0].astype(BF16))

    @pl.when(p == 0)
    def _():
        kn = jnp.broadcast_to(knew_ref[0], (IDX_DIM, PAGE_SIZE))
        sc = _idx_score_rows(qi, w, kn)
        lane = lax.broadcasted_iota(jnp.int32, (1, PAGE_SIZE), 1)
        onew_ref[0] = jnp.where(lane == 0, sc, -jnp.inf)


def _page_specs(page_shape, npp, layer):
    zeros = (0,) * len(page_shape)
    return [pl.BlockSpec((1, 1) + page_shape, lambda b, p, pt, j=j: (layer, pt[b, p * npp + j]) + zeros)
            for j in range(npp)]


def _pages_per_step(n_pages, cap):
    npp = min(cap, n_pages)
    while n_pages % npp:
        npp -= 1
    return npp


def _sample_scores(page_table, qi3, w3, knew3, cache_ki, layer):
    DB, n_pages = page_table.shape
    npp = _pages_per_step(n_pages, 16)
    return pl.pallas_call(
        _sample_scores_kernel,
        grid_spec=pltpu.PrefetchScalarGridSpec(
            num_scalar_prefetch=1,
            grid=(DB, n_pages // npp),
            in_specs=[
                pl.BlockSpec((1, IDX_HEADS, IDX_DIM), lambda b, p, pt: (b, 0, 0)),
                pl.BlockSpec((1, IDX_HEADS, 1), lambda b, p, pt: (b, 0, 0)),
                pl.BlockSpec((1, IDX_DIM, 1), lambda b, p, pt: (b, 0, 0)),
            ] + _page_specs((IDX_DIM, PAGE_SIZE), npp, layer),
            out_specs=[
                pl.BlockSpec((1, n_pages, PAGE_SIZE), lambda b, p, pt: (b, 0, 0)),
                pl.BlockSpec((1, 1, PAGE_SIZE), lambda b, p, pt: (b, 0, 0)),
            ],
        ),
        out_shape=[jax.ShapeDtypeStruct((DB, n_pages, PAGE_SIZE), F32),
                   jax.ShapeDtypeStruct((DB, 1, PAGE_SIZE), F32)],
        compiler_params=_cparams("parallel", "arbitrary"),
        name="sample_scores",
    )(page_table, qi3, w3, knew3, *([cache_ki] * npp))


def _sample_thr_kernel(s_ref, thr_ref, jmax_ref, all_ref, *, ktop, n_valid, idx_bits):
    kidx = lax.broadcasted_iota(jnp.int32, s_ref.shape, 1)
    score = jnp.where(kidx < n_valid, s_ref[...], -jnp.inf)
    n_allowed = jnp.full((s_ref.shape[0], 1), n_valid, jnp.int32)
    thr, jmax, take_all = _topk_threshold(score, kidx, n_allowed, ktop, idx_bits)
    thr_ref[...] = jnp.broadcast_to(thr, thr_ref.shape)
    jmax_ref[...] = jnp.broadcast_to(jmax, jmax_ref.shape)
    all_ref[...] = jnp.broadcast_to(take_all.astype(jnp.int32), all_ref.shape)


def _sample_threshold(scores, ktop, n_valid):
    DB, L = scores.shape
    out = lambda dt: jax.ShapeDtypeStruct((DB, LANES), dt)
    return pl.pallas_call(
        functools.partial(_sample_thr_kernel, ktop=ktop, n_valid=n_valid, idx_bits=int(L - 1).bit_length()),
        out_shape=[out(F32), out(jnp.int32), out(jnp.int32)],
        compiler_params=pltpu.CompilerParams(vmem_limit_bytes=VMEM_LIMIT),
        name="sample_threshold",
    )(scores)


def _sample_attn_kernel(pt_ref, q_ref, s_ref, thr_ref, jmax_ref, all_ref, knew_ref, vnew_ref, *refs, n_pages):
    npp = (len(refs) - 4) // 2
    kp_refs, vp_refs = refs[:npp], refs[npp:2 * npp]
    o_ref, m_ref, l_ref, acc_ref = refs[2 * npp:]
    b = pl.program_id(0)
    p = pl.program_id(1)

    @pl.when(p == 0)
    def _():
        m_ref[...] = jnp.full_like(m_ref, NEG_BIG)
        l_ref[...] = jnp.zeros_like(l_ref)
        acc_ref[...] = jnp.zeros_like(acc_ref)

    thr = thr_ref[pl.ds(b, 1), 0:1]
    jmax = jmax_ref[pl.ds(b, 1), 0:1]
    take_all = all_ref[pl.ds(b, 1), 0:1] > 0

    q = q_ref[0]
    group = lax.broadcasted_iota(jnp.int32, (N_HEADS, 1), 0) // GROUP
    X = PAGE_SIZE * N_KV_HEADS
    own_head = (lax.broadcasted_iota(jnp.int32, (N_HEADS, X), 1) % N_KV_HEADS) == group
    repeat = (lax.broadcasted_iota(jnp.int32, (PAGE_SIZE, X), 1) // N_KV_HEADS
              == lax.broadcasted_iota(jnp.int32, (PAGE_SIZE, X), 0)).astype(BF16)

    def fold(logits, pv_fns):
        m_old = m_ref[...]
        m_new = functools.reduce(jnp.maximum, [jnp.max(s, axis=-1, keepdims=True) for s in logits], m_old)
        corr = jnp.exp(m_old - m_new)
        probs = [jnp.exp(s - m_new) for s in logits]
        l_ref[...] = l_ref[...] * corr + functools.reduce(
            lambda a, c: a + c, [jnp.sum(pr, axis=-1, keepdims=True) for pr in probs])
        acc_ref[...] = acc_ref[...] * corr + functools.reduce(
            lambda a, c: a + c, [fn(pr) for fn, pr in zip(pv_fns, probs)])
        m_ref[...] = m_new

    logits, pv_fns = [], []
    for j in range(npp):
        page = p * npp + j
        scores_row = s_ref[0, pl.ds(page, 1), :]
        kidx = page * PAGE_SIZE + lax.broadcasted_iota(jnp.int32, (1, PAGE_SIZE), 1)
        sel = _selected(scores_row, kidx, thr, jmax, take_all)
        sel_rows = _dot(jnp.broadcast_to(jnp.where(sel, 1.0, 0.0), (N_HEADS, PAGE_SIZE)).astype(BF16), repeat)
        s = _dot_nt(q, kp_refs[j][0].astype(BF16))
        logits.append(jnp.where(own_head & (sel_rows > 0.5), s, -jnp.inf))
        pv_fns.append(lambda pr, ref=vp_refs[j]: _dot(pr.astype(BF16), ref[0].astype(BF16)))
    fold(logits, pv_fns)

    @pl.when(p == pl.num_programs(1) - 1)
    def _():
        def per_head(ref):
            rows = [jnp.where(group == n, jnp.broadcast_to(ref[0, :, n * HEAD_DIM:(n + 1) * HEAD_DIM],
                                                           (N_HEADS, HEAD_DIM)), 0.0) for n in range(N_KV_HEADS)]
            return functools.reduce(lambda a, c: a + c, rows).astype(BF16).astype(F32)

        score_new = s_ref[0, pl.ds(n_pages, 1), 0:1]
        sel = _selected(score_new, jnp.full((1, 1), n_pages * PAGE_SIZE, jnp.int32), thr, jmax, take_all)
        s = jnp.sum(q.astype(F32) * per_head(knew_ref), axis=-1, keepdims=True)
        s = jnp.where(sel, s, -jnp.inf)
        vexp = per_head(vnew_ref)
        fold([s], [lambda pr: pr.astype(BF16).astype(F32) * vexp])
        o_ref[0] = acc_ref[...] / l_ref[...]


def _sample_attn(page_table, q3, scores3, thr, jmax, take_all, knew3, vnew3, cache_k, cache_v, layer):
    DB, n_pages = page_table.shape
    width = N_KV_HEADS * HEAD_DIM
    npp = _pages_per_step(n_pages, 32)
    bsel = lambda *shape: pl.BlockSpec((1,) + shape, lambda b, p, pt: (b,) + (0,) * len(shape))
    whole = pl.BlockSpec((DB, LANES), lambda b, p, pt: (0, 0))
    n_phys = cache_k.shape[1]
    rows = PAGE_SIZE * N_KV_HEADS
    cache_k = cache_k.reshape(-1, rows, HEAD_DIM)
    cache_v = cache_v.reshape(-1, rows, HEAD_DIM)
    pages = [pl.BlockSpec((1, rows, HEAD_DIM),
                          lambda b, p, pt, j=j: (layer * n_phys + pt[b, p * npp + j], 0, 0)) for j in range(npp)]
    return pl.pallas_call(
        functools.partial(_sample_attn_kernel, n_pages=n_pages),
        grid_spec=pltpu.PrefetchScalarGridSpec(
            num_scalar_prefetch=1,
            grid=(DB, n_pages // npp),
            in_specs=[bsel(N_HEADS, HEAD_DIM), bsel(n_pages + 1, PAGE_SIZE), whole, whole, whole,
                      bsel(1, width), bsel(1, width)] + pages + pages,
            out_specs=bsel(N_HEADS, HEAD_DIM),
            scratch_shapes=[pltpu.VMEM((N_HEADS, 1), F32), pltpu.VMEM((N_HEADS, 1), F32),
                            pltpu.VMEM((N_HEADS, HEAD_DIM), F32)],
        ),
        out_shape=jax.ShapeDtypeStruct((DB, N_HEADS, HEAD_DIM), F32),
        compiler_params=_cparams("parallel", "arbitrary"),
        name="sample_attn",
    )(page_table, q3, scores3, thr, jmax, take_all, knew3, vnew3, *([cache_k] * npp), *([cache_v] * npp))


CARRY = 8
MXU_DIM = 256
GDN_GROUP = MXU_DIM // CHUNK


def _spread(a, row_head, hg):
    return jnp.concatenate([jnp.where(row_head == i, a, 0.0) for i in range(hg)], axis=1)


def _gdn_prep_kernel(x_ref, w_ref, buf_ref, q_ref, k_ref, v_ref, conv_ref, xpad_ref):
    tt = x_ref.shape[1]
    t = pl.program_id(1)
    lo = CARRY - (CONV_W - 1)

    @pl.when(t == 0)
    def _():
        xpad_ref[lo:CARRY, :] = buf_ref[0]

    xpad_ref[CARRY:CARRY + tt, :] = x_ref[0]
    nh = GDN_HEADS
    for c in range(CONV_DIM // LANES):
        sl = slice(c * LANES, (c + 1) * LANES)
        y = w_ref[0:1, sl] * xpad_ref[lo:lo + tt, sl]
        for j in range(1, CONV_W):
            y = y + w_ref[j:j + 1, sl] * xpad_ref[lo + j:lo + j + tt, sl]
        y = _silu(y)
        if c < 2 * nh:
            y = y * lax.rsqrt(jnp.sum(y * y, axis=-1, keepdims=True) + NORM_EPS)
        if c < nh:
            q_ref[0, :, sl] = y * (GDN_DK ** -0.5)
        elif c < 2 * nh:
            k_ref[0, :, slice((c - nh) * LANES, (c - nh + 1) * LANES)] = y
        else:
            v_ref[0, :, slice((c - 2 * nh) * LANES, (c - 2 * nh + 1) * LANES)] = y
    last = xpad_ref[lo + tt:CARRY + tt, :]
    xpad_ref[lo:CARRY, :] = last

    @pl.when(t == pl.num_programs(1) - 1)
    def _():
        conv_ref[0] = last


def _gdn_prep(u_gdn3, conv_w, buf, tt):
    B, T, _ = u_gdn3.shape
    w = SIZES['qkv_b']
    hd = GDN_HEADS * GDN_DK
    out = jax.ShapeDtypeStruct((B, T, hd), F32)
    ospec = pl.BlockSpec((1, tt, hd), lambda b, t: (b, t, 0))
    return pl.pallas_call(
        _gdn_prep_kernel,
        grid=(B, T // tt),
        in_specs=[
            pl.BlockSpec((1, tt, w), lambda b, t: (b, t, 0)),
            pl.BlockSpec((CONV_W, w), lambda b, t: (0, 0)),
            pl.BlockSpec((1, CONV_W - 1, w), lambda b, t: (b, 0, 0)),
        ],
        out_specs=[ospec, ospec, ospec, pl.BlockSpec((1, CONV_W - 1, w), lambda b, t: (b, 0, 0))],
        out_shape=[out, out, out, jax.ShapeDtypeStruct((B, CONV_W - 1, w), F32)],
        scratch_shapes=[pltpu.VMEM((CARRY + tt, w), F32)],
        compiler_params=_cparams("parallel", "arbitrary"),
        name="gdn_prep",
    )(u_gdn3, conv_w, buf)


def _gdn_chunk_kernel(q_ref, k_ref, v_ref, z_ref, sm_ref, alog_ref, dtb_ref, gn_ref, s0_ref,
                      o_ref, sout_ref, state_ref, *, t_valid):
    C = q_ref.shape[1]
    c = pl.program_id(1)

    @pl.when(c == 0)
    def _():
        state_ref[...] = s0_ref[0]

    H = GDN_HEADS
    a0, b0 = SMALL_OFF['a_b'], SMALL_OFF['beta_b']
    live = (c * C + lax.broadcasted_iota(jnp.int32, (C, H), 0)) < t_valid
    xs = sm_ref[0, :, a0:a0 + H] + dtb_ref[...]
    softplus = jnp.maximum(xs, 0.0) + jnp.log1p(jnp.exp(-jnp.abs(xs)))
    g_all = jnp.where(live, -jnp.exp(alog_ref[...]) * softplus, 0.0)
    beta_all = jnp.where(live, jax.nn.sigmoid(sm_ref[0, :, b0:b0 + H]), 0.0)
    tri_f = (lax.broadcasted_iota(jnp.int32, (C, C), 0) >= lax.broadcasted_iota(jnp.int32, (C, C), 1)).astype(F32)
    eye_h = (lax.broadcasted_iota(jnp.int32, (H, H), 0) == lax.broadcasted_iota(jnp.int32, (H, H), 1)).astype(F32)
    gcum = _dot_hi(tri_f, g_all)
    gcum_t = _dot_nt_hi(eye_h, gcum)

    HG = GDN_GROUP
    R = HG * C
    ri = lax.broadcasted_iota(jnp.int32, (R, R), 0)
    ci = lax.broadcasted_iota(jnp.int32, (R, R), 1)
    same_head = (ri // C) == (ci // C)
    mask_incl = same_head & (ri >= ci)
    mask_strict = same_head & (ri > ci)
    eye = (ri == ci).astype(F32)
    row_head = lax.broadcasted_iota(jnp.int32, (R, 1), 0) // C
    row_head2 = jnp.concatenate([row_head, row_head], axis=0)

    groups = []
    for grp in range(H // HG):
        heads = range(grp * HG, (grp + 1) * HG)
        sls = [slice(h * GDN_DK, (h + 1) * GDN_DK) for h in heads]
        rows = lambda ref: jnp.concatenate([ref[0, :, sl] for sl in sls], axis=0)
        cols = lambda a: jnp.concatenate([a[:, h:h + 1] for h in heads], axis=0)
        q, k, v = rows(q_ref), rows(k_ref), rows(v_ref)
        beta = cols(beta_all)
        gcol = cols(gcum)
        grow = jnp.concatenate([gcum_t[h:h + 1, :] for h in heads], axis=1)
        glast = jnp.concatenate([jnp.broadcast_to(gcum[C - 1:C, h:h + 1], (C, 1)) for h in heads], axis=0)
        decay = jnp.where(mask_incl, jnp.exp(jnp.where(mask_incl, gcol - grow, 0.0)), 0.0)
        kb = k * beta
        kk_qk = _dot_nt(jnp.concatenate([kb, q], axis=0).astype(BF16), k.astype(BF16))
        nmat = jnp.where(mask_strict, kk_qk[:R] * decay, 0.0)
        groups.append(dict(
            heads=heads, sls=sls, nmat=nmat.astype(BF16), xinv=eye - nmat, qk=(kk_qk[R:] * decay).astype(BF16),
            rhs=jnp.concatenate([v * beta, kb * jnp.exp(gcol)], axis=1).astype(BF16),
            q_dec=q * jnp.exp(gcol), k_dec=k * jnp.exp(glast - gcol)))

    for _ in range(max(C - 1, 1).bit_length() - 1):
        for g in groups:
            g['resid'] = (eye - g['xinv']) - _dot(g['nmat'], g['xinv'].astype(BF16))
        for g in groups:
            g['xinv'] = g['xinv'] + _dot_bf16(g['xinv'], g['resid'])
    for g in groups:
        g['x'] = _dot(g['xinv'].astype(BF16), g['rhs'])
    for grp, g in enumerate(groups):
        u, w = g['x'][:, :GDN_DV], g['x'][:, GDN_DV:]
        s_stack = state_ref[grp * HG:(grp + 1) * HG].reshape(HG * GDN_DK, GDN_DV)
        w_q = jnp.concatenate([w, g['q_dec']], axis=0)
        ws_qs = _dot(_spread(w_q, row_head2, HG).astype(BF16), s_stack.astype(BF16))
        g['v_new'] = u - ws_qs[:R]
        g['o'] = ws_qs[R:]
    for g in groups:
        g['o'] = g['o'] + _dot(g['qk'], g['v_new'].astype(BF16))
        g['s_add'] = _dot(g['k_dec'].T.astype(BF16), _spread(g['v_new'], row_head, HG).astype(BF16))
    for g in groups:
        o = g['o']
        on = o * lax.rsqrt(jnp.mean(o * o, axis=-1, keepdims=True) + NORM_EPS) * gn_ref[...]
        for a, h in enumerate(g['heads']):
            state_ref[h] = (state_ref[h] * jnp.exp(gcum[C - 1:C, h:h + 1])
                            + g['s_add'][:, a * GDN_DV:(a + 1) * GDN_DV])
            o_ref[0, :, g['sls'][a]] = on[a * C:(a + 1) * C] * _silu(z_ref[0, :, g['sls'][a]])

    @pl.when(c == pl.num_programs(1) - 1)
    def _():
        sout_ref[0] = state_ref[...]


def _gdn_chunks(qn, kn, vv, u_gate3, small3, a_log, dt_bias, gn, s0, t_valid):
    B, Tp, hd = qn.shape
    nc = Tp // CHUNK
    blk = pl.BlockSpec((1, CHUNK, hd), lambda b, c: (b, c, 0))
    vec = lambda w: pl.BlockSpec((1, w), lambda b, c: (0, 0))
    st = pl.BlockSpec((1, GDN_HEADS, GDN_DK, GDN_DV), lambda b, c: (b, 0, 0, 0))
    return pl.pallas_call(
        functools.partial(_gdn_chunk_kernel, t_valid=t_valid),
        grid=(B, nc),
        in_specs=[blk, blk, blk,
                  pl.BlockSpec((1, CHUNK, hd), lambda b, c, o=GATE_OFF['z_b'] // hd: (b, c, o)),
                  pl.BlockSpec((1, CHUNK, LANES), lambda b, c: (b, c, 0)),
                  vec(GDN_HEADS), vec(GDN_HEADS), vec(GDN_DV), st],
        out_specs=[blk, st],
        out_shape=[jax.ShapeDtypeStruct((B, Tp, hd), F32),
                   jax.ShapeDtypeStruct((B, GDN_HEADS, GDN_DK, GDN_DV), F32)],
        scratch_shapes=[pltpu.VMEM((GDN_HEADS, GDN_DK, GDN_DV), F32)],
        compiler_params=_cparams("parallel", "arbitrary"),
        name="gdn_chunks",
    )(qn, kn, vv, u_gate3, small3, a_log, dt_bias, gn, s0)


def _merge_kernel(x_ref, oa_ref, ob_ref, ga_ref, gb_ref, wo_ref, g_ref, b_ref, o_ref):
    merged = jax.nn.sigmoid(ga_ref[...]) * oa_ref[...] + jax.nn.sigmoid(gb_ref[...]) * ob_ref[...]
    y = DN_ALPHA * x_ref[...] + _dot(merged.astype(BF16), wo_ref[...])
    o_ref[...] = _layer_norm(y, g_ref[...], b_ref[...])


def _merge_proj_ln(x, o_a, o_b, u_gate, w_o, g, b, tm):
    M, D = x.shape
    row = pl.BlockSpec((tm, D), lambda i: (i, 0))
    col = lambda name: pl.BlockSpec((tm, D), lambda i, o=GATE_OFF[name] // D: (i, o))
    vec = pl.BlockSpec((1, D), lambda i: (0, 0))
    return pl.pallas_call(
        _merge_kernel,
        grid=(M // tm,),
        in_specs=[row, row, row, col('gate_a'), col('gate_b'),
                  pl.BlockSpec((D, D), lambda i: (0, 0)), vec, vec],
        out_specs=row,
        out_shape=jax.ShapeDtypeStruct((M, D), F32),
        compiler_params=_cparams("parallel"),
        name="merge_proj_ln",
    )(x, o_a, o_b, u_gate, u_gate, w_o, g, b)


def _tiles(M):
    if M % 1024 == 0:
        return 512, 1024, 256
    return M, M, M


def _layer(x, B, T, pos, wts, conv_buf, ssm0, dsa_fn):
    M = B * T
    tm, tp, te = _tiles(M)
    row = lambda a: a.reshape(1, -1)
    x1 = _ffn_ln(x, wts['ffn1_g'], wts['ffn1_u'], wts['ffn1_d'], row(wts['ln1_g']), row(wts['ln1_b']),
                 tm, 512)
    u_attn = _proj(x1, wts['w_attn'], tp, 1024)
    u_gdn = _proj(x1, wts['w_gdn'], tp, 1024)
    u_gate = _proj(x1, wts['w_gate'], tp, 1024)
    u_small = _proj(x1, wts['w_small'], tp, LANES)

    pos_rows = pos if T > 1 else jnp.broadcast_to(pos, (M,))
    tabs = (_rope_tables(pos_rows, ROPE_DIM, HEAD_DIM) + _rope_tables(pos_rows, IDX_ROPE_DIM, IDX_DIM)
            + _rope_tables(pos_rows, IDX_ROPE_DIM, IDX_DIM, live_lanes=IDX_DIM))
    q_bf, k_rot, k_bf, v_bf, qi_bf, small_rot, small_bf = _rope_prep(u_attn, u_small, tabs, te)
    v_rows = u_attn[:, ATTN_OFF['v_a']:ATTN_OFF['v_a'] + SIZES['v_a']]
    o_a = dsa_fn(v_rows, q_bf, k_rot, k_bf, v_bf, qi_bf, small_rot, small_bf)

    qn, kn, vv, conv_new = _gdn_prep(u_gdn.reshape(B, T, -1), wts['conv_w'], conv_buf, min(T, 256))
    pad = (-T) % CHUNK
    pad3 = lambda a: jnp.pad(a, ((0, 0), (0, pad), (0, 0))) if pad else a
    o_b, ssm_new = _gdn_chunks(pad3(qn), pad3(kn), pad3(vv), pad3(u_gate.reshape(B, T, -1)),
                               pad3(u_small.reshape(B, T, LANES)),
                               row(wts['a_log']), row(wts['dt_bias']), row(wts['gdn_norm_g']), ssm0, T)
    o_b = o_b[:, :T].reshape(M, D_MODEL)

    x2 = _merge_proj_ln(x1, o_a, o_b, u_gate, wts['w_o'], row(wts['ln2_g']), row(wts['ln2_b']), te)
    y = _ffn_ln(x2, wts['ffn2_g'], wts['ffn2_u'], wts['ffn2_d'], row(wts['ln3_g']), row(wts['ln3_b']),
                tm, 512)
    ki_rows = small_rot[:, SMALL_OFF['k_idx']:SMALL_OFF['k_idx'] + IDX_DIM]
    return y, (k_rot, v_rows, ki_rows, ssm_new, conv_new)


def _split_w_in(w_in):
    offs = dict(zip([nm for nm, _ in IN_SPLITS], np.cumsum([0] + [n for _, n in IN_SPLITS])))

    def span(order):
        lo = offs[order[0]]
        hi = offs[order[-1]] + SIZES[order[-1]]
        assert hi - lo == sum(SIZES[nm] for nm in order)
        return w_in[:, lo:hi].astype(BF16)

    w_small = jnp.concatenate([w_in[:, offs[nm]:offs[nm] + SIZES[nm]] for nm in SMALL_ORDER], axis=1)
    w_small = jnp.pad(w_small, ((0, 0), (0, LANES - SMALL_USED))).astype(BF16)
    return span(ATTN_ORDER), span(GDN_ORDER), span(GATE_ORDER), w_small


def kernel(x_prompt, x_sample, cache_k, cache_v, cache_idx_k, state_ssm, state_conv, page_table, ffn1_w_gate, ffn1_w_up, ffn1_w_down, ln1_g, ln1_b, w_in, conv_w, a_log, dt_bias, gdn_norm_g, w_o, ln2_g, ln2_b, ffn2_w_gate, ffn2_w_up, ffn2_w_down, ln3_g, ln3_b):
    B, S, _ = x_prompt.shape
    DB, T, _ = x_sample.shape
    assert T == 1, "the sample path handles one new token per sequence"
    n_pages = page_table.shape[1]
    yp = x_prompt.reshape(B * S, D_MODEL)
    ys = x_sample.reshape(DB * T, D_MODEL)
    outs_p, outs_s = [], []
    for l in range(ffn1_w_gate.shape[0]):
        w_attn, w_gdn, w_gate, w_small = _split_w_in(w_in[l])
        wts = dict(
            ffn1_g=ffn1_w_gate[l].astype(BF16), ffn1_u=ffn1_w_up[l].astype(BF16), ffn1_d=ffn1_w_down[l].astype(BF16),
            ffn2_g=ffn2_w_gate[l].astype(BF16), ffn2_u=ffn2_w_up[l].astype(BF16), ffn2_d=ffn2_w_down[l].astype(BF16),
            ln1_g=ln1_g[l], ln1_b=ln1_b[l], ln2_g=ln2_g[l], ln2_b=ln2_b[l], ln3_g=ln3_g[l], ln3_b=ln3_b[l],
            w_attn=w_attn, w_gdn=w_gdn, w_gate=w_gate, w_small=w_small, w_o=w_o[l].astype(BF16), conv_w=conv_w[l],
            a_log=a_log[l], dt_bias=dt_bias[l], gdn_norm_g=gdn_norm_g[l],
        )

        def dsa_p(v_rows, q_bf, k_rot, k_bf, v_bf, qi_bf, small_rot, small_bf):
            return _dsa_prompt(q_bf, qi_bf, small_rot, k_bf, v_bf, small_bf, B, S, min(S, 256))

        def dsa_s(v_rows, q_bf, k_rot, k_bf, v_bf, qi_bf, small_rot, small_bf, l=l):
            w0 = SMALL_OFF['w_idx']
            w3 = small_rot[:, w0:w0 + IDX_HEADS].reshape(DB, IDX_HEADS, 1)
            qi3 = qi_bf.reshape(DB, IDX_HEADS, IDX_DIM)
            knew_i = small_bf[:, :IDX_DIM].reshape(DB, IDX_DIM, 1)
            width = N_KV_HEADS * HEAD_DIM
            past, new = _sample_scores(page_table, qi3, w3, knew_i, jnp.swapaxes(cache_idx_k, 2, 3), l)
            scores3 = jnp.concatenate([past, new], axis=1)
            n_keys = n_pages * PAGE_SIZE + T
            ktop = min(TOPK_MAX, n_keys // 4)
            thr, jmax, take_all = _sample_threshold(scores3.reshape(DB, -1), ktop, n_keys)
            o = _sample_attn(page_table, q_bf.reshape(DB, N_HEADS, HEAD_DIM), scores3, thr, jmax, take_all,
                             k_rot.reshape(DB, 1, width), v_rows.reshape(DB, 1, width), cache_k, cache_v, l)
            return o.reshape(DB, D_MODEL)

        conv0 = jnp.zeros((B, CONV_W - 1, CONV_DIM), F32)
        ssm_zero = jnp.zeros((B, GDN_HEADS, GDN_DK, GDN_DV), F32)
        yp, st_p = _layer(yp, B, S, jnp.arange(S, dtype=jnp.int32), wts, conv0, ssm_zero, dsa_p)
        past_len = n_pages * PAGE_SIZE
        ys, st_s = _layer(ys, DB, T, past_len + jnp.arange(T, dtype=jnp.int32), wts, state_conv[l], state_ssm[l], dsa_s)
        outs_p.append(st_p)
        outs_s.append(st_s)

    def stack(outs, nb, nt):
        d = len(outs)
        k, v, ki, ssm, conv = [a[0][None] if d == 1 else jnp.stack(a) for a in zip(*outs)]
        return (k.reshape(d, nb, nt, N_KV_HEADS, HEAD_DIM), v.reshape(d, nb, nt, N_KV_HEADS, HEAD_DIM),
                ki.reshape(d, nb, nt, IDX_DIM), ssm, conv)

    return (yp.reshape(B, S, D_MODEL), ys.reshape(DB, T, D_MODEL)) + stack(outs_p, B, S) + stack(outs_s, DB, T)
```

```python
import functools

import jax
import jax.numpy as jnp
import numpy as np
from jax import lax
from jax.experimental import pallas as pl
from jax.experimental.pallas import tpu as pltpu

D_MODEL = 2048
PAGE_SIZE = 128
HEAD_DIM = 128
N_HEADS = D_MODEL // HEAD_DIM
N_KV_HEADS = 4
GROUP = N_HEADS // N_KV_HEADS
ROPE_DIM = HEAD_DIM // 4
IDX_HEADS = 16
IDX_DIM = 64
IDX_ROPE_DIM = IDX_DIM // 4
TOPK_MAX = 256
ROPE_THETA = 500000.0
GDN_DK = 128
GDN_DV = 128
GDN_HEADS = D_MODEL // GDN_DV
CONV_W = 4
CONV_DIM = 2 * GDN_HEADS * GDN_DK + GDN_HEADS * GDN_DV
CHUNK = 64
LN_EPS = 1e-5
NORM_EPS = 1e-6
DEPTH = 1
DN_ALPHA = (2 * DEPTH) ** 0.25

IN_SPLITS = (
    ('q_a', N_HEADS * HEAD_DIM), ('k_a', N_KV_HEADS * HEAD_DIM), ('v_a', N_KV_HEADS * HEAD_DIM),
    ('q_idx', IDX_HEADS * IDX_DIM), ('k_idx', IDX_DIM), ('w_idx', IDX_HEADS),
    ('qkv_b', CONV_DIM), ('a_b', GDN_HEADS), ('beta_b', GDN_HEADS), ('z_b', GDN_HEADS * GDN_DV),
    ('gate_a', D_MODEL), ('gate_b', D_MODEL),
)
ATTN_ORDER = ('q_a', 'k_a', 'v_a', 'q_idx')
GDN_ORDER = ('qkv_b',)
GATE_ORDER = ('z_b', 'gate_a', 'gate_b')
SMALL_ORDER = ('k_idx', 'w_idx', 'a_b', 'beta_b')
LANES = 128
VMEM_LIMIT = 56 * 1024 * 1024
NEG_BIG = -1e30

F32 = jnp.float32
BF16 = jnp.bfloat16


def _offsets(order):
    sizes = dict(IN_SPLITS)
    offs, o = {}, 0
    for nm in order:
        offs[nm] = o
        o += sizes[nm]
    return offs, o


ATTN_OFF, ATTN_COLS = _offsets(ATTN_ORDER)
GATE_OFF, _ = _offsets(GATE_ORDER)
SMALL_OFF, SMALL_USED = _offsets(SMALL_ORDER)
SIZES = dict(IN_SPLITS)


def _cparams(*sem):
    return pltpu.CompilerParams(dimension_semantics=sem, vmem_limit_bytes=VMEM_LIMIT)


def _dot(a, b):
    return jnp.dot(a, b, preferred_element_type=F32)


def _dot_nt(a, b):
    return lax.dot_general(a, b, (((1,), (1,)), ((), ())), preferred_element_type=F32)


def _dot_hi(a, b):
    return jnp.dot(a, b, preferred_element_type=F32, precision=lax.Precision.HIGHEST)


def _dot_nt_hi(a, b):
    return lax.dot_general(a, b, (((1,), (1,)), ((), ())), preferred_element_type=F32,
                           precision=lax.Precision.HIGHEST)


def _dot_bf16(a, b):
    return _dot(a.astype(BF16), b.astype(BF16))


def _silu(x):
    return x * jax.nn.sigmoid(x)


def _layer_norm(y, g, b):
    mu = jnp.mean(y, axis=-1, keepdims=True)
    d = y - mu
    var = jnp.mean(d * d, axis=-1, keepdims=True)
    return d * lax.rsqrt(var + LN_EPS) * g + b


def _ffn_ln_kernel(x_ref, wg_ref, wu_ref, wd_ref, g_ref, b_ref, o_ref, acc_ref, xb_ref):
    j = pl.program_id(1)

    @pl.when(j == 0)
    def _():
        acc_ref[...] = jnp.zeros_like(acc_ref)
        xb_ref[...] = x_ref[...].astype(BF16)

    xb = xb_ref[...]
    hg = _dot(xb, wg_ref[...])
    hu = _dot(xb, wu_ref[...])
    h = _silu(hg) * hu
    acc_ref[...] += _dot(h.astype(BF16), wd_ref[...])

    @pl.when(j == pl.num_programs(1) - 1)
    def _():
        y = DN_ALPHA * x_ref[...] + 0.5 * acc_ref[...]
        o_ref[...] = _layer_norm(y, g_ref[...], b_ref[...])


def _ffn_ln(x, wg, wu, wd, g, b, tm, tf):
    M, D = x.shape
    F = wg.shape[1]
    return pl.pallas_call(
        _ffn_ln_kernel,
        grid=(M // tm, F // tf),
        in_specs=[
            pl.BlockSpec((tm, D), lambda i, j: (i, 0)),
            pl.BlockSpec((D, tf), lambda i, j: (0, j)),
            pl.BlockSpec((D, tf), lambda i, j: (0, j)),
            pl.BlockSpec((tf, D), lambda i, j: (j, 0)),
            pl.BlockSpec((1, D), lambda i, j: (0, 0)),
            pl.BlockSpec((1, D), lambda i, j: (0, 0)),
        ],
        out_specs=pl.BlockSpec((tm, D), lambda i, j: (i, 0)),
        out_shape=jax.ShapeDtypeStruct((M, D), F32),
        scratch_shapes=[pltpu.VMEM((tm, D), F32), pltpu.VMEM((tm, D), BF16)],
        compiler_params=_cparams("parallel", "arbitrary"),
        name="ffn_ln",
    )(x, wg, wu, wd, g, b)


def _proj_kernel(x_ref, w_ref, o_ref, xb_ref):
    @pl.when(pl.program_id(1) == 0)
    def _():
        xb_ref[...] = x_ref[...].astype(BF16)

    o_ref[...] = _dot_nt(xb_ref[...], w_ref[...])


def _proj(x, w_t, tm, tn):
    M, K = x.shape
    N = w_t.shape[0]
    return pl.pallas_call(
        _proj_kernel,
        grid=(M // tm, N // tn),
        in_specs=[
            pl.BlockSpec((tm, K), lambda i, j: (i, 0)),
            pl.BlockSpec((tn, K), lambda i, j: (j, 0)),
        ],
        out_specs=pl.BlockSpec((tm, tn), lambda i, j: (i, j)),
        out_shape=jax.ShapeDtypeStruct((M, N), F32),
        scratch_shapes=[pltpu.VMEM((tm, K), BF16)],
        compiler_params=_cparams("parallel", "arbitrary"),
        name="in_proj",
    )(x, w_t)


def _rope_tables(pos, rot_dim, period, live_lanes=LANES):
    half = rot_dim // 2
    inv = ROPE_THETA ** (-jnp.arange(half, dtype=F32) / half)
    ang = pos.astype(F32)[:, None] * inv[None, :]
    cos, sin = jnp.cos(ang), jnp.sin(ang)
    lane = np.arange(LANES)
    lp = lane % period
    idx = lp % half
    live = (lp < rot_dim) & (lane < live_lanes)
    c = jnp.where(live[None, :], cos[:, idx], 1.0)
    s = jnp.where(live[None, :], jnp.where((lp < half)[None, :], -sin[:, idx], sin[:, idx]), 0.0)
    return c.astype(F32), s.astype(F32)


def _rope_tile(x, c, s, half, period):
    lane = lax.broadcasted_iota(jnp.int32, x.shape, 1)
    first = (lane & (period - 1)) < half
    partner = jnp.where(first, pltpu.roll(x, LANES - half, 1), pltpu.roll(x, half, 1))
    return x * c + partner * s


def _rope_kernel(q_ref, k_ref, v_ref, qi_ref, sm_ref, ca_ref, sa_ref, ci_ref, si_ref, cs_ref, ss_ref,
                 qo_ref, ko_ref, kb_ref, vb_ref, qio_ref, smo_ref, smb_ref):
    ca, sa = ca_ref[...], sa_ref[...]
    ci, si = ci_ref[...], si_ref[...]
    for h in range(N_HEADS):
        sl = slice(h * LANES, (h + 1) * LANES)
        qo_ref[:, sl] = (_rope_tile(q_ref[:, sl], ca, sa, ROPE_DIM // 2, HEAD_DIM) * (HEAD_DIM ** -0.5)).astype(BF16)
    for h in range(N_KV_HEADS):
        sl = slice(h * LANES, (h + 1) * LANES)
        kr = _rope_tile(k_ref[:, sl], ca, sa, ROPE_DIM // 2, HEAD_DIM)
        ko_ref[:, sl] = kr
        kb_ref[:, sl] = kr.astype(BF16)
    vb_ref[...] = v_ref[...].astype(BF16)
    for h in range(IDX_HEADS * IDX_DIM // LANES):
        sl = slice(h * LANES, (h + 1) * LANES)
        qio_ref[:, sl] = _rope_tile(qi_ref[:, sl], ci, si, IDX_ROPE_DIM // 2, IDX_DIM).astype(BF16)
    sm = _rope_tile(sm_ref[...], cs_ref[...], ss_ref[...], IDX_ROPE_DIM // 2, IDX_DIM)
    smo_ref[...] = sm
    smb_ref[...] = sm.astype(BF16)


def _rope_prep(u_attn, u_small, tabs, tm):
    M = u_attn.shape[0]
    tpos = tabs[0].shape[0]
    nt = tpos // tm

    def col(name):
        w = SIZES[name]
        return pl.BlockSpec((tm, w), lambda i, o=ATTN_OFF[name] // w: (i, o))

    tab_spec = pl.BlockSpec((tm, LANES), lambda i: (i % nt, 0))
    row = lambda w: pl.BlockSpec((tm, w), lambda i: (i, 0))
    return pl.pallas_call(
        _rope_kernel,
        grid=(M // tm,),
        in_specs=[col('q_a'), col('k_a'), col('v_a'), col('q_idx'), row(LANES)] + [tab_spec] * 6,
        out_specs=[row(SIZES['q_a']), row(SIZES['k_a']), row(SIZES['k_a']), row(SIZES['v_a']),
                   row(SIZES['q_idx']), row(LANES), row(LANES)],
        out_shape=[
            jax.ShapeDtypeStruct((M, SIZES['q_a']), BF16),
            jax.ShapeDtypeStruct((M, SIZES['k_a']), F32),
            jax.ShapeDtypeStruct((M, SIZES['k_a']), BF16),
            jax.ShapeDtypeStruct((M, SIZES['v_a']), BF16),
            jax.ShapeDtypeStruct((M, SIZES['q_idx']), BF16),
            jax.ShapeDtypeStruct((M, LANES), F32),
            jax.ShapeDtypeStruct((M, LANES), BF16),
        ],
        compiler_params=_cparams("parallel"),
        name="rope_prep",
    )(u_attn, u_attn, u_attn, u_attn, u_small, *tabs)


BISECT_UNROLL = 4


def _count(pred):
    return jnp.sum(jnp.where(pred, 1.0, 0.0), axis=-1, keepdims=True)


def _tie_index(score, kidx, thr):
    return jnp.where(score == thr, kidx, jnp.int32(2 ** 31 - 1))


def _topk_threshold(score, kidx, n_allowed, k, idx_bits):
    rows = score.shape[0]
    take_all = n_allowed <= k
    lo0 = jnp.min(jnp.where(score == -jnp.inf, jnp.inf, score), axis=-1, keepdims=True)
    hi0 = jnp.max(score, axis=-1, keepdims=True)
    lo0 = jnp.where(take_all, 0.0, lo0)
    hi0 = jnp.where(take_all, 0.0, hi0)

    def step(lo, hi, n_lo):
        mid = 0.5 * lo + 0.5 * hi
        n_mid = _count(score >= mid)
        ge = n_mid >= k
        return jnp.where(ge, mid, lo), jnp.where(ge, hi, mid), jnp.where(ge, n_mid, n_lo)

    def body(state):
        lo, hi, n_lo, _ = state
        for _ in range(BISECT_UNROLL):
            lo, hi, n_lo = step(lo, hi, n_lo)
        mid = 0.5 * lo + 0.5 * hi
        still_open = jnp.max(jnp.where((mid > lo) & (mid < hi) & (n_lo > k), 1.0, 0.0))
        return lo, hi, n_lo, still_open

    n_lo0 = jnp.where(take_all, float(k), n_allowed.astype(F32))
    lo, hi, _, _ = lax.while_loop(lambda state: state[3] > 0.5, body, (lo0, hi0, n_lo0, jnp.float32(1.0)))
    thr = jnp.where(_count(score >= hi) >= k, hi, lo)
    tie = _tie_index(score, kidx, thr)

    def tie_search():
        need = k - _count(score > thr)

        def ibody(t, j):
            cand = j + jnp.left_shift(jnp.int32(1), idx_bits - 1 - t)
            return jnp.where(_count(tie < cand) < need, cand, j)

        return lax.fori_loop(0, idx_bits, ibody, jnp.zeros((rows, 1), jnp.int32))

    repeated = jnp.max(jnp.where(take_all, 0.0, _count(score == thr))) > 1.5
    jmax = lax.cond(repeated, tie_search, lambda: jnp.full((rows, 1), 2 ** 31 - 2, jnp.int32))
    return thr, jmax, take_all


def _selected(score, kidx, thr, jmax, take_all):
    return take_all | (score > thr) | (_tie_index(score, kidx, thr) <= jmax)


def _dsa_prompt_kernel(q_ref, qi_ref, sm_ref, k_ref, v_ref, kis_ref, o_ref, *, ktop, key_step):
    tq = q_ref.shape[0]
    S = k_ref.shape[0]
    i = pl.program_id(1)
    w0 = SMALL_OFF['w_idx']

    def attend(L):
        ki = kis_ref[:L, :IDX_DIM]
        w = sm_ref[:, w0:w0 + IDX_HEADS] * (IDX_HEADS ** -0.5 * IDX_DIM ** -0.5)
        head_lane = lax.broadcasted_iota(jnp.int32, (tq, IDX_HEADS), 1)
        per_trip = 2
        width = per_trip * IDX_DIM

        def idx_heads(t, score):
            qi = qi_ref[:, pl.ds(pl.multiple_of(t * width, width), width)]
            for j in range(per_trip):
                s = _dot_nt(qi[:, j * IDX_DIM:(j + 1) * IDX_DIM], ki)
                w_h = jnp.sum(jnp.where(head_lane == t * per_trip + j, w, 0.0), axis=-1, keepdims=True)
                score = score + w_h * jnp.maximum(s, 0.0)
            return score

        score = lax.fori_loop(0, IDX_HEADS // per_trip, idx_heads, jnp.zeros((tq, L), F32))
        qpos = i * tq + lax.broadcasted_iota(jnp.int32, (tq, 1), 0)
        kidx = lax.broadcasted_iota(jnp.int32, (tq, L), 1)
        allowed = kidx <= qpos
        score = jnp.where(allowed, score, -jnp.inf)
        thr, jmax, take_all = _topk_threshold(score, kidx, qpos + 1, ktop, int(L - 1).bit_length())
        bias = jnp.where(_selected(score, kidx, thr, jmax, take_all) & allowed, 0.0, -jnp.inf)
        def head(h):
            sl = pl.ds(pl.multiple_of(h * HEAD_DIM, HEAD_DIM), HEAD_DIM)
            kv = pl.ds(pl.multiple_of((h // GROUP) * HEAD_DIM, HEAD_DIM), HEAD_DIM)
            s = _dot_nt(q_ref[:, sl], k_ref[:L, kv]) + bias
            m = jnp.max(s, axis=-1, keepdims=True)
            p = jnp.exp(s - m)
            l = jnp.sum(p, axis=-1, keepdims=True)
            o_ref[:, sl] = _dot(p.astype(BF16), v_ref[:L, kv]) / l

        def pair(t, carry):
            head(2 * t)
            head(2 * t + 1)
            return carry

        lax.fori_loop(0, N_HEADS // 2, pair, 0)

    level = ((i + 1) * tq - 1) // key_step
    for lv in range(S // key_step):
        pl.when(level == lv)(functools.partial(attend, (lv + 1) * key_step))


def _dsa_prompt(q_bf, qi_bf, small_rot, k_bf, v_bf, small_bf, B, S, tq):
    ktop = min(TOPK_MAX, S // 4)
    nq = S // tq
    key_step = min(S, 256)
    row = lambda w: pl.BlockSpec((tq, w), lambda b, i: (b * nq + i, 0))
    full = lambda w: pl.BlockSpec((S, w), lambda b, i: (b, 0))
    return pl.pallas_call(
        functools.partial(_dsa_prompt_kernel, ktop=ktop, key_step=key_step),
        grid=(B, nq),
        in_specs=[row(q_bf.shape[1]), row(qi_bf.shape[1]), row(LANES),
                  full(k_bf.shape[1]), full(v_bf.shape[1]), full(LANES)],
        out_specs=row(q_bf.shape[1]),
        out_shape=jax.ShapeDtypeStruct((B * S, q_bf.shape[1]), F32),
        compiler_params=_cparams("parallel", "arbitrary"),
        name="dsa_prompt",
    )(q_bf, qi_bf, small_rot, k_bf, v_bf, small_bf)


def _idx_score_rows(qi, w, kpage_t):
    s = _dot(qi, kpage_t)
    return jnp.sum(w * jnp.maximum(s, 0.0), axis=0, keepdims=True)


def _sample_scores_kernel(pt_ref, qi_ref, w_ref, knew_ref, *refs):
    page_refs, (o_ref, onew_ref) = refs[:-2], refs[-2:]
    npp = len(page_refs)
    p = pl.program_id(1)
    qi = qi_ref[0]
    w = w_ref[0] * (IDX_HEADS ** -0.5 * IDX_DIM ** -0.5)
    for j, page_ref in enumerate(page_refs):
        o_ref[0, pl.ds(p * npp + j, 1), :] = _idx_score_rows(qi, w, page_ref[0, 0].astype(BF16))

    @pl.when(p == 0)
    def _():
        kn = jnp.broadcast_to(knew_ref[0], (IDX_DIM, PAGE_SIZE))
        sc = _idx_score_rows(qi, w, kn)
        lane = lax.broadcasted_iota(jnp.int32, (1, PAGE_SIZE), 1)
        onew_ref[0] = jnp.where(lane == 0, sc, -jnp.inf)


def _page_specs(page_shape, npp, layer):
    zeros = (0,) * len(page_shape)
    return [pl.BlockSpec((1, 1) + page_shape, lambda b, p, pt, j=j: (layer, pt[b, p * npp + j]) + zeros)
            for j in range(npp)]


def _pages_per_step(n_pages, cap):
    npp = min(cap, n_pages)
    while n_pages % npp:
        npp -= 1
    return npp


def _sample_scores(page_table, qi3, w3, knew3, cache_ki, layer):
    DB, n_pages = page_table.shape
    npp = _pages_per_step(n_pages, 16)
    return pl.pallas_call(
        _sample_scores_kernel,
        grid_spec=pltpu.PrefetchScalarGridSpec(
            num_scalar_prefetch=1,
            grid=(DB, n_pages // npp),
            in_specs=[
                pl.BlockSpec((1, IDX_HEADS, IDX_DIM), lambda b, p, pt: (b, 0, 0)),
                pl.BlockSpec((1, IDX_HEADS, 1), lambda b, p, pt: (b, 0, 0)),
                pl.BlockSpec((1, IDX_DIM, 1), lambda b, p, pt: (b, 0, 0)),
            ] + _page_specs((IDX_DIM, PAGE_SIZE), npp, layer),
            out_specs=[
                pl.BlockSpec((1, n_pages, PAGE_SIZE), lambda b, p, pt: (b, 0, 0)),
                pl.BlockSpec((1, 1, PAGE_SIZE), lambda b, p, pt: (b, 0, 0)),
            ],
        ),
        out_shape=[jax.ShapeDtypeStruct((DB, n_pages, PAGE_SIZE), F32),
                   jax.ShapeDtypeStruct((DB, 1, PAGE_SIZE), F32)],
        compiler_params=_cparams("parallel", "arbitrary"),
        name="sample_scores",
    )(page_table, qi3, w3, knew3, *([cache_ki] * npp))


def _sample_thr_kernel(s_ref, thr_ref, jmax_ref, all_ref, *, ktop, n_valid, idx_bits):
    kidx = lax.broadcasted_iota(jnp.int32, s_ref.shape, 1)
    score = jnp.where(kidx < n_valid, s_ref[...], -jnp.inf)
    n_allowed = jnp.full((s_ref.shape[0], 1), n_valid, jnp.int32)
    thr, jmax, take_all = _topk_threshold(score, kidx, n_allowed, ktop, idx_bits)
    thr_ref[...] = jnp.broadcast_to(thr, thr_ref.shape)
    jmax_ref[...] = jnp.broadcast_to(jmax, jmax_ref.shape)
    all_ref[...] = jnp.broadcast_to(take_all.astype(jnp.int32), all_ref.shape)


def _sample_threshold(scores, ktop, n_valid):
    DB, L = scores.shape
    out = lambda dt: jax.ShapeDtypeStruct((DB, LANES), dt)
    return pl.pallas_call(
        functools.partial(_sample_thr_kernel, ktop=ktop, n_valid=n_valid, idx_bits=int(L - 1).bit_length()),
        out_shape=[out(F32), out(jnp.int32), out(jnp.int32)],
        compiler_params=pltpu.CompilerParams(vmem_limit_bytes=VMEM_LIMIT),
        name="sample_threshold",
    )(scores)


def _sample_attn_kernel(pt_ref, q_ref, s_ref, thr_ref, jmax_ref, all_ref, knew_ref, vnew_ref, *refs, n_pages):
    npp = (len(refs) - 4) // 2
    kp_refs, vp_refs = refs[:npp], refs[npp:2 * npp]
    o_ref, m_ref, l_ref, acc_ref = refs[2 * npp:]
    b = pl.program_id(0)
    p = pl.program_id(1)

    @pl.when(p == 0)
    def _():
        m_ref[...] = jnp.full_like(m_ref, NEG_BIG)
        l_ref[...] = jnp.zeros_like(l_ref)
        acc_ref[...] = jnp.zeros_like(acc_ref)

    thr = thr_ref[pl.ds(b, 1), 0:1]
    jmax = jmax_ref[pl.ds(b, 1), 0:1]
    take_all = all_ref[pl.ds(b, 1), 0:1] > 0

    q = q_ref[0]
    group = lax.broadcasted_iota(jnp.int32, (N_HEADS, 1), 0) // GROUP
    X = PAGE_SIZE * N_KV_HEADS
    own_head = (lax.broadcasted_iota(jnp.int32, (N_HEADS, X), 1) % N_KV_HEADS) == group
    repeat = (lax.broadcasted_iota(jnp.int32, (PAGE_SIZE, X), 1) // N_KV_HEADS
              == lax.broadcasted_iota(jnp.int32, (PAGE_SIZE, X), 0)).astype(BF16)

    def fold(logits, pv_fns):
        m_old = m_ref[...]
        m_new = functools.reduce(jnp.maximum, [jnp.max(s, axis=-1, keepdims=True) for s in logits], m_old)
        corr = jnp.exp(m_old - m_new)
        probs = [jnp.exp(s - m_new) for s in logits]
        l_ref[...] = l_ref[...] * corr + functools.reduce(
            lambda a, c: a + c, [jnp.sum(pr, axis=-1, keepdims=True) for pr in probs])
        acc_ref[...] = acc_ref[...] * corr + functools.reduce(
            lambda a, c: a + c, [fn(pr) for fn, pr in zip(pv_fns, probs)])
        m_ref[...] = m_new

    logits, pv_fns = [], []
    for j in range(npp):
        page = p * npp + j
        scores_row = s_ref[0, pl.ds(page, 1), :]
        kidx = page * PAGE_SIZE + lax.broadcasted_iota(jnp.int32, (1, PAGE_SIZE), 1)
        sel = _selected(scores_row, kidx, thr, jmax, take_all)
        sel_rows = _dot(jnp.broadcast_to(jnp.where(sel, 1.0, 0.0), (N_HEADS, PAGE_SIZE)).astype(BF16), repeat)
        s = _dot_nt(q, kp_refs[j][0].astype(BF16))
        logits.append(jnp.where(own_head & (sel_rows > 0.5), s, -jnp.inf))
        pv_fns.append(lambda pr, ref=vp_refs[j]: _dot(pr.astype(BF16), ref[0].astype(BF16)))
    fold(logits, pv_fns)

    @pl.when(p == pl.num_programs(1) - 1)
    def _():
        def per_head(ref):
            rows = [jnp.where(group == n, jnp.broadcast_to(ref[0, :, n * HEAD_DIM:(n + 1) * HEAD_DIM],
                                                           (N_HEADS, HEAD_DIM)), 0.0) for n in range(N_KV_HEADS)]
            return functools.reduce(lambda a, c: a + c, rows).astype(BF16).astype(F32)

        score_new = s_ref[0, pl.ds(n_pages, 1), 0:1]
        sel = _selected(score_new, jnp.full((1, 1), n_pages * PAGE_SIZE, jnp.int32), thr, jmax, take_all)
        s = jnp.sum(q.astype(F32) * per_head(knew_ref), axis=-1, keepdims=True)
        s = jnp.where(sel, s, -jnp.inf)
        vexp = per_head(vnew_ref)
        fold([s], [lambda pr: pr.astype(BF16).astype(F32) * vexp])
        o_ref[0] = acc_ref[...] / l_ref[...]


def _sample_attn(page_table, q3, scores3, thr, jmax, take_all, knew3, vnew3, cache_k, cache_v, layer):
    DB, n_pages = page_table.shape
    width = N_KV_HEADS * HEAD_DIM
    npp = _pages_per_step(n_pages, 32)
    bsel = lambda *shape: pl.BlockSpec((1,) + shape, lambda b, p, pt: (b,) + (0,) * len(shape))
    whole = pl.BlockSpec((DB, LANES), lambda b, p, pt: (0, 0))
    n_phys = cache_k.shape[1]
    rows = PAGE_SIZE * N_KV_HEADS
    cache_k = cache_k.reshape(-1, rows, HEAD_DIM)
    cache_v = cache_v.reshape(-1, rows, HEAD_DIM)
    pages = [pl.BlockSpec((1, rows, HEAD_DIM),
                          lambda b, p, pt, j=j: (layer * n_phys + pt[b, p * npp + j], 0, 0)) for j in range(npp)]
    return pl.pallas_call(
        functools.partial(_sample_attn_kernel, n_pages=n_pages),
        grid_spec=pltpu.PrefetchScalarGridSpec(
            num_scalar_prefetch=1,
            grid=(DB, n_pages // npp),
            in_specs=[bsel(N_HEADS, HEAD_DIM), bsel(n_pages + 1, PAGE_SIZE), whole, whole, whole,
                      bsel(1, width), bsel(1, width)] + pages + pages,
            out_specs=bsel(N_HEADS, HEAD_DIM),
            scratch_shapes=[pltpu.VMEM((N_HEADS, 1), F32), pltpu.VMEM((N_HEADS, 1), F32),
                            pltpu.VMEM((N_HEADS, HEAD_DIM), F32)],
        ),
        out_shape=jax.ShapeDtypeStruct((DB, N_HEADS, HEAD_DIM), F32),
        compiler_params=_cparams("parallel", "arbitrary"),
        name="sample_attn",
    )(page_table, q3, scores3, thr, jmax, take_all, knew3, vnew3, *([cache_k] * npp), *([cache_v] * npp))


CARRY = 8
MXU_DIM = 256
GDN_GROUP = MXU_DIM // CHUNK


def _spread(a, row_head, hg):
    return jnp.concatenate([jnp.where(row_head == i, a, 0.0) for i in range(hg)], axis=1)


def _gdn_prep_kernel(x_ref, w_ref, buf_ref, q_ref, k_ref, v_ref, conv_ref, xpad_ref):
    tt = x_ref.shape[1]
    t = pl.program_id(1)
    lo = CARRY - (CONV_W - 1)

    @pl.when(t == 0)
    def _():
        xpad_ref[lo:CARRY, :] = buf_ref[0]

    xpad_ref[CARRY:CARRY + tt, :] = x_ref[0]
    nh = GDN_HEADS
    for c in range(CONV_DIM // LANES):
        sl = slice(c * LANES, (c + 1) * LANES)
        y = w_ref[0:1, sl] * xpad_ref[lo:lo + tt, sl]
        for j in range(1, CONV_W):
            y = y + w_ref[j:j + 1, sl] * xpad_ref[lo + j:lo + j + tt, sl]
        y = _silu(y)
        if c < 2 * nh:
            y = y * lax.rsqrt(jnp.sum(y * y, axis=-1, keepdims=True) + NORM_EPS)
        if c < nh:
            q_ref[0, :, sl] = y * (GDN_DK ** -0.5)
        elif c < 2 * nh:
            k_ref[0, :, slice((c - nh) * LANES, (c - nh + 1) * LANES)] = y
        else:
            v_ref[0, :, slice((c - 2 * nh) * LANES, (c - 2 * nh + 1) * LANES)] = y
    last = xpad_ref[lo + tt:CARRY + tt, :]
    xpad_ref[lo:CARRY, :] = last

    @pl.when(t == pl.num_programs(1) - 1)
    def _():
        conv_ref[0] = last


def _gdn_prep(u_gdn3, conv_w, buf, tt):
    B, T, _ = u_gdn3.shape
    w = SIZES['qkv_b']
    hd = GDN_HEADS * GDN_DK
    out = jax.ShapeDtypeStruct((B, T, hd), F32)
    ospec = pl.BlockSpec((1, tt, hd), lambda b, t: (b, t, 0))
    return pl.pallas_call(
        _gdn_prep_kernel,
        grid=(B, T // tt),
        in_specs=[
            pl.BlockSpec((1, tt, w), lambda b, t: (b, t, 0)),
            pl.BlockSpec((CONV_W, w), lambda b, t: (0, 0)),
            pl.BlockSpec((1, CONV_W - 1, w), lambda b, t: (b, 0, 0)),
        ],
        out_specs=[ospec, ospec, ospec, pl.BlockSpec((1, CONV_W - 1, w), lambda b, t: (b, 0, 0))],
        out_shape=[out, out, out, jax.ShapeDtypeStruct((B, CONV_W - 1, w), F32)],
        scratch_shapes=[pltpu.VMEM((CARRY + tt, w), F32)],
        compiler_params=_cparams("parallel", "arbitrary"),
        name="gdn_prep",
    )(u_gdn3, conv_w, buf)


def _gdn_chunk_kernel(q_ref, k_ref, v_ref, z_ref, sm_ref, alog_ref, dtb_ref, gn_ref, s0_ref,
                      o_ref, sout_ref, state_ref, *, t_valid):
    C = q_ref.shape[1]
    c = pl.program_id(1)

    @pl.when(c == 0)
    def _():
        state_ref[...] = s0_ref[0]

    H = GDN_HEADS
    a0, b0 = SMALL_OFF['a_b'], SMALL_OFF['beta_b']
    live = (c * C + lax.broadcasted_iota(jnp.int32, (C, H), 0)) < t_valid
    xs = sm_ref[0, :, a0:a0 + H] + dtb_ref[...]
    softplus = jnp.maximum(xs, 0.0) + jnp.log1p(jnp.exp(-jnp.abs(xs)))
    g_all = jnp.where(live, -jnp.exp(alog_ref[...]) * softplus, 0.0)
    beta_all = jnp.where(live, jax.nn.sigmoid(sm_ref[0, :, b0:b0 + H]), 0.0)
    tri_f = (lax.broadcasted_iota(jnp.int32, (C, C), 0) >= lax.broadcasted_iota(jnp.int32, (C, C), 1)).astype(F32)
    eye_h = (lax.broadcasted_iota(jnp.int32, (H, H), 0) == lax.broadcasted_iota(jnp.int32, (H, H), 1)).astype(F32)
    gcum = _dot_hi(tri_f, g_all)
    gcum_t = _dot_nt_hi(eye_h, gcum)

    HG = GDN_GROUP
    R = HG * C
    ri = lax.broadcasted_iota(jnp.int32, (R, R), 0)
    ci = lax.broadcasted_iota(jnp.int32, (R, R), 1)
    same_head = (ri // C) == (ci // C)
    mask_incl = same_head & (ri >= ci)
    mask_strict = same_head & (ri > ci)
    eye = (ri == ci).astype(F32)
    row_head = lax.broadcasted_iota(jnp.int32, (R, 1), 0) // C
    row_head2 = jnp.concatenate([row_head, row_head], axis=0)

    groups = []
    for grp in range(H // HG):
        heads = range(grp * HG, (grp + 1) * HG)
        sls = [slice(h * GDN_DK, (h + 1) * GDN_DK) for h in heads]
        rows = lambda ref: jnp.concatenate([ref[0, :, sl] for sl in sls], axis=0)
        cols = lambda a: jnp.concatenate([a[:, h:h + 1] for h in heads], axis=0)
        q, k, v = rows(q_ref), rows(k_ref), rows(v_ref)
        beta = cols(beta_all)
        gcol = cols(gcum)
        grow = jnp.concatenate([gcum_t[h:h + 1, :] for h in heads], axis=1)
        glast = jnp.concatenate([jnp.broadcast_to(gcum[C - 1:C, h:h + 1], (C, 1)) for h in heads], axis=0)
        decay = jnp.where(mask_incl, jnp.exp(jnp.where(mask_incl, gcol - grow, 0.0)), 0.0)
        kb = k * beta
        kk_qk = _dot_nt(jnp.concatenate([kb, q], axis=0).astype(BF16), k.astype(BF16))
        nmat = jnp.where(mask_strict, kk_qk[:R] * decay, 0.0)
        groups.append(dict(
            heads=heads, sls=sls, nmat=nmat.astype(BF16), xinv=eye - nmat, qk=(kk_qk[R:] * decay).astype(BF16),
            rhs=jnp.concatenate([v * beta, kb * jnp.exp(gcol)], axis=1).astype(BF16),
            q_dec=q * jnp.exp(gcol), k_dec=k * jnp.exp(glast - gcol)))

    for _ in range(max(C - 1, 1).bit_length() - 1):
        for g in groups:
            g['resid'] = (eye - g['xinv']) - _dot(g['nmat'], g['xinv'].astype(BF16))
        for g in groups:
            g['xinv'] = g['xinv'] + _dot_bf16(g['xinv'], g['resid'])
    for g in groups:
        g['x'] = _dot(g['xinv'].astype(BF16), g['rhs'])
    for grp, g in enumerate(groups):
        u, w = g['x'][:, :GDN_DV], g['x'][:, GDN_DV:]
        s_stack = state_ref[grp * HG:(grp + 1) * HG].reshape(HG * GDN_DK, GDN_DV)
        w_q = jnp.concatenate([w, g['q_dec']], axis=0)
        ws_qs = _dot(_spread(w_q, row_head2, HG).astype(BF16), s_stack.astype(BF16))
        g['v_new'] = u - ws_qs[:R]
        g['o'] = ws_qs[R:]
    for g in groups:
        g['o'] = g['o'] + _dot(g['qk'], g['v_new'].astype(BF16))
        g['s_add'] = _dot(g['k_dec'].T.astype(BF16), _spread(g['v_new'], row_head, HG).astype(BF16))
    for g in groups:
        o = g['o']
        on = o * lax.rsqrt(jnp.mean(o * o, axis=-1, keepdims=True) + NORM_EPS) * gn_ref[...]
        for a, h in enumerate(g['heads']):
            state_ref[h] = (state_ref[h] * jnp.exp(gcum[C - 1:C, h:h + 1])
                            + g['s_add'][:, a * GDN_DV:(a + 1) * GDN_DV])
            o_ref[0, :, g['sls'][a]] = on[a * C:(a + 1) * C] * _silu(z_ref[0, :, g['sls'][a]])

    @pl.when(c == pl.num_programs(1) - 1)
    def _():
        sout_ref[0] = state_ref[...]


def _gdn_chunks(qn, kn, vv, u_gate3, small3, a_log, dt_bias, gn, s0, t_valid):
    B, Tp, hd = qn.shape
    nc = Tp // CHUNK
    blk = pl.BlockSpec((1, CHUNK, hd), lambda b, c: (b, c, 0))
    vec = lambda w: pl.BlockSpec((1, w), lambda b, c: (0, 0))
    st = pl.BlockSpec((1, GDN_HEADS, GDN_DK, GDN_DV), lambda b, c: (b, 0, 0, 0))
    return pl.pallas_call(
        functools.partial(_gdn_chunk_kernel, t_valid=t_valid),
        grid=(B, nc),
        in_specs=[blk, blk, blk,
                  pl.BlockSpec((1, CHUNK, hd), lambda b, c, o=GATE_OFF['z_b'] // hd: (b, c, o)),
                  pl.BlockSpec((1, CHUNK, LANES), lambda b, c: (b, c, 0)),
                  vec(GDN_HEADS), vec(GDN_HEADS), vec(GDN_DV), st],
        out_specs=[blk, st],
        out_shape=[jax.ShapeDtypeStruct((B, Tp, hd), F32),
                   jax.ShapeDtypeStruct((B, GDN_HEADS, GDN_DK, GDN_DV), F32)],
        scratch_shapes=[pltpu.VMEM((GDN_HEADS, GDN_DK, GDN_DV), F32)],
        compiler_params=_cparams("parallel", "arbitrary"),
        name="gdn_chunks",
    )(qn, kn, vv, u_gate3, small3, a_log, dt_bias, gn, s0)


def _merge_kernel(x_ref, oa_ref, ob_ref, ga_ref, gb_ref, wo_ref, g_ref, b_ref, o_ref):
    merged = jax.nn.sigmoid(ga_ref[...]) * oa_ref[...] + jax.nn.sigmoid(gb_ref[...]) * ob_ref[...]
    y = DN_ALPHA * x_ref[...] + _dot(merged.astype(BF16), wo_ref[...])
    o_ref[...] = _layer_norm(y, g_ref[...], b_ref[...])


def _merge_proj_ln(x, o_a, o_b, u_gate, w_o, g, b, tm):
    M, D = x.shape
    row = pl.BlockSpec((tm, D), lambda i: (i, 0))
    col = lambda name: pl.BlockSpec((tm, D), lambda i, o=GATE_OFF[name] // D: (i, o))
    vec = pl.BlockSpec((1, D), lambda i: (0, 0))
    return pl.pallas_call(
        _merge_kernel,
        grid=(M // tm,),
        in_specs=[row, row, row, col('gate_a'), col('gate_b'),
                  pl.BlockSpec((D, D), lambda i: (0, 0)), vec, vec],
        out_specs=row,
        out_shape=jax.ShapeDtypeStruct((M, D), F32),
        compiler_params=_cparams("parallel"),
        name="merge_proj_ln",
    )(x, o_a, o_b, u_gate, u_gate, w_o, g, b)


def _tiles(M):
    if M % 1024 == 0:
        return 512, 1024, 256
    return M, M, M


def _layer(x, B, T, pos, wts, conv_buf, ssm0, dsa_fn):
    M = B * T
    tm, tp, te = _tiles(M)
    row = lambda a: a.reshape(1, -1)
    x1 = _ffn_ln(x, wts['ffn1_g'], wts['ffn1_u'], wts['ffn1_d'], row(wts['ln1_g']), row(wts['ln1_b']),
                 tm, 512)
    u_attn = _proj(x1, wts['w_attn'], tp, 1024)
    u_gdn = _proj(x1, wts['w_gdn'], tp, 1024)
    u_gate = _proj(x1, wts['w_gate'], tp, 1024)
    u_small = _proj(x1, wts['w_small'], tp, LANES)

    pos_rows = pos if T > 1 else jnp.broadcast_to(pos, (M,))
    tabs = (_rope_tables(pos_rows, ROPE_DIM, HEAD_DIM) + _rope_tables(pos_rows, IDX_ROPE_DIM, IDX_DIM)
            + _rope_tables(pos_rows, IDX_ROPE_DIM, IDX_DIM, live_lanes=IDX_DIM))
    q_bf, k_rot, k_bf, v_bf, qi_bf, small_rot, small_bf = _rope_prep(u_attn, u_small, tabs, te)
    v_rows = u_attn[:, ATTN_OFF['v_a']:ATTN_OFF['v_a'] + SIZES['v_a']]
    o_a = dsa_fn(v_rows, q_bf, k_rot, k_bf, v_bf, qi_bf, small_rot, small_bf)

    qn, kn, vv, conv_new = _gdn_prep(u_gdn.reshape(B, T, -1), wts['conv_w'], conv_buf, min(T, 256))
    pad = (-T) % CHUNK
    pad3 = lambda a: jnp.pad(a, ((0, 0), (0, pad), (0, 0))) if pad else a
    o_b, ssm_new = _gdn_chunks(pad3(qn), pad3(kn), pad3(vv), pad3(u_gate.reshape(B, T, -1)),
                               pad3(u_small.reshape(B, T, LANES)),
                               row(wts['a_log']), row(wts['dt_bias']), row(wts['gdn_norm_g']), ssm0, T)
    o_b = o_b[:, :T].reshape(M, D_MODEL)

    x2 = _merge_proj_ln(x1, o_a, o_b, u_gate, wts['w_o'], row(wts['ln2_g']), row(wts['ln2_b']), te)
    y = _ffn_ln(x2, wts['ffn2_g'], wts['ffn2_u'], wts['ffn2_d'], row(wts['ln3_g']), row(wts['ln3_b']),
                tm, 512)
    ki_rows = small_rot[:, SMALL_OFF['k_idx']:SMALL_OFF['k_idx'] + IDX_DIM]
    return y, (k_rot, v_rows, ki_rows, ssm_new, conv_new)


def _split_w_in(w_in):
    offs = dict(zip([nm for nm, _ in IN_SPLITS], np.cumsum([0] + [n for _, n in IN_SPLITS])))
    w_t = jnp.swapaxes(w_in, 0, 1)

    def span(order):
        lo = offs[order[0]]
        hi = offs[order[-1]] + SIZES[order[-1]]
        assert hi - lo == sum(SIZES[nm] for nm in order)
        return w_t[lo:hi].astype(BF16)

    w_small = jnp.concatenate([w_t[offs[nm]:offs[nm] + SIZES[nm]] for nm in SMALL_ORDER], axis=0)
    w_small = jnp.pad(w_small, ((0, LANES - SMALL_USED), (0, 0))).astype(BF16)
    return span(ATTN_ORDER), span(GDN_ORDER), span(GATE_ORDER), w_small


def kernel(x_prompt, x_sample, cache_k, cache_v, cache_idx_k, state_ssm, state_conv, page_table, ffn1_w_gate, ffn1_w_up, ffn1_w_down, ln1_g, ln1_b, w_in, conv_w, a_log, dt_bias, gdn_norm_g, w_o, ln2_g, ln2_b, ffn2_w_gate, ffn2_w_up, ffn2_w_down, ln3_g, ln3_b):
    B, S, _ = x_prompt.shape
    DB, T, _ = x_sample.shape
    assert T == 1, "the sample path handles one new token per sequence"
    n_pages = page_table.shape[1]
    yp = x_prompt.reshape(B * S, D_MODEL)
    ys = x_sample.reshape(DB * T, D_MODEL)
    outs_p, outs_s = [], []
    for l in range(ffn1_w_gate.shape[0]):
        w_attn, w_gdn, w_gate, w_small = _split_w_in(w_in[l])
        wts = dict(
            ffn1_g=ffn1_w_gate[l].astype(BF16), ffn1_u=ffn1_w_up[l].astype(BF16), ffn1_d=ffn1_w_down[l].astype(BF16),
            ffn2_g=ffn2_w_gate[l].astype(BF16), ffn2_u=ffn2_w_up[l].astype(BF16), ffn2_d=ffn2_w_down[l].astype(BF16),
            ln1_g=ln1_g[l], ln1_b=ln1_b[l], ln2_g=ln2_g[l], ln2_b=ln2_b[l], ln3_g=ln3_g[l], ln3_b=ln3_b[l],
            w_attn=w_attn, w_gdn=w_gdn, w_gate=w_gate, w_small=w_small, w_o=w_o[l].astype(BF16), conv_w=conv_w[l],
            a_log=a_log[l], dt_bias=dt_bias[l], gdn_norm_g=gdn_norm_g[l],
        )

        def dsa_p(v_rows, q_bf, k_rot, k_bf, v_bf, qi_bf, small_rot, small_bf):
            return _dsa_prompt(q_bf, qi_bf, small_rot, k_bf, v_bf, small_bf, B, S, min(S, 256))

        def dsa_s(v_rows, q_bf, k_rot, k_bf, v_bf, qi_bf, small_rot, small_bf, l=l):
            w0 = SMALL_OFF['w_idx']
            w3 = small_rot[:, w0:w0 + IDX_HEADS].reshape(DB, IDX_HEADS, 1)
            qi3 = qi_bf.reshape(DB, IDX_HEADS, IDX_DIM)
            knew_i = small_bf[:, :IDX_DIM].reshape(DB, IDX_DIM, 1)
            width = N_KV_HEADS * HEAD_DIM
            past, new = _sample_scores(page_table, qi3, w3, knew_i, jnp.swapaxes(cache_idx_k, 2, 3), l)
            scores3 = jnp.concatenate([past, new], axis=1)
            n_keys = n_pages * PAGE_SIZE + T
            ktop = min(TOPK_MAX, n_keys // 4)
            thr, jmax, take_all = _sample_threshold(scores3.reshape(DB, -1), ktop, n_keys)
            o = _sample_attn(page_table, q_bf.reshape(DB, N_HEADS, HEAD_DIM), scores3, thr, jmax, take_all,
                             k_rot.reshape(DB, 1, width), v_rows.reshape(DB, 1, width), cache_k, cache_v, l)
            return o.reshape(DB, D_MODEL)

        conv0 = jnp.zeros((B, CONV_W - 1, CONV_DIM), F32)
        ssm_zero = jnp.zeros((B, GDN_HEADS, GDN_DK, GDN_DV), F32)
        yp, st_p = _layer(yp, B, S, jnp.arange(S, dtype=jnp.int32), wts, conv0, ssm_zero, dsa_p)
        past_len = n_pages * PAGE_SIZE
        ys, st_s = _layer(ys, DB, T, past_len + jnp.arange(T, dtype=jnp.int32), wts, state_conv[l], state_ssm[l], dsa_s)
        outs_p.append(st_p)
        outs_s.append(st_s)

    def stack(outs, nb, nt):
        d = len(outs)
        k, v, ki, ssm, conv = [a[0][None] if d == 1 else jnp.stack(a) for a in zip(*outs)]
        return (k.reshape(d, nb, nt, N_KV_HEADS, HEAD_DIM), v.reshape(d, nb, nt, N_KV_HEADS, HEAD_DIM),
                ki.reshape(d, nb, nt, IDX_DIM), ssm, conv)

    return (yp.reshape(B, S, D_MODEL), ys.reshape(DB, T, D_MODEL)) + stack(outs_p, B, S) + stack(outs_s, DB, T)
```

```python
import functools

import jax
import jax.numpy as jnp
import numpy as np
from jax import lax
from jax.experimental import pallas as pl
from jax.experimental.pallas import tpu as pltpu

D_MODEL = 2048
PAGE_SIZE = 128
HEAD_DIM = 128
N_HEADS = D_MODEL // HEAD_DIM
N_KV_HEADS = 4
GROUP = N_HEADS // N_KV_HEADS
ROPE_DIM = HEAD_DIM // 4
IDX_HEADS = 16
IDX_DIM = 64
IDX_ROPE_DIM = IDX_DIM // 4
TOPK_MAX = 256
ROPE_THETA = 500000.0
GDN_DK = 128
GDN_DV = 128
GDN_HEADS = D_MODEL // GDN_DV
CONV_W = 4
CONV_DIM = 2 * GDN_HEADS * GDN_DK + GDN_HEADS * GDN_DV
CHUNK = 64
LN_EPS = 1e-5
NORM_EPS = 1e-6
DEPTH = 1
DN_ALPHA = (2 * DEPTH) ** 0.25

IN_SPLITS = (
    ('q_a', N_HEADS * HEAD_DIM), ('k_a', N_KV_HEADS * HEAD_DIM), ('v_a', N_KV_HEADS * HEAD_DIM),
    ('q_idx', IDX_HEADS * IDX_DIM), ('k_idx', IDX_DIM), ('w_idx', IDX_HEADS),
    ('qkv_b', CONV_DIM), ('a_b', GDN_HEADS), ('beta_b', GDN_HEADS), ('z_b', GDN_HEADS * GDN_DV),
    ('gate_a', D_MODEL), ('gate_b', D_MODEL),
)
ATTN_ORDER = ('q_a', 'k_a', 'v_a', 'q_idx')
GDN_ORDER = ('qkv_b',)
GATE_ORDER = ('z_b', 'gate_a', 'gate_b')
SMALL_ORDER = ('k_idx', 'w_idx', 'a_b', 'beta_b')
LANES = 128
VMEM_LIMIT = 56 * 1024 * 1024
NEG_BIG = -1e30

F32 = jnp.float32
BF16 = jnp.bfloat16


def _offsets(order):
    sizes = dict(IN_SPLITS)
    offs, o = {}, 0
    for nm in order:
        offs[nm] = o
        o += sizes[nm]
    return offs, o


ATTN_OFF, ATTN_COLS = _offsets(ATTN_ORDER)
GATE_OFF, _ = _offsets(GATE_ORDER)
SMALL_OFF, SMALL_USED = _offsets(SMALL_ORDER)
SIZES = dict(IN_SPLITS)


def _cparams(*sem):
    return pltpu.CompilerParams(dimension_semantics=sem, vmem_limit_bytes=VMEM_LIMIT)


def _dot(a, b):
    return jnp.dot(a, b, preferred_element_type=F32)


def _dot_nt(a, b):
    return lax.dot_general(a, b, (((1,), (1,)), ((), ())), preferred_element_type=F32)


def _dot_hi(a, b):
    return jnp.dot(a, b, preferred_element_type=F32, precision=lax.Precision.HIGHEST)


def _dot_nt_hi(a, b):
    return lax.dot_general(a, b, (((1,), (1,)), ((), ())), preferred_element_type=F32,
                           precision=lax.Precision.HIGHEST)


def _dot_bf16(a, b):
    return _dot(a.astype(BF16), b.astype(BF16))


def _silu(x):
    return x * jax.nn.sigmoid(x)


def _layer_norm(y, g, b):
    mu = jnp.mean(y, axis=-1, keepdims=True)
    d = y - mu
    var = jnp.mean(d * d, axis=-1, keepdims=True)
    return d * lax.rsqrt(var + LN_EPS) * g + b


def _ffn_ln_kernel(x_ref, wg_ref, wu_ref, wd_ref, g_ref, b_ref, o_ref, acc_ref, xb_ref):
    j = pl.program_id(1)

    @pl.when(j == 0)
    def _():
        acc_ref[...] = jnp.zeros_like(acc_ref)
        xb_ref[...] = x_ref[...].astype(BF16)

    xb = xb_ref[...]
    hg = _dot(xb, wg_ref[...])
    hu = _dot(xb, wu_ref[...])
    h = _silu(hg) * hu
    acc_ref[...] += _dot(h.astype(BF16), wd_ref[...])

    @pl.when(j == pl.num_programs(1) - 1)
    def _():
        y = DN_ALPHA * x_ref[...] + 0.5 * acc_ref[...]
        o_ref[...] = _layer_norm(y, g_ref[...], b_ref[...])


def _ffn_ln(x, wg, wu, wd, g, b, tm, tf):
    M, D = x.shape
    F = wg.shape[1]
    return pl.pallas_call(
        _ffn_ln_kernel,
        grid=(M // tm, F // tf),
        in_specs=[
            pl.BlockSpec((tm, D), lambda i, j: (i, 0)),
            pl.BlockSpec((D, tf), lambda i, j: (0, j)),
            pl.BlockSpec((D, tf), lambda i, j: (0, j)),
            pl.BlockSpec((tf, D), lambda i, j: (j, 0)),
            pl.BlockSpec((1, D), lambda i, j: (0, 0)),
            pl.BlockSpec((1, D), lambda i, j: (0, 0)),
        ],
        out_specs=pl.BlockSpec((tm, D), lambda i, j: (i, 0)),
        out_shape=jax.ShapeDtypeStruct((M, D), F32),
        scratch_shapes=[pltpu.VMEM((tm, D), F32), pltpu.VMEM((tm, D), BF16)],
        compiler_params=_cparams("parallel", "arbitrary"),
        name="ffn_ln",
    )(x, wg, wu, wd, g, b)


def _proj_kernel(x_ref, w_ref, o_ref, xb_ref):
    @pl.when(pl.program_id(1) == 0)
    def _():
        xb_ref[...] = x_ref[...].astype(BF16)

    o_ref[...] = _dot_nt(xb_ref[...], w_ref[...])


def _proj(x, w_t, tm, tn):
    M, K = x.shape
    N = w_t.shape[0]
    return pl.pallas_call(
        _proj_kernel,
        grid=(M // tm, N // tn),
        in_specs=[
            pl.BlockSpec((tm, K), lambda i, j: (i, 0)),
            pl.BlockSpec((tn, K), lambda i, j: (j, 0)),
        ],
        out_specs=pl.BlockSpec((tm, tn), lambda i, j: (i, j)),
        out_shape=jax.ShapeDtypeStruct((M, N), F32),
        scratch_shapes=[pltpu.VMEM((tm, K), BF16)],
        compiler_params=_cparams("parallel", "arbitrary"),
        name="in_proj",
    )(x, w_t)


def _rope_tables(pos, rot_dim, period, live_lanes=LANES):
    half = rot_dim // 2
    inv = ROPE_THETA ** (-jnp.arange(half, dtype=F32) / half)
    ang = pos.astype(F32)[:, None] * inv[None, :]
    cos, sin = jnp.cos(ang), jnp.sin(ang)
    lane = np.arange(LANES)
    lp = lane % period
    idx = lp % half
    live = (lp < rot_dim) & (lane < live_lanes)
    c = jnp.where(live[None, :], cos[:, idx], 1.0)
    s = jnp.where(live[None, :], jnp.where((lp < half)[None, :], -sin[:, idx], sin[:, idx]), 0.0)
    return c.astype(F32), s.astype(F32)


def _rope_tile(x, c, s, half, period):
    lane = lax.broadcasted_iota(jnp.int32, x.shape, 1)
    first = (lane & (period - 1)) < half
    partner = jnp.where(first, pltpu.roll(x, LANES - half, 1), pltpu.roll(x, half, 1))
    return x * c + partner * s


def _rope_kernel(q_ref, k_ref, v_ref, qi_ref, sm_ref, ca_ref, sa_ref, ci_ref, si_ref, cs_ref, ss_ref,
                 qo_ref, ko_ref, kb_ref, vb_ref, qio_ref, smo_ref, smb_ref):
    ca, sa = ca_ref[...], sa_ref[...]
    ci, si = ci_ref[...], si_ref[...]
    for h in range(N_HEADS):
        sl = slice(h * LANES, (h + 1) * LANES)
        qo_ref[:, sl] = (_rope_tile(q_ref[:, sl], ca, sa, ROPE_DIM // 2, HEAD_DIM) * (HEAD_DIM ** -0.5)).astype(BF16)
    for h in range(N_KV_HEADS):
        sl = slice(h * LANES, (h + 1) * LANES)
        kr = _rope_tile(k_ref[:, sl], ca, sa, ROPE_DIM // 2, HEAD_DIM)
        ko_ref[:, sl] = kr
        kb_ref[:, sl] = kr.astype(BF16)
    vb_ref[...] = v_ref[...].astype(BF16)
    for h in range(IDX_HEADS * IDX_DIM // LANES):
        sl = slice(h * LANES, (h + 1) * LANES)
        qio_ref[:, sl] = _rope_tile(qi_ref[:, sl], ci, si, IDX_ROPE_DIM // 2, IDX_DIM).astype(BF16)
    sm = _rope_tile(sm_ref[...], cs_ref[...], ss_ref[...], IDX_ROPE_DIM // 2, IDX_DIM)
    smo_ref[...] = sm
    smb_ref[...] = sm.astype(BF16)


def _rope_prep(u_attn, u_small, tabs, tm):
    M = u_attn.shape[0]
    tpos = tabs[0].shape[0]
    nt = tpos // tm

    def col(name):
        w = SIZES[name]
        return pl.BlockSpec((tm, w), lambda i, o=ATTN_OFF[name] // w: (i, o))

    tab_spec = pl.BlockSpec((tm, LANES), lambda i: (i % nt, 0))
    row = lambda w: pl.BlockSpec((tm, w), lambda i: (i, 0))
    return pl.pallas_call(
        _rope_kernel,
        grid=(M // tm,),
        in_specs=[col('q_a'), col('k_a'), col('v_a'), col('q_idx'), row(LANES)] + [tab_spec] * 6,
        out_specs=[row(SIZES['q_a']), row(SIZES['k_a']), row(SIZES['k_a']), row(SIZES['v_a']),
                   row(SIZES['q_idx']), row(LANES), row(LANES)],
        out_shape=[
            jax.ShapeDtypeStruct((M, SIZES['q_a']), BF16),
            jax.ShapeDtypeStruct((M, SIZES['k_a']), F32),
            jax.ShapeDtypeStruct((M, SIZES['k_a']), BF16),
            jax.ShapeDtypeStruct((M, SIZES['v_a']), BF16),
            jax.ShapeDtypeStruct((M, SIZES['q_idx']), BF16),
            jax.ShapeDtypeStruct((M, LANES), F32),
            jax.ShapeDtypeStruct((M, LANES), BF16),
        ],
        compiler_params=_cparams("parallel"),
        name="rope_prep",
    )(u_attn, u_attn, u_attn, u_attn, u_small, *tabs)


BISECT_UNROLL = 4


def _count(pred):
    return jnp.sum(jnp.where(pred, 1.0, 0.0), axis=-1, keepdims=True)


def _tie_index(score, kidx, thr):
    return jnp.where(score == thr, kidx, jnp.int32(2 ** 31 - 1))


def _topk_threshold(score, kidx, n_allowed, k, idx_bits):
    rows = score.shape[0]
    take_all = n_allowed <= k
    lo0 = jnp.min(jnp.where(score == -jnp.inf, jnp.inf, score), axis=-1, keepdims=True)
    hi0 = jnp.max(score, axis=-1, keepdims=True)
    lo0 = jnp.where(take_all, 0.0, lo0)
    hi0 = jnp.where(take_all, 0.0, hi0)

    def step(lo, hi, n_lo):
        mid = 0.5 * lo + 0.5 * hi
        n_mid = _count(score >= mid)
        ge = n_mid >= k
        return jnp.where(ge, mid, lo), jnp.where(ge, hi, mid), jnp.where(ge, n_mid, n_lo)

    def body(state):
        lo, hi, n_lo, _ = state
        for _ in range(BISECT_UNROLL):
            lo, hi, n_lo = step(lo, hi, n_lo)
        mid = 0.5 * lo + 0.5 * hi
        still_open = jnp.max(jnp.where((mid > lo) & (mid < hi) & (n_lo > k), 1.0, 0.0))
        return lo, hi, n_lo, still_open

    n_lo0 = jnp.where(take_all, float(k), n_allowed.astype(F32))
    lo, hi, _, _ = lax.while_loop(lambda state: state[3] > 0.5, body, (lo0, hi0, n_lo0, jnp.float32(1.0)))
    thr = jnp.where(_count(score >= hi) >= k, hi, lo)
    tie = _tie_index(score, kidx, thr)

    def tie_search():
        need = k - _count(score > thr)

        def ibody(t, j):
            cand = j + jnp.left_shift(jnp.int32(1), idx_bits - 1 - t)
            return jnp.where(_count(tie < cand) < need, cand, j)

        return lax.fori_loop(0, idx_bits, ibody, jnp.zeros((rows, 1), jnp.int32))

    repeated = jnp.max(jnp.where(take_all, 0.0, _count(score == thr))) > 1.5
    jmax = lax.cond(repeated, tie_search, lambda: jnp.full((rows, 1), 2 ** 31 - 2, jnp.int32))
    return thr, jmax, take_all


def _selected(score, kidx, thr, jmax, take_all):
    return take_all | (score > thr) | (_tie_index(score, kidx, thr) <= jmax)


def _dsa_prompt_kernel(q_ref, qi_ref, sm_ref, k_ref, v_ref, kis_ref, o_ref, *, ktop, key_step):
    tq = q_ref.shape[0]
    S = k_ref.shape[0]
    i = pl.program_id(1)
    w0 = SMALL_OFF['w_idx']

    def attend(L):
        ki = kis_ref[:L, :IDX_DIM]
        w = sm_ref[:, w0:w0 + IDX_HEADS] * (IDX_HEADS ** -0.5 * IDX_DIM ** -0.5)
        head_lane = lax.broadcasted_iota(jnp.int32, (tq, IDX_HEADS), 1)
        per_trip = 2
        width = per_trip * IDX_DIM

        def idx_heads(t, score):
            qi = qi_ref[:, pl.ds(pl.multiple_of(t * width, width), width)]
            for j in range(per_trip):
                s = _dot_nt(qi[:, j * IDX_DIM:(j + 1) * IDX_DIM], ki)
                w_h = jnp.sum(jnp.where(head_lane == t * per_trip + j, w, 0.0), axis=-1, keepdims=True)
                score = score + w_h * jnp.maximum(s, 0.0)
            return score

        score = lax.fori_loop(0, IDX_HEADS // per_trip, idx_heads, jnp.zeros((tq, L), F32))
        qpos = i * tq + lax.broadcasted_iota(jnp.int32, (tq, 1), 0)
        kidx = lax.broadcasted_iota(jnp.int32, (tq, L), 1)
        allowed = kidx <= qpos
        score = jnp.where(allowed, score, -jnp.inf)
        thr, jmax, take_all = _topk_threshold(score, kidx, qpos + 1, ktop, int(L - 1).bit_length())
        bias = jnp.where(_selected(score, kidx, thr, jmax, take_all) & allowed, 0.0, -jnp.inf)
        def head(h):
            sl = pl.ds(pl.multiple_of(h * HEAD_DIM, HEAD_DIM), HEAD_DIM)
            kv = pl.ds(pl.multiple_of((h // GROUP) * HEAD_DIM, HEAD_DIM), HEAD_DIM)
            s = _dot_nt(q_ref[:, sl], k_ref[:L, kv]) + bias
            m = jnp.max(s, axis=-1, keepdims=True)
            p = jnp.exp(s - m)
            l = jnp.sum(p, axis=-1, keepdims=True)
            o_ref[:, sl] = _dot(p.astype(BF16), v_ref[:L, kv]) / l

        def pair(t, carry):
            head(2 * t)
            head(2 * t + 1)
            return carry

        lax.fori_loop(0, N_HEADS // 2, pair, 0)

    level = ((i + 1) * tq - 1) // key_step
    for lv in range(S // key_step):
        pl.when(level == lv)(functools.partial(attend, (lv + 1) * key_step))


def _dsa_prompt(q_bf, qi_bf, small_rot, k_bf, v_bf, small_bf, B, S, tq):
    ktop = min(TOPK_MAX, S // 4)
    nq = S // tq
    key_step = min(S, 256)
    row = lambda w: pl.BlockSpec((tq, w), lambda b, i: (b * nq + i, 0))
    full = lambda w: pl.BlockSpec((S, w), lambda b, i: (b, 0))
    return pl.pallas_call(
        functools.partial(_dsa_prompt_kernel, ktop=ktop, key_step=key_step),
        grid=(B, nq),
        in_specs=[row(q_bf.shape[1]), row(qi_bf.shape[1]), row(LANES),
                  full(k_bf.shape[1]), full(v_bf.shape[1]), full(LANES)],
        out_specs=row(q_bf.shape[1]),
        out_shape=jax.ShapeDtypeStruct((B * S, q_bf.shape[1]), F32),
        compiler_params=_cparams("parallel", "arbitrary"),
        name="dsa_prompt",
    )(q_bf, qi_bf, small_rot, k_bf, v_bf, small_bf)


def _idx_score_rows(qi, w, kpage_t):
    s = _dot(qi, kpage_t)
    return jnp.sum(w * jnp.maximum(s, 0.0), axis=0, keepdims=True)


def _sample_scores_kernel(pt_ref, qi_ref, w_ref, knew_ref, *refs):
    page_refs, (o_ref, onew_ref) = refs[:-2], refs[-2:]
    npp = len(page_refs)
    p = pl.program_id(1)
    qi = qi_ref[0]
    w = w_ref[0] * (IDX_HEADS ** -0.5 * IDX_DIM ** -0.5)
    for j, page_ref in enumerate(page_refs):
        o_ref[0, pl.ds(p * npp + j, 1), :] = _idx_score_rows(qi, w, page_ref[0, 0].astype(BF16))

    @pl.when(p == 0)
    def _():
        kn = jnp.broadcast_to(knew_ref[0], (IDX_DIM, PAGE_SIZE))
        sc = _idx_score_rows(qi, w, kn)
        lane = lax.broadcasted_iota(jnp.int32, (1, PAGE_SIZE), 1)
        onew_ref[0] = jnp.where(lane == 0, sc, -jnp.inf)


def _page_specs(page_shape, npp, layer):
    zeros = (0,) * len(page_shape)
    return [pl.BlockSpec((1, 1) + page_shape, lambda b, p, pt, j=j: (layer, pt[b, p * npp + j]) + zeros)
            for j in range(npp)]


def _pages_per_step(n_pages, cap):
    npp = min(cap, n_pages)
    while n_pages % npp:
        npp -= 1
    return npp


def _sample_scores(page_table, qi3, w3, knew3, cache_ki, layer):
    DB, n_pages = page_table.shape
    npp = _pages_per_step(n_pages, 128)
    return pl.pallas_call(
        _sample_scores_kernel,
        grid_spec=pltpu.PrefetchScalarGridSpec(
            num_scalar_prefetch=1,
            grid=(DB, n_pages // npp),
            in_specs=[
                pl.BlockSpec((1, IDX_HEADS, IDX_DIM), lambda b, p, pt: (b, 0, 0)),
                pl.BlockSpec((1, IDX_HEADS, 1), lambda b, p, pt: (b, 0, 0)),
                pl.BlockSpec((1, IDX_DIM, 1), lambda b, p, pt: (b, 0, 0)),
            ] + _page_specs((IDX_DIM, PAGE_SIZE), npp, layer),
            out_specs=[
                pl.BlockSpec((1, n_pages, PAGE_SIZE), lambda b, p, pt: (b, 0, 0)),
                pl.BlockSpec((1, 1, PAGE_SIZE), lambda b, p, pt: (b, 0, 0)),
            ],
        ),
        out_shape=[jax.ShapeDtypeStruct((DB, n_pages, PAGE_SIZE), F32),
                   jax.ShapeDtypeStruct((DB, 1, PAGE_SIZE), F32)],
        compiler_params=_cparams("parallel", "arbitrary"),
        name="sample_scores",
    )(page_table, qi3, w3, knew3, *([cache_ki] * npp))


def _sample_thr_kernel(s_ref, thr_ref, jmax_ref, all_ref, *, ktop, n_valid, idx_bits):
    kidx = lax.broadcasted_iota(jnp.int32, s_ref.shape, 1)
    score = jnp.where(kidx < n_valid, s_ref[...], -jnp.inf)
    n_allowed = jnp.full((s_ref.shape[0], 1), n_valid, jnp.int32)
    thr, jmax, take_all = _topk_threshold(score, kidx, n_allowed, ktop, idx_bits)
    thr_ref[...] = jnp.broadcast_to(thr, thr_ref.shape)
    jmax_ref[...] = jnp.broadcast_to(jmax, jmax_ref.shape)
    all_ref[...] = jnp.broadcast_to(take_all.astype(jnp.int32), all_ref.shape)


def _sample_threshold(scores, ktop, n_valid):
    DB, L = scores.shape
    out = lambda dt: jax.ShapeDtypeStruct((DB, LANES), dt)
    return pl.pallas_call(
        functools.partial(_sample_thr_kernel, ktop=ktop, n_valid=n_valid, idx_bits=int(L - 1).bit_length()),
        out_shape=[out(F32), out(jnp.int32), out(jnp.int32)],
        compiler_params=pltpu.CompilerParams(vmem_limit_bytes=VMEM_LIMIT),
        name="sample_threshold",
    )(scores)


def _sample_attn_kernel(pt_ref, q_ref, s_ref, thr_ref, jmax_ref, all_ref, knew_ref, vnew_ref, *refs, n_pages):
    npp = (len(refs) - 4) // 2
    kp_refs, vp_refs = refs[:npp], refs[npp:2 * npp]
    o_ref, m_ref, l_ref, acc_ref = refs[2 * npp:]
    b = pl.program_id(0)
    p = pl.program_id(1)

    @pl.when(p == 0)
    def _():
        m_ref[...] = jnp.full_like(m_ref, NEG_BIG)
        l_ref[...] = jnp.zeros_like(l_ref)
        acc_ref[...] = jnp.zeros_like(acc_ref)

    thr = thr_ref[pl.ds(b, 1), 0:1]
    jmax = jmax_ref[pl.ds(b, 1), 0:1]
    take_all = all_ref[pl.ds(b, 1), 0:1] > 0

    q = q_ref[0]
    group = lax.broadcasted_iota(jnp.int32, (N_HEADS, 1), 0) // GROUP
    X = PAGE_SIZE * N_KV_HEADS
    own_head = (lax.broadcasted_iota(jnp.int32, (N_HEADS, X), 1) % N_KV_HEADS) == group
    repeat = (lax.broadcasted_iota(jnp.int32, (PAGE_SIZE, X), 1) // N_KV_HEADS
              == lax.broadcasted_iota(jnp.int32, (PAGE_SIZE, X), 0)).astype(BF16)

    def fold(logits, pv_fns):
        m_old = m_ref[...]
        m_new = functools.reduce(jnp.maximum, [jnp.max(s, axis=-1, keepdims=True) for s in logits], m_old)
        corr = jnp.exp(m_old - m_new)
        probs = [jnp.exp(s - m_new) for s in logits]
        l_ref[...] = l_ref[...] * corr + functools.reduce(
            lambda a, c: a + c, [jnp.sum(pr, axis=-1, keepdims=True) for pr in probs])
        acc_ref[...] = acc_ref[...] * corr + functools.reduce(
            lambda a, c: a + c, [fn(pr) for fn, pr in zip(pv_fns, probs)])
        m_ref[...] = m_new

    logits, pv_fns = [], []
    for j in range(npp):
        page = p * npp + j
        scores_row = s_ref[0, pl.ds(page, 1), :]
        kidx = page * PAGE_SIZE + lax.broadcasted_iota(jnp.int32, (1, PAGE_SIZE), 1)
        sel = _selected(scores_row, kidx, thr, jmax, take_all)
        sel_rows = _dot(jnp.broadcast_to(jnp.where(sel, 1.0, 0.0), (N_HEADS, PAGE_SIZE)).astype(BF16), repeat)
        s = _dot_nt(q, kp_refs[j][0].astype(BF16))
        logits.append(jnp.where(own_head & (sel_rows > 0.5), s, -jnp.inf))
        pv_fns.append(lambda pr, ref=vp_refs[j]: _dot(pr.astype(BF16), ref[0].astype(BF16)))
    fold(logits, pv_fns)

    @pl.when(p == pl.num_programs(1) - 1)
    def _():
        def per_head(ref):
            rows = [jnp.where(group == n, jnp.broadcast_to(ref[0, :, n * HEAD_DIM:(n + 1) * HEAD_DIM],
                                                           (N_HEADS, HEAD_DIM)), 0.0) for n in range(N_KV_HEADS)]
            return functools.reduce(lambda a, c: a + c, rows).astype(BF16).astype(F32)

        score_new = s_ref[0, pl.ds(n_pages, 1), 0:1]
        sel = _selected(score_new, jnp.full((1, 1), n_pages * PAGE_SIZE, jnp.int32), thr, jmax, take_all)
        s = jnp.sum(q.astype(F32) * per_head(knew_ref), axis=-1, keepdims=True)
        s = jnp.where(sel, s, -jnp.inf)
        vexp = per_head(vnew_ref)
        fold([s], [lambda pr: pr.astype(BF16).astype(F32) * vexp])
        o_ref[0] = acc_ref[...] / l_ref[...]


def _sample_attn(page_table, q3, scores3, thr, jmax, take_all, knew3, vnew3, cache_k, cache_v, layer):
    DB, n_pages = page_table.shape
    width = N_KV_HEADS * HEAD_DIM
    npp = _pages_per_step(n_pages, 32)
    bsel = lambda *shape: pl.BlockSpec((1,) + shape, lambda b, p, pt: (b,) + (0,) * len(shape))
    whole = pl.BlockSpec((DB, LANES), lambda b, p, pt: (0, 0))
    n_phys = cache_k.shape[1]
    rows = PAGE_SIZE * N_KV_HEADS
    cache_k = cache_k.reshape(-1, rows, HEAD_DIM)
    cache_v = cache_v.reshape(-1, rows, HEAD_DIM)
    pages = [pl.BlockSpec((1, rows, HEAD_DIM),
                          lambda b, p, pt, j=j: (layer * n_phys + pt[b, p * npp + j], 0, 0)) for j in range(npp)]
    return pl.pallas_call(
        functools.partial(_sample_attn_kernel, n_pages=n_pages),
        grid_spec=pltpu.PrefetchScalarGridSpec(
            num_scalar_prefetch=1,
            grid=(DB, n_pages // npp),
            in_specs=[bsel(N_HEADS, HEAD_DIM), bsel(n_pages + 1, PAGE_SIZE), whole, whole, whole,
                      bsel(1, width), bsel(1, width)] + pages + pages,
            out_specs=bsel(N_HEADS, HEAD_DIM),
            scratch_shapes=[pltpu.VMEM((N_HEADS, 1), F32), pltpu.VMEM((N_HEADS, 1), F32),
                            pltpu.VMEM((N_HEADS, HEAD_DIM), F32)],
        ),
        out_shape=jax.ShapeDtypeStruct((DB, N_HEADS, HEAD_DIM), F32),
        compiler_params=_cparams("parallel", "arbitrary"),
        name="sample_attn",
    )(page_table, q3, scores3, thr, jmax, take_all, knew3, vnew3, *([cache_k] * npp), *([cache_v] * npp))


CARRY = 8
MXU_DIM = 256
GDN_GROUP = MXU_DIM // CHUNK


def _spread(a, row_head, hg):
    return jnp.concatenate([jnp.where(row_head == i, a, 0.0) for i in range(hg)], axis=1)


def _gdn_prep_kernel(x_ref, w_ref, buf_ref, q_ref, k_ref, v_ref, conv_ref, xpad_ref):
    tt = x_ref.shape[1]
    t = pl.program_id(1)
    lo = CARRY - (CONV_W - 1)

    @pl.when(t == 0)
    def _():
        xpad_ref[lo:CARRY, :] = buf_ref[0]

    xpad_ref[CARRY:CARRY + tt, :] = x_ref[0]
    nh = GDN_HEADS
    for c in range(CONV_DIM // LANES):
        sl = slice(c * LANES, (c + 1) * LANES)
        y = w_ref[0:1, sl] * xpad_ref[lo:lo + tt, sl]
        for j in range(1, CONV_W):
            y = y + w_ref[j:j + 1, sl] * xpad_ref[lo + j:lo + j + tt, sl]
        y = _silu(y)
        if c < 2 * nh:
            y = y * lax.rsqrt(jnp.sum(y * y, axis=-1, keepdims=True) + NORM_EPS)
        if c < nh:
            q_ref[0, :, sl] = y * (GDN_DK ** -0.5)
        elif c < 2 * nh:
            k_ref[0, :, slice((c - nh) * LANES, (c - nh + 1) * LANES)] = y
        else:
            v_ref[0, :, slice((c - 2 * nh) * LANES, (c - 2 * nh + 1) * LANES)] = y
    last = xpad_ref[lo + tt:CARRY + tt, :]
    xpad_ref[lo:CARRY, :] = last

    @pl.when(t == pl.num_programs(1) - 1)
    def _():
        conv_ref[0] = last


def _gdn_prep(u_gdn3, conv_w, buf, tt):
    B, T, _ = u_gdn3.shape
    w = SIZES['qkv_b']
    hd = GDN_HEADS * GDN_DK
    out = jax.ShapeDtypeStruct((B, T, hd), F32)
    ospec = pl.BlockSpec((1, tt, hd), lambda b, t: (b, t, 0))
    return pl.pallas_call(
        _gdn_prep_kernel,
        grid=(B, T // tt),
        in_specs=[
            pl.BlockSpec((1, tt, w), lambda b, t: (b, t, 0)),
            pl.BlockSpec((CONV_W, w), lambda b, t: (0, 0)),
            pl.BlockSpec((1, CONV_W - 1, w), lambda b, t: (b, 0, 0)),
        ],
        out_specs=[ospec, ospec, ospec, pl.BlockSpec((1, CONV_W - 1, w), lambda b, t: (b, 0, 0))],
        out_shape=[out, out, out, jax.ShapeDtypeStruct((B, CONV_W - 1, w), F32)],
        scratch_shapes=[pltpu.VMEM((CARRY + tt, w), F32)],
        compiler_params=_cparams("parallel", "arbitrary"),
        name="gdn_prep",
    )(u_gdn3, conv_w, buf)


def _gdn_chunk_kernel(q_ref, k_ref, v_ref, z_ref, sm_ref, alog_ref, dtb_ref, gn_ref, s0_ref,
                      o_ref, sout_ref, state_ref, *, t_valid):
    C = q_ref.shape[1]
    c = pl.program_id(1)

    @pl.when(c == 0)
    def _():
        state_ref[...] = s0_ref[0]

    H = GDN_HEADS
    a0, b0 = SMALL_OFF['a_b'], SMALL_OFF['beta_b']
    live = (c * C + lax.broadcasted_iota(jnp.int32, (C, H), 0)) < t_valid
    xs = sm_ref[0, :, a0:a0 + H] + dtb_ref[...]
    softplus = jnp.maximum(xs, 0.0) + jnp.log1p(jnp.exp(-jnp.abs(xs)))
    g_all = jnp.where(live, -jnp.exp(alog_ref[...]) * softplus, 0.0)
    beta_all = jnp.where(live, jax.nn.sigmoid(sm_ref[0, :, b0:b0 + H]), 0.0)
    tri_f = (lax.broadcasted_iota(jnp.int32, (C, C), 0) >= lax.broadcasted_iota(jnp.int32, (C, C), 1)).astype(F32)
    eye_h = (lax.broadcasted_iota(jnp.int32, (H, H), 0) == lax.broadcasted_iota(jnp.int32, (H, H), 1)).astype(F32)
    gcum = _dot_hi(tri_f, g_all)
    gcum_t = _dot_nt_hi(eye_h, gcum)

    HG = GDN_GROUP
    R = HG * C
    ri = lax.broadcasted_iota(jnp.int32, (R, R), 0)
    ci = lax.broadcasted_iota(jnp.int32, (R, R), 1)
    same_head = (ri // C) == (ci // C)
    mask_incl = same_head & (ri >= ci)
    mask_strict = same_head & (ri > ci)
    eye = (ri == ci).astype(F32)
    row_head = lax.broadcasted_iota(jnp.int32, (R, 1), 0) // C
    row_head2 = jnp.concatenate([row_head, row_head], axis=0)

    groups = []
    for grp in range(H // HG):
        heads = range(grp * HG, (grp + 1) * HG)
        sls = [slice(h * GDN_DK, (h + 1) * GDN_DK) for h in heads]
        rows = lambda ref: jnp.concatenate([ref[0, :, sl] for sl in sls], axis=0)
        cols = lambda a: jnp.concatenate([a[:, h:h + 1] for h in heads], axis=0)
        q, k, v = rows(q_ref), rows(k_ref), rows(v_ref)
        beta = cols(beta_all)
        gcol = cols(gcum)
        grow = jnp.concatenate([gcum_t[h:h + 1, :] for h in heads], axis=1)
        glast = jnp.concatenate([jnp.broadcast_to(gcum[C - 1:C, h:h + 1], (C, 1)) for h in heads], axis=0)
        decay = jnp.where(mask_incl, jnp.exp(jnp.where(mask_incl, gcol - grow, 0.0)), 0.0)
        kb = k * beta
        kk_qk = _dot_nt(jnp.concatenate([kb, q], axis=0).astype(BF16), k.astype(BF16))
        nmat = jnp.where(mask_strict, kk_qk[:R] * decay, 0.0)
        groups.append(dict(
            heads=heads, sls=sls, nmat=nmat.astype(BF16), xinv=eye - nmat, qk=(kk_qk[R:] * decay).astype(BF16),
            rhs=jnp.concatenate([v * beta, kb * jnp.exp(gcol)], axis=1).astype(BF16),
            q_dec=q * jnp.exp(gcol), k_dec=k * jnp.exp(glast - gcol)))

    for _ in range(max(C - 1, 1).bit_length() - 1):
        for g in groups:
            g['resid'] = (eye - g['xinv']) - _dot(g['nmat'], g['xinv'].astype(BF16))
        for g in groups:
            g['xinv'] = g['xinv'] + _dot_bf16(g['xinv'], g['resid'])
    for g in groups:
        g['x'] = _dot(g['xinv'].astype(BF16), g['rhs'])
    for grp, g in enumerate(groups):
        u, w = g['x'][:, :GDN_DV], g['x'][:, GDN_DV:]
        s_stack = state_ref[grp * HG:(grp + 1) * HG].reshape(HG * GDN_DK, GDN_DV)
        w_q = jnp.concatenate([w, g['q_dec']], axis=0)
        ws_qs = _dot(_spread(w_q, row_head2, HG).astype(BF16), s_stack.astype(BF16))
        g['v_new'] = u - ws_qs[:R]
        g['o'] = ws_qs[R:]
    for g in groups:
        g['o'] = g['o'] + _dot(g['qk'], g['v_new'].astype(BF16))
        g['s_add'] = _dot(g['k_dec'].T.astype(BF16), _spread(g['v_new'], row_head, HG).astype(BF16))
    for g in groups:
        o = g['o']
        on = o * lax.rsqrt(jnp.mean(o * o, axis=-1, keepdims=True) + NORM_EPS) * gn_ref[...]
        for a, h in enumerate(g['heads']):
            state_ref[h] = (state_ref[h] * jnp.exp(gcum[C - 1:C, h:h + 1])
                            + g['s_add'][:, a * GDN_DV:(a + 1) * GDN_DV])
            o_ref[0, :, g['sls'][a]] = on[a * C:(a + 1) * C] * _silu(z_ref[0, :, g['sls'][a]])

    @pl.when(c == pl.num_programs(1) - 1)
    def _():
        sout_ref[0] = state_ref[...]


def _gdn_chunks(qn, kn, vv, u_gate3, small3, a_log, dt_bias, gn, s0, t_valid):
    B, Tp, hd = qn.shape
    nc = Tp // CHUNK
    blk = pl.BlockSpec((1, CHUNK, hd), lambda b, c: (b, c, 0))
    vec = lambda w: pl.BlockSpec((1, w), lambda b, c: (0, 0))
    st = pl.BlockSpec((1, GDN_HEADS, GDN_DK, GDN_DV), lambda b, c: (b, 0, 0, 0))
    return pl.pallas_call(
        functools.partial(_gdn_chunk_kernel, t_valid=t_valid),
        grid=(B, nc),
        in_specs=[blk, blk, blk,
                  pl.BlockSpec((1, CHUNK, hd), lambda b, c, o=GATE_OFF['z_b'] // hd: (b, c, o)),
                  pl.BlockSpec((1, CHUNK, LANES), lambda b, c: (b, c, 0)),
                  vec(GDN_HEADS), vec(GDN_HEADS), vec(GDN_DV), st],
        out_specs=[blk, st],
        out_shape=[jax.ShapeDtypeStruct((B, Tp, hd), F32),
                   jax.ShapeDtypeStruct((B, GDN_HEADS, GDN_DK, GDN_DV), F32)],
        scratch_shapes=[pltpu.VMEM((GDN_HEADS, GDN_DK, GDN_DV), F32)],
        compiler_params=_cparams("parallel", "arbitrary"),
        name="gdn_chunks",
    )(qn, kn, vv, u_gate3, small3, a_log, dt_bias, gn, s0)


def _merge_kernel(x_ref, oa_ref, ob_ref, ga_ref, gb_ref, wo_ref, g_ref, b_ref, o_ref):
    merged = jax.nn.sigmoid(ga_ref[...]) * oa_ref[...] + jax.nn.sigmoid(gb_ref[...]) * ob_ref[...]
    y = DN_ALPHA * x_ref[...] + _dot(merged.astype(BF16), wo_ref[...])
    o_ref[...] = _layer_norm(y, g_ref[...], b_ref[...])


def _merge_proj_ln(x, o_a, o_b, u_gate, w_o, g, b, tm):
    M, D = x.shape
    row = pl.BlockSpec((tm, D), lambda i: (i, 0))
    col = lambda name: pl.BlockSpec((tm, D), lambda i, o=GATE_OFF[name] // D: (i, o))
    vec = pl.BlockSpec((1, D), lambda i: (0, 0))
    return pl.pallas_call(
        _merge_kernel,
        grid=(M // tm,),
        in_specs=[row, row, row, col('gate_a'), col('gate_b'),
                  pl.BlockSpec((D, D), lambda i: (0, 0)), vec, vec],
        out_specs=row,
        out_shape=jax.ShapeDtypeStruct((M, D), F32),
        compiler_params=_cparams("parallel"),
        name="merge_proj_ln",
    )(x, o_a, o_b, u_gate, u_gate, w_o, g, b)


def _tiles(M):
    if M % 1024 == 0:
        return 512, 1024, 256
    return M, M, M


def _layer(x, B, T, pos, wts, conv_buf, ssm0, dsa_fn):
    M = B * T
    tm, tp, te = _tiles(M)
    row = lambda a: a.reshape(1, -1)
    x1 = _ffn_ln(x, wts['ffn1_g'], wts['ffn1_u'], wts['ffn1_d'], row(wts['ln1_g']), row(wts['ln1_b']),
                 tm, 512)
    u_attn = _proj(x1, wts['w_attn'], tp, 1024)
    u_gdn = _proj(x1, wts['w_gdn'], tp, 1024)
    u_gate = _proj(x1, wts['w_gate'], tp, 1024)
    u_small = _proj(x1, wts['w_small'], tp, LANES)

    pos_rows = pos if T > 1 else jnp.broadcast_to(pos, (M,))
    tabs = (_rope_tables(pos_rows, ROPE_DIM, HEAD_DIM) + _rope_tables(pos_rows, IDX_ROPE_DIM, IDX_DIM)
            + _rope_tables(pos_rows, IDX_ROPE_DIM, IDX_DIM, live_lanes=IDX_DIM))
    q_bf, k_rot, k_bf, v_bf, qi_bf, small_rot, small_bf = _rope_prep(u_attn, u_small, tabs, te)
    v_rows = u_attn[:, ATTN_OFF['v_a']:ATTN_OFF['v_a'] + SIZES['v_a']]
    o_a = dsa_fn(v_rows, q_bf, k_rot, k_bf, v_bf, qi_bf, small_rot, small_bf)

    qn, kn, vv, conv_new = _gdn_prep(u_gdn.reshape(B, T, -1), wts['conv_w'], conv_buf, min(T, 256))
    pad = (-T) % CHUNK
    pad3 = lambda a: jnp.pad(a, ((0, 0), (0, pad), (0, 0))) if pad else a
    o_b, ssm_new = _gdn_chunks(pad3(qn), pad3(kn), pad3(vv), pad3(u_gate.reshape(B, T, -1)),
                               pad3(u_small.reshape(B, T, LANES)),
                               row(wts['a_log']), row(wts['dt_bias']), row(wts['gdn_norm_g']), ssm0, T)
    o_b = o_b[:, :T].reshape(M, D_MODEL)

    x2 = _merge_proj_ln(x1, o_a, o_b, u_gate, wts['w_o'], row(wts['ln2_g']), row(wts['ln2_b']), te)
    y = _ffn_ln(x2, wts['ffn2_g'], wts['ffn2_u'], wts['ffn2_d'], row(wts['ln3_g']), row(wts['ln3_b']),
                tm, 512)
    ki_rows = small_rot[:, SMALL_OFF['k_idx']:SMALL_OFF['k_idx'] + IDX_DIM]
    return y, (k_rot, v_rows, ki_rows, ssm_new, conv_new)


def _split_w_in(w_in):
    offs = dict(zip([nm for nm, _ in IN_SPLITS], np.cumsum([0] + [n for _, n in IN_SPLITS])))
    w_t = jnp.swapaxes(w_in, 0, 1)

    def span(order):
        lo = offs[order[0]]
        hi = offs[order[-1]] + SIZES[order[-1]]
        assert hi - lo == sum(SIZES[nm] for nm in order)
        return w_t[lo:hi].astype(BF16)

    w_small = jnp.concatenate([w_t[offs[nm]:offs[nm] + SIZES[nm]] for nm in SMALL_ORDER], axis=0)
    w_small = jnp.pad(w_small, ((0, LANES - SMALL_USED), (0, 0))).astype(BF16)
    return span(ATTN_ORDER), span(GDN_ORDER), span(GATE_ORDER), w_small


def kernel(x_prompt, x_sample, cache_k, cache_v, cache_idx_k, state_ssm, state_conv, page_table, ffn1_w_gate, ffn1_w_up, ffn1_w_down, ln1_g, ln1_b, w_in, conv_w, a_log, dt_bias, gdn_norm_g, w_o, ln2_g, ln2_b, ffn2_w_gate, ffn2_w_up, ffn2_w_down, ln3_g, ln3_b):
    B, S, _ = x_prompt.shape
    DB, T, _ = x_sample.shape
    assert T == 1, "the sample path handles one new token per sequence"
    n_pages = page_table.shape[1]
    yp = x_prompt.reshape(B * S, D_MODEL)
    ys = x_sample.reshape(DB * T, D_MODEL)
    outs_p, outs_s = [], []
    for l in range(ffn1_w_gate.shape[0]):
        w_attn, w_gdn, w_gate, w_small = _split_w_in(w_in[l])
        wts = dict(
            ffn1_g=ffn1_w_gate[l].astype(BF16), ffn1_u=ffn1_w_up[l].astype(BF16), ffn1_d=ffn1_w_down[l].astype(BF16),
            ffn2_g=ffn2_w_gate[l].astype(BF16), ffn2_u=ffn2_w_up[l].astype(BF16), ffn2_d=ffn2_w_down[l].astype(BF16),
            ln1_g=ln1_g[l], ln1_b=ln1_b[l], ln2_g=ln2_g[l], ln2_b=ln2_b[l], ln3_g=ln3_g[l], ln3_b=ln3_b[l],
            w_attn=w_attn, w_gdn=w_gdn, w_gate=w_gate, w_small=w_small, w_o=w_o[l].astype(BF16), conv_w=conv_w[l],
            a_log=a_log[l], dt_bias=dt_bias[l], gdn_norm_g=gdn_norm_g[l],
        )

        def dsa_p(v_rows, q_bf, k_rot, k_bf, v_bf, qi_bf, small_rot, small_bf):
            return _dsa_prompt(q_bf, qi_bf, small_rot, k_bf, v_bf, small_bf, B, S, min(S, 256))

        def dsa_s(v_rows, q_bf, k_rot, k_bf, v_bf, qi_bf, small_rot, small_bf, l=l):
            w0 = SMALL_OFF['w_idx']
            w3 = small_rot[:, w0:w0 + IDX_HEADS].reshape(DB, IDX_HEADS, 1)
            qi3 = qi_bf.reshape(DB, IDX_HEADS, IDX_DIM)
            knew_i = small_bf[:, :IDX_DIM].reshape(DB, IDX_DIM, 1)
            width = N_KV_HEADS * HEAD_DIM
            past, new = _sample_scores(page_table, qi3, w3, knew_i, jnp.swapaxes(cache_idx_k, 2, 3), l)
            scores3 = jnp.concatenate([past, new], axis=1)
            n_keys = n_pages * PAGE_SIZE + T
            ktop = min(TOPK_MAX, n_keys // 4)
            thr, jmax, take_all = _sample_threshold(scores3.reshape(DB, -1), ktop, n_keys)
            o = _sample_attn(page_table, q_bf.reshape(DB, N_HEADS, HEAD_DIM), scores3, thr, jmax, take_all,
                             k_rot.reshape(DB, 1, width), v_rows.reshape(DB, 1, width), cache_k, cache_v, l)
            return o.reshape(DB, D_MODEL)

        conv0 = jnp.zeros((B, CONV_W - 1, CONV_DIM), F32)
        ssm_zero = jnp.zeros((B, GDN_HEADS, GDN_DK, GDN_DV), F32)
        yp, st_p = _layer(yp, B, S, jnp.arange(S, dtype=jnp.int32), wts, conv0, ssm_zero, dsa_p)
        past_len = n_pages * PAGE_SIZE
        ys, st_s = _layer(ys, DB, T, past_len + jnp.arange(T, dtype=jnp.int32), wts, state_conv[l], state_ssm[l], dsa_s)
        outs_p.append(st_p)
        outs_s.append(st_s)

    def stack(outs, nb, nt):
        d = len(outs)
        k, v, ki, ssm, conv = [a[0][None] if d == 1 else jnp.stack(a) for a in zip(*outs)]
        return (k.reshape(d, nb, nt, N_KV_HEADS, HEAD_DIM), v.reshape(d, nb, nt, N_KV_HEADS, HEAD_DIM),
                ki.reshape(d, nb, nt, IDX_DIM), ssm, conv)

    return (yp.reshape(B, S, D_MODEL), ys.reshape(DB, T, D_MODEL)) + stack(outs_p, B, S) + stack(outs_s, DB, T)
```
